```python
import math
import jax, jax.numpy as jnp
from jax import lax
import numpy as np

D_MODEL = 2048
BATCH = 4
SEQ = 2048
DEPTH = 4
DEC_BATCH = 8
DEC_SEQ = 8
PAST_LEN = 16384
PAGE_SIZE = 128

HEAD_DIM = 64
ATTN_WIDTH = D_MODEL // 2
N_HEADS_A = ATTN_WIDTH // HEAD_DIM
N_KV_HEADS = 4
GQA = N_HEADS_A // N_KV_HEADS
KV_WIDTH = N_KV_HEADS * HEAD_DIM
CMP_BLOCK = 32
CMP_STRIDE = 16
SEL_BLOCK = 64
SEL_TOPK = 16
WINDOW = 512
SEL_Q_BLOCK = 32
WIN_Q_BLOCK = 128
NUM_BUCKETS = 32
MAX_DISTANCE = 128
SSM_WIDTH = D_MODEL // 4
SSM_GROUP = 16
SSM_GROUPS = SSM_WIDTH // SSM_GROUP
SSM_STATE = 64
GMLP_WIDTH = D_MODEL // 4
GMLP_GROUPS = 4
GMLP_GROUP_DIM = GMLP_WIDTH // GMLP_GROUPS
CHUNK = 128
MIX_WIDTH = ATTN_WIDTH + SSM_WIDTH + GMLP_WIDTH
IN_SPLITS = (ATTN_WIDTH, 6 * KV_WIDTH, 3 * N_HEADS_A, SSM_WIDTH, GMLP_WIDTH, GMLP_WIDTH)
IN_WIDTH = ATTN_WIDTH + 6 * KV_WIDTH + 3 * N_HEADS_A + SSM_WIDTH + 2 * GMLP_WIDTH
MOE_GROUPS = 4
EXPERTS_PER_GROUP = 4
N_EXPERTS = MOE_GROUPS * EXPERTS_PER_GROUP
INNER_TOP_K = 2
D_EXPERT = D_MODEL // 4

SCALE = HEAD_DIM ** -0.5
EPS = 1e-6
NEG_INF = -1e30

kernel_name = 'hymba_nsa_s5_gmlp_hmoe_decode_step'


def rms_norm(x, g):
    xf = x.astype(jnp.float32)
    y = xf * lax.rsqrt(jnp.mean(xf * xf, axis=-1, keepdims=True) + EPS)
    return (y * g.astype(jnp.float32)).astype(x.dtype)


def masked_softmax(logits, mask):
    l = jnp.where(mask, logits.astype(jnp.float32), NEG_INF)
    e = jnp.where(mask, jnp.exp(l - jnp.max(l, axis=-1, keepdims=True)), 0.0)
    return e / jnp.maximum(jnp.sum(e, axis=-1, keepdims=True), 1e-30)


def t5_bucket(dist):
    n = jnp.maximum(dist, 0)
    max_exact = NUM_BUCKETS // 2
    nf = jnp.maximum(n, max_exact).astype(jnp.float32)
    large = max_exact + (jnp.log(nf / max_exact) / math.log(MAX_DISTANCE / max_exact)
                         * (NUM_BUCKETS - max_exact)).astype(jnp.int32)
    return jnp.where(n < max_exact, n, jnp.minimum(large, NUM_BUCKETS - 1))


def to_blocks(x, n):
    return jnp.moveaxis(x.reshape(x.shape[0], n, x.shape[1] // n, *x.shape[2:]), 1, 0)


def from_blocks(y):
    y = jnp.moveaxis(y, 0, 1)
    return y.reshape(y.shape[0], y.shape[1] * y.shape[2], *y.shape[3:])


def split_projection(h, w_in):
    Bn, T = h.shape[:2]
    offs = np.cumsum(IN_SPLITS)[:-1].tolist()
    q, kv, gates, u_ssm, u_g, v_g = jnp.split(h @ w_in, offs, axis=-1)
    q = q.reshape(Bn, T, N_HEADS_A, HEAD_DIM)
    kv = kv.reshape(Bn, T, 3, N_KV_HEADS, 2, HEAD_DIM)
    gates = jax.nn.sigmoid(gates.astype(jnp.float32)).reshape(Bn, T, 3, N_KV_HEADS, GQA, 1)
    return q, kv, gates, u_ssm, jax.nn.gelu(u_g), jax.nn.gelu(v_g)


def nsa_rows(q, kv, qk_norm):
    Bn, T = q.shape[:2]
    qg = rms_norm(q, qk_norm[0]).reshape(Bn, T, N_KV_HEADS, GQA, HEAD_DIM)

    def normed(branch, g):
        return jnp.stack([rms_norm(kv[:, :, branch, :, 0], g), kv[:, :, branch, :, 1]], axis=-2)

    return qg, kv[:, :, 0], normed(1, qk_norm[2]), normed(2, qk_norm[3])


def compress(rows, pos, w1, w2):
    Bn, L = rows.shape[:2]
    n_cmp = (L - CMP_BLOCK) // CMP_STRIDE + 1
    idx = jnp.arange(n_cmp)[:, None] * CMP_STRIDE + jnp.arange(CMP_BLOCK)[None, :]
    blocks = rows[:, idx] + pos[:, None, :]
    flat = jnp.swapaxes(blocks, 2, 3).reshape(Bn, n_cmp, N_KV_HEADS, CMP_BLOCK * HEAD_DIM)
    return jax.nn.gelu(flat @ w1) @ w2


def compress_kv(cmp_rows, lp):
    kc = rms_norm(compress(cmp_rows[..., 0, :], lp['cmp_pos'][0], lp['cmp_w1'][0], lp['cmp_w2'][0]),
                  lp['qk_norm'][1])
    vc = compress(cmp_rows[..., 1, :], lp['cmp_pos'][1], lp['cmp_w1'][1], lp['cmp_w2'][1])
    return kc, vc


def cmp_attention(qg, kc, vc, qpos, table):
    Q, n = qpos.shape[0], kc.shape[1]
    dist = qpos[:, None] - (jnp.arange(n) * CMP_STRIDE + CMP_BLOCK - 1)[None, :]
    bias = jnp.transpose(table[t5_bucket(dist)], (2, 0, 1)).reshape(N_KV_HEADS, GQA, Q, n)
    logits = jnp.einsum('bqhgd,bnhd->bhgqn', qg, kc, preferred_element_type=jnp.float32) * SCALE + bias
    p = masked_softmax(logits, dist >= 0)
    o = jnp.einsum('bhgqn,bnhd->bqhgd', p.astype(vc.dtype), vc)
    return o, p


def select_blocks(p_cmp, qpos, n_sel):
    n_cmp = p_cmp.shape[-1]
    c_start = jnp.arange(n_cmp)[:, None] * CMP_STRIDE
    s_start = jnp.arange(n_sel)[None, :] * SEL_BLOCK
    overlap = ((c_start < s_start + SEL_BLOCK) & (c_start + CMP_BLOCK > s_start)).astype(jnp.float32)
    score = jnp.einsum('bhgqn,ns->bhqs', p_cmp, overlap)
    blk = jnp.arange(n_sel)[None, :]
    cur = (qpos // SEL_BLOCK)[:, None]
    forced = (blk == 0) | (blk == cur) | (blk == cur - 1)
    future = blk * SEL_BLOCK > qpos[:, None]
    score = jnp.where(forced, 1e6, jnp.where(future, -1.0, score))
    _, idx = lax.top_k(score, min(SEL_TOPK, n_sel))
    return idx


def sel_attention(qg, qpos, idx, gather, table):
    Bn, Q = qg.shape[:2]
    spos = (idx[..., None] * SEL_BLOCK + jnp.arange(SEL_BLOCK)).reshape(Bn, N_KV_HEADS, Q, -1)
    rows = gather(spos)
    dist = qpos[:, None] - spos
    tab = table.reshape(NUM_BUCKETS, N_KV_HEADS, GQA)
    bias = tab[t5_bucket(dist)[:, :, None], jnp.arange(N_KV_HEADS)[:, None, None, None],
               jnp.arange(GQA)[:, None, None]]
    logits = jnp.einsum('bqhgd,bhqnd->bhgqn', qg, rows[..., 0, :],
                        preferred_element_type=jnp.float32) * SCALE + bias
    p = masked_softmax(logits, (dist >= 0)[:, :, None])
    return jnp.einsum('bhgqn,bhqnd->bqhgd', p.astype(rows.dtype), rows[..., 1, :])


def win_attention(qg, rows, qpos, kpos, table):
    Q, N = qpos.shape[0], kpos.shape[0]
    dist = qpos[:, None] - kpos[None, :]
    mask = (dist >= 0) & (dist <= WINDOW) & (kpos >= 0)[None, :]
    bias = jnp.transpose(table[t5_bucket(dist)], (2, 0, 1)).reshape(N_KV_HEADS, GQA, Q, N)
    logits = jnp.einsum('bqhgd,bnhd->bhgqn', qg, rows[..., 0, :],
                        preferred_element_type=jnp.float32) * SCALE + bias
    p = masked_softmax(logits, mask)
    return jnp.einsum('bhgqn,bnhd->bqhgd', p.astype(rows.dtype), rows[..., 1, :])


def combine_branches(gates, o_cmp, o_sel, o_win):
    g = gates.astype(o_cmp.dtype)
    o = g[:, :, 0] * o_cmp + g[:, :, 1] * o_sel + g[:, :, 2] * o_win
    return o.reshape(o.shape[0], o.shape[1], ATTN_WIDTH)


def nsa_prompt(q, kv, gates, lp, table):
    Bn, T = q.shape[:2]
    qpos = jnp.arange(T)
    qg, cmp_rows, sel_rows, win_rows = nsa_rows(q, kv, lp['qk_norm'])
    kc, vc = compress_kv(cmp_rows, lp)
    o_cmp, p_cmp = cmp_attention(qg, kc, vc, qpos, table)
    idx = select_blocks(p_cmp, qpos, -(-T // SEL_BLOCK))
    bidx = jnp.arange(Bn)[:, None, None, None]
    hidx = jnp.arange(N_KV_HEADS)[None, :, None, None]

    def sel_block(args):
        q_blk, qp, idx_blk = args
        return sel_attention(q_blk, qp, idx_blk, lambda spos: sel_rows[bidx, spos, hidx], table)

    nb = T // SEL_Q_BLOCK
    o_sel = from_blocks(lax.map(sel_block, (to_blocks(qg, nb), qpos.reshape(nb, SEL_Q_BLOCK),
                                            jnp.moveaxis(idx.reshape(Bn, N_KV_HEADS, nb, SEL_Q_BLOCK, -1), 2, 0))))
    nw = T // WIN_Q_BLOCK
    band = WINDOW + WIN_Q_BLOCK
    win_pad = jnp.pad(win_rows, ((0, 0), (WINDOW, 0), (0, 0), (0, 0), (0, 0)))

    def win_block(args):
        q_blk, start = args
        rows = lax.dynamic_slice_in_dim(win_pad, start, band, axis=1)
        return win_attention(q_blk, rows, start + jnp.arange(WIN_Q_BLOCK),
                             start - WINDOW + jnp.arange(band), table)

    o_win = from_blocks(lax.map(win_block, (to_blocks(qg, nw), jnp.arange(nw) * WIN_Q_BLOCK)))
    o = combine_branches(gates, o_cmp, o_sel, o_win)
    return o, (cmp_rows, sel_rows, win_rows[:, T - min(WINDOW, T):])


def nsa_sample(q, kv, gates, lp, table, cmp_pool, sel_pool, win_buf, page_table, layer):
    Bn, S = q.shape[:2]
    P = page_table.shape[1] * PAGE_SIZE
    qpos = P + jnp.arange(S)
    qg, cmp_new, sel_new, win_new = nsa_rows(q, kv, lp['qk_norm'])
    past_cmp = cmp_pool[page_table, layer].reshape(Bn, P, N_KV_HEADS, 2, HEAD_DIM)
    kc, vc = compress_kv(jnp.concatenate([past_cmp.astype(cmp_new.dtype), cmp_new], axis=1), lp)
    o_cmp, p_cmp = cmp_attention(qg, kc, vc, qpos, table)
    idx = select_blocks(p_cmp, qpos, -(-(P + S) // SEL_BLOCK))
    bidx = jnp.arange(Bn)[:, None, None, None]
    hidx = jnp.arange(N_KV_HEADS)[None, :, None, None]

    def gather(spos):
        past_pos = jnp.minimum(spos, P - 1)
        phys = page_table[bidx, past_pos // PAGE_SIZE]
        past = sel_pool[phys, layer, past_pos % PAGE_SIZE, hidx]
        new = sel_new[bidx, jnp.clip(spos - P, 0, S - 1), hidx]
        return jnp.where((spos >= P)[..., None, None], new, past.astype(new.dtype))

    o_sel = sel_attention(qg, qpos, idx, gather, table)
    n_buf = win_buf.shape[1]
    win_all = jnp.concatenate([win_buf.astype(win_new.dtype), win_new], axis=1)
    o_win = win_attention(qg, win_all, qpos, P - n_buf + jnp.arange(n_buf + S), table)
    o = combine_branches(gates, o_cmp, o_sel, o_win)
    return o, (cmp_new, sel_new, win_all[:, S:])


def s5_mixer(u, h0_re, h0_im, lp):
    f32 = jnp.float32
    Bn, T = u.shape[:2]
    a_re, a_im = lp['ssm_a_re'].astype(f32), lp['ssm_a_im'].astype(f32)
    dt = jnp.exp(lp['ssm_log_dt'].astype(f32))[:, None]
    mag = jnp.exp(dt * a_re)
    ab_re, ab_im = mag * jnp.cos(dt * a_im), mag * jnp.sin(dt * a_im)
    den = a_re * a_re + a_im * a_im
    f_re = ((ab_re - 1.0) * a_re + ab_im * a_im) / den
    f_im = (ab_im * a_re - (ab_re - 1.0) * a_im) / den
    b_re, b_im = lp['ssm_b_re'].astype(f32), lp['ssm_b_im'].astype(f32)
    bb_re = f_re[..., None] * b_re - f_im[..., None] * b_im
    bb_im = f_re[..., None] * b_im + f_im[..., None] * b_re
    ug = u.astype(f32).reshape(Bn, T, SSM_GROUPS, SSM_GROUP)
    x_re = jnp.einsum('btgc,gnc->btgn', ug, bb_re)
    x_im = jnp.einsum('btgc,gnc->btgn', ug, bb_im)
    h0_re, h0_im = h0_re.astype(f32), h0_im.astype(f32)
    x_re = x_re.at[:, 0].add(ab_re * h0_re - ab_im * h0_im)
    x_im = x_im.at[:, 0].add(ab_re * h0_im + ab_im * h0_re)

    def combine(e1, e2):
        a1r, a1i, x1r, x1i = e1
        a2r, a2i, x2r, x2i = e2
        return (a1r * a2r - a1i * a2i, a1r * a2i + a1i * a2r,
                a2r * x1r - a2i * x1i + x2r, a2r * x1i + a2i * x1r + x2i)

    elems = (jnp.broadcast_to(ab_re, x_re.shape), jnp.broadcast_to(ab_im, x_re.shape), x_re, x_im)
    _, _, h_re, h_im = lax.associative_scan(combine, elems, axis=1)
    c_re, c_im = lp['ssm_c_re'].astype(f32), lp['ssm_c_im'].astype(f32)
    y = (jnp.einsum('btgn,gcn->btgc', h_re, c_re) - jnp.einsum('btgn,gcn->btgc', h_im, c_im)
         + lp['ssm_d'].astype(f32).reshape(SSM_GROUPS, SSM_GROUP) * ug)
    y = jax.nn.gelu(y.reshape(Bn, T, SSM_WIDTH))
    y = y * jax.nn.sigmoid(y @ lp['ssm_glu_w'].astype(f32) + lp['ssm_glu_b'].astype(f32))
    return y.astype(u.dtype), h_re[:, -1], h_im[:, -1]


def gmlp_mixer(u, v, lp):
    Bn, T = u.shape[:2]
    vn = rms_norm(v.reshape(Bn, T, GMLP_GROUPS, GMLP_GROUP_DIM),
                  lp['gmlp_norm'].reshape(GMLP_GROUPS, GMLP_GROUP_DIM))
    n_chunks = -(-T // CHUNK)
    vc = jnp.pad(vn, ((0, 0), (0, n_chunks * CHUNK - T), (0, 0), (0, 0)))
    vc = vc.reshape(Bn, n_chunks, CHUNK, GMLP_GROUPS, GMLP_GROUP_DIM)
    w = lp['gmlp_ws'] * jnp.tril(jnp.ones((CHUNK, CHUNK), lp['gmlp_ws'].dtype))
    mixed = jnp.einsum('gts,bcsgd->bctgd', w, vc) + lp['gmlp_bs'].T[None, None, :, :, None]
    mixed = mixed.reshape(Bn, n_chunks * CHUNK, GMLP_GROUPS, GMLP_GROUP_DIM)[:, :T]
    out = (u.reshape(Bn, T, GMLP_GROUPS, GMLP_GROUP_DIM) * mixed).reshape(Bn, T, GMLP_WIDTH)
    v_rows = vn.reshape(Bn, T, GMLP_WIDTH)[:, (T - 1) // CHUNK * CHUNK:]
    return out, v_rows


def merge_mixers(o_a, o_b, o_c, lp):
    g = lp['out_norm']
    cat = jnp.concatenate([rms_norm(o_a, g[:ATTN_WIDTH]),
                           rms_norm(o_b, g[ATTN_WIDTH:ATTN_WIDTH + SSM_WIDTH]),
                           rms_norm(o_c, g[ATTN_WIDTH + SSM_WIDTH:])], axis=-1)
    return cat @ lp['w_out']


def hier_moe(h, lp):
    Bn, T, D = h.shape
    x = h.reshape(Bn * T, D)
    g_logits = (x @ lp['router_group_w'] + lp['router_group_b']).astype(jnp.float32)
    g_star = jnp.argmax(g_logits, axis=-1)
    g_prob = jnp.max(jax.nn.softmax(g_logits, axis=-1), axis=-1)
    g_onehot = jax.nn.one_hot(g_star, MOE_GROUPS, dtype=jnp.float32)
    e_logits = (jnp.einsum('nd,gde->nge', x, lp['router_expert_w']) + lp['router_expert_b']).astype(jnp.float32)
    e_logits = jnp.einsum('nge,ng->ne', e_logits, g_onehot)
    top_v, top_i = lax.top_k(e_logits, INNER_TOP_K)
    top_w = jax.nn.softmax(top_v, axis=-1) * g_prob[:, None]
    inner = jnp.sum(jax.nn.one_hot(top_i, EXPERTS_PER_GROUP, dtype=jnp.float32) * top_w[..., None], axis=1)
    gate = (g_onehot[:, :, None] * inner[:, None, :]).reshape(-1, N_EXPERTS).astype(x.dtype)
    hid = jax.nn.silu(jnp.einsum('nd,edf->nef', x, lp['expert_w1'])) * jnp.einsum('nd,edf->nef', x, lp['expert_w3'])
    y = jnp.einsum('nef,efd->nd', hid * gate[:, :, None], lp['expert_w2'])
    return y.reshape(Bn, T, D)


def decoder_layer(x, lp, table, h0_re, h0_im, cache=None):
    h = rms_norm(x, lp['norm_mix'])
    q, kv, gates, u_ssm, u_g, v_g = split_projection(h, lp['w_in'])
    if cache is None:
        o_a, (cmp_rows, sel_rows, win_buf) = nsa_prompt(q, kv, gates, lp, table)
    else:
        o_a, (cmp_rows, sel_rows, win_buf) = nsa_sample(q, kv, gates, lp, table, *cache)
    o_b, s_re, s_im = s5_mixer(u_ssm, h0_re, h0_im, lp)
    o_c, v_rows = gmlp_mixer(u_g, v_g, lp)
    x = x + merge_mixers(o_a, o_b, o_c, lp)
    x = x + hier_moe(rms_norm(x, lp['norm_ffn']), lp)
    return x, (cmp_rows, sel_rows, win_buf, s_re, s_im, v_rows)


def setup_inputs(seed: int = 0) -> dict:
    key = jax.random.key(seed)
    keys = iter(jax.random.split(key, 48))
    f32 = jnp.float32

    def normal(shape, scale):
        return scale * jax.random.normal(next(keys), shape, f32)

    def gain(shape):
        return 1.0 + 0.05 * jax.random.normal(next(keys), shape, f32)

    n_pages = PAST_LEN // PAGE_SIZE
    n_phys = (5 * DEC_BATCH * n_pages + 3) // 4
    n_buf = min(WINDOW, PAST_LEN)
    page_table = jax.random.permutation(next(keys), n_phys)[:DEC_BATCH * n_pages]
    page_table = page_table.reshape(DEC_BATCH, n_pages).astype(jnp.int32)
    kv_pages = (n_phys, DEPTH, PAGE_SIZE, N_KV_HEADS, 2, HEAD_DIM)
    n_idx = jnp.arange(SSM_STATE, dtype=f32)
    return {
        'x_prompt': normal((BATCH, SEQ, D_MODEL), 1.0),
        'x_sample': normal((DEC_BATCH, DEC_SEQ, D_MODEL), 1.0),
        'cache_kv_cmp': normal(kv_pages, 1.0),
        'cache_kv_sel': normal(kv_pages, 1.0),
        'cache_kv_win': normal((DEC_BATCH, DEPTH, n_buf, N_KV_HEADS, 2, HEAD_DIM), 1.0),
        'state_ssm_re': normal((DEC_BATCH, DEPTH, SSM_GROUPS, SSM_STATE), 0.5),
        'state_ssm_im': normal((DEC_BATCH, DEPTH, SSM_GROUPS, SSM_STATE), 0.5),
        'page_table': page_table,
        'rel_bias': normal((NUM_BUCKETS, N_HEADS_A), 0.3),
        'norm_mix': gain((DEPTH, D_MODEL)),
        'w_in': normal((DEPTH, D_MODEL, IN_WIDTH), D_MODEL ** -0.5),
        'qk_norm': gain((DEPTH, 4, HEAD_DIM)),
        'cmp_pos': normal((DEPTH, 2, CMP_BLOCK, HEAD_DIM), 0.3),
        'cmp_w1': normal((DEPTH, 2, CMP_BLOCK * HEAD_DIM, HEAD_DIM), (CMP_BLOCK * HEAD_DIM) ** -0.5),
        'cmp_w2': normal((DEPTH, 2, HEAD_DIM, HEAD_DIM), HEAD_DIM ** -0.5),
        'ssm_a_re': -0.5 + normal((DEPTH, SSM_GROUPS, SSM_STATE), 0.01),
        'ssm_a_im': math.pi * n_idx + normal((DEPTH, SSM_GROUPS, SSM_STATE), 0.01),
        'ssm_log_dt': jax.random.uniform(next(keys), (DEPTH, SSM_GROUPS), f32, math.log(1e-3), math.log(1e-1)),
        'ssm_b_re': normal((DEPTH, SSM_GROUPS, SSM_STATE, SSM_GROUP), (2 * SSM_GROUP) ** -0.5),
        'ssm_b_im': normal((DEPTH, SSM_GROUPS, SSM_STATE, SSM_GROUP), (2 * SSM_GROUP) ** -0.5),
        'ssm_c_re': normal((DEPTH, SSM_GROUPS, SSM_GROUP, SSM_STATE), (2 * SSM_STATE) ** -0.5),
        'ssm_c_im': normal((DEPTH, SSM_GROUPS, SSM_GROUP, SSM_STATE), (2 * SSM_STATE) ** -0.5),
        'ssm_d': normal((DEPTH, SSM_WIDTH), 1.0),
        'ssm_glu_w': normal((DEPTH, SSM_WIDTH, SSM_WIDTH), SSM_WIDTH ** -0.5),
        'ssm_glu_b': normal((DEPTH, SSM_WIDTH), 0.02),
        'gmlp_norm': gain((DEPTH, GMLP_WIDTH)),
        'gmlp_ws': normal((DEPTH, GMLP_GROUPS, CHUNK, CHUNK), CHUNK ** -0.5),
        'gmlp_bs': gain((DEPTH, GMLP_GROUPS, CHUNK)),
        'out_norm': gain((DEPTH, MIX_WIDTH)),
        'w_out': normal((DEPTH, MIX_WIDTH, D_MODEL), MIX_WIDTH ** -0.5),
        'norm_ffn': gain((DEPTH, D_MODEL)),
        'router_group_w': normal((DEPTH, D_MODEL, MOE_GROUPS), D_MODEL ** -0.5),
        'router_group_b': normal((DEPTH, MOE_GROUPS), 0.01),
        'router_expert_w': normal((DEPTH, MOE_GROUPS, D_MODEL, EXPERTS_PER_GROUP), D_MODEL ** -0.5),
        'router_expert_b': normal((DEPTH, MOE_GROUPS, EXPERTS_PER_GROUP), 0.01),
        'expert_w1': normal((DEPTH, N_EXPERTS, D_MODEL, D_EXPERT), D_MODEL ** -0.5),
        'expert_w3': normal((DEPTH, N_EXPERTS, D_MODEL, D_EXPERT), D_MODEL ** -0.5),
        'expert_w2': normal((DEPTH, N_EXPERTS, D_EXPERT, D_MODEL), D_EXPERT ** -0.5),
    }


def reference(x_prompt, x_sample, cache_kv_cmp, cache_kv_sel, cache_kv_win, state_ssm_re, state_ssm_im,
              page_table, rel_bias, norm_mix, w_in, qk_norm, cmp_pos, cmp_w1, cmp_w2,
              ssm_a_re, ssm_a_im, ssm_log_dt, ssm_b_re, ssm_b_im, ssm_c_re, ssm_c_im, ssm_d,
              ssm_glu_w, ssm_glu_b, gmlp_norm, gmlp_ws, gmlp_bs, out_norm, w_out, norm_ffn,
              router_group_w, router_group_b, router_expert_w, router_expert_b,
              expert_w1, expert_w3, expert_w2):
    xp, xs = x_prompt, x_sample
    states_p, states_s = [], []
    zeros_state = jnp.zeros((x_prompt.shape[0], SSM_GROUPS, SSM_STATE), jnp.float32)
    for l in range(DEPTH):
        lp = {
            'norm_mix': norm_mix[l], 'w_in': w_in[l], 'qk_norm': qk_norm[l],
            'cmp_pos': cmp_pos[l], 'cmp_w1': cmp_w1[l], 'cmp_w2': cmp_w2[l],
            'ssm_a_re': ssm_a_re[l], 'ssm_a_im': ssm_a_im[l], 'ssm_log_dt': ssm_log_dt[l],
            'ssm_b_re': ssm_b_re[l], 'ssm_b_im': ssm_b_im[l], 'ssm_c_re': ssm_c_re[l], 'ssm_c_im': ssm_c_im[l],
            'ssm_d': ssm_d[l], 'ssm_glu_w': ssm_glu_w[l], 'ssm_glu_b': ssm_glu_b[l],
            'gmlp_norm': gmlp_norm[l], 'gmlp_ws': gmlp_ws[l], 'gmlp_bs': gmlp_bs[l],
            'out_norm': out_norm[l], 'w_out': w_out[l], 'norm_ffn': norm_ffn[l],
            'router_group_w': router_group_w[l], 'router_group_b': router_group_b[l],
            'router_expert_w': router_expert_w[l], 'router_expert_b': router_expert_b[l],
            'expert_w1': expert_w1[l], 'expert_w3': expert_w3[l], 'expert_w2': expert_w2[l],
        }
        xp, st_p = decoder_layer(xp, lp, rel_bias, zeros_state, zeros_state)
        xs, st_s = decoder_layer(xs, lp, rel_bias, state_ssm_re[:, l], state_ssm_im[:, l],
                                 cache=(cache_kv_cmp, cache_kv_sel, cache_kv_win[:, l], page_table, l))
        states_p.append(st_p)
        states_s.append(st_s)
    p_kv_cmp, p_kv_sel, p_kv_win, p_ssm_re, p_ssm_im, p_gmlp_v = (jnp.stack(z, axis=1) for z in zip(*states_p))
    s_kv_cmp, s_kv_sel, s_kv_win, s_ssm_re, s_ssm_im, s_gmlp_v = (jnp.stack(z, axis=1) for z in zip(*states_s))
    return (xp, xs, p_kv_cmp, p_kv_sel, p_kv_win, p_ssm_re, p_ssm_im, p_gmlp_v,
            s_kv_cmp, s_kv_sel, s_kv_win, s_ssm_re, s_ssm_im, s_gmlp_v)
```

```python
import functools
import math

import numpy as np
import jax
import jax.numpy as jnp
from jax import lax
from jax.experimental import pallas as pl
from jax.experimental.pallas import tpu as pltpu

f32 = jnp.float32
bf16 = jnp.bfloat16

D_MODEL = 2048
DEPTH = 4
PAGE_SIZE = 128
HEAD_DIM = 64
ATTN_WIDTH = 1024
N_HEADS_A = 16
N_KV_HEADS = 4
GQA = 4
KV_WIDTH = 256
CMP_BLOCK = 32
CMP_STRIDE = 16
SEL_BLOCK = 64
SEL_TOPK = 16
WINDOW = 512
NUM_BUCKETS = 32
MAX_DISTANCE = 128
SSM_WIDTH = 512
SSM_GROUP = 16
SSM_GROUPS = 32
SSM_STATE = 64
GMLP_WIDTH = 512
GMLP_GROUPS = 4
CHUNK = 128
MOE_GROUPS = 4
EXPERTS_PER_GROUP = 4
N_EXPERTS = 16
D_EXPERT = 512
SCALE = HEAD_DIM ** -0.5
EPS = 1e-6
NEG = -1e30

PROJ_W = 4224
LANES = 128
VMEM_LIMIT = 56 * 1024 * 1024


def _cparams(sem):
    return pltpu.CompilerParams(dimension_semantics=sem, vmem_limit_bytes=VMEM_LIMIT)


def _dot(a, b):
    return jnp.dot(a, b, preferred_element_type=f32)


def _dot_nt(a, b):
    return lax.dot_general(a, b, (((1,), (1,)), ((), ())), preferred_element_type=f32)


def _split3(x):
    hi = x.astype(bf16)
    r = x - hi.astype(f32)
    mid = r.astype(bf16)
    lo = (r - mid.astype(f32)).astype(bf16)
    return hi, mid, lo


def _gelu(x):
    return 0.5 * x * (1.0 + jnp.tanh(math.sqrt(2.0 / math.pi) * (x + 0.044715 * (x * x * x))))


def _sigmoid(x):
    return 1.0 / (1.0 + jnp.exp(-x))


def _group_mean_sq(x, gmat, group):
    x2 = x * x
    hi = x2.astype(bf16)
    lo = (x2 - hi.astype(f32)).astype(bf16)
    return (_dot(hi, gmat) + _dot(lo, gmat)) * (1.0 / group)


def _proj_kernel(*refs, has_y):
    if has_y:
        x_ref, ya_ref, yb_ref, g_ref, w_ref, proj_ref, xs_ref, xn_sc = refs
    else:
        x_ref, g_ref, w_ref, proj_ref, xn_sc = refs

    @pl.when(pl.program_id(1) == 0)
    def _():
        x = x_ref[...]
        if has_y:
            x = x + ya_ref[...] + yb_ref[...]
            xs_ref[...] = x
        ms = jnp.mean(x * x, axis=-1, keepdims=True)
        xn_sc[...] = (x * lax.rsqrt(ms + EPS) * g_ref[...]).astype(bf16)

    proj_ref[...] = _dot(xn_sc[...], w_ref[...])


def _proj_call(x, y2, g, w, tm=640, tn=384):
    nt = x.shape[0]
    has_y = y2 is not None
    nrow = nt // tm
    in_specs = [pl.BlockSpec((tm, D_MODEL), lambda i, j: (i, 0))]
    args = [x]
    if has_y:
        in_specs += [pl.BlockSpec((tm, D_MODEL), lambda i, j: (i, 0)),
                     pl.BlockSpec((tm, D_MODEL), lambda i, j: (i + nrow, 0))]
        args += [y2, y2]
    in_specs += [pl.BlockSpec((1, D_MODEL), lambda i, j: (0, 0)),
                 pl.BlockSpec((D_MODEL, tn), lambda i, j: (0, j))]
    args += [g, w]
    out_shape = [jax.ShapeDtypeStruct((nt, PROJ_W), f32)]
    out_specs = [pl.BlockSpec((tm, tn), lambda i, j: (i, j))]
    if has_y:
        out_shape.append(jax.ShapeDtypeStruct((nt, D_MODEL), f32))
        out_specs.append(pl.BlockSpec((tm, D_MODEL), lambda i, j: (i, 0)))
    res = pl.pallas_call(
        functools.partial(_proj_kernel, has_y=has_y),
        grid=(nrow, PROJ_W // tn),
        in_specs=in_specs, out_specs=out_specs, out_shape=out_shape,
        scratch_shapes=[pltpu.VMEM((tm, D_MODEL), bf16)],
        compiler_params=_cparams(("parallel", "arbitrary")),
        name="proj",
    )(*args)
    return (res[0], res[1]) if has_y else (res[0], x)


def _post_kernel(q_ref, sel_ref, win_ref, ug_ref, vg_ref, gt_ref, gq_ref, gsel_ref, gwin_ref, gv_ref,
                 g64_ref, g128_ref, qn_ref, ksel_ref, kwin_ref, gate_ref, uga_ref, vn_ref):
    g64 = g64_ref[...]
    lane = lax.broadcasted_iota(jnp.int32, (1, 512), 1)
    is_k = (lane % 128) < 64
    for half in range(2):
        q = q_ref[:, half * 512:(half + 1) * 512]
        ms = _group_mean_sq(q, g64, 64)
        qn = q * lax.rsqrt(ms + EPS) * gq_ref[:, half * 512:(half + 1) * 512]
        qn_ref[:, half * 512:(half + 1) * 512] = (qn * SCALE).astype(bf16)
    for src, gref, dst in ((sel_ref, gsel_ref, ksel_ref), (win_ref, gwin_ref, kwin_ref)):
        x = src[...]
        ms = _group_mean_sq(x, g64, 64)
        dst[...] = jnp.where(is_k, x * lax.rsqrt(ms + EPS) * gref[...], x)
    gate_ref[...] = _sigmoid(gt_ref[...])
    uga_ref[...] = _gelu(ug_ref[...])
    v = _gelu(vg_ref[...])
    ms = _group_mean_sq(v, g128_ref[...], 128)
    vn_ref[...] = v * lax.rsqrt(ms + EPS) * gv_ref[...]


def _post_call(proj, gq, gsel, gwin, gv, g64, g128, tm=320):
    nt = proj.shape[0]

    def col(width, idx):
        return pl.BlockSpec((tm, width), lambda i: (i, idx))

    def full(shape):
        return pl.BlockSpec(shape, lambda i: (0,) * len(shape))

    def out(width):
        return pl.BlockSpec((tm, width), lambda i: (i, 0))

    return pl.pallas_call(
        _post_kernel,
        grid=(nt // tm,),
        in_specs=[col(1024, 0), col(512, 3), col(512, 4), col(512, 6), col(512, 7), col(128, 32),
                  full((1, 1024)), full((1, 512)), full((1, 512)), full((1, 512)),
                  full((512, 512)), full((512, 512))],
        out_specs=[out(1024), out(512), out(512), out(128), out(512), out(512)],
        out_shape=[jax.ShapeDtypeStruct((nt, 1024), bf16), jax.ShapeDtypeStruct((nt, 512), f32),
                   jax.ShapeDtypeStruct((nt, 512), f32), jax.ShapeDtypeStruct((nt, 128), f32),
                   jax.ShapeDtypeStruct((nt, 512), f32), jax.ShapeDtypeStruct((nt, 512), f32)],
        compiler_params=_cparams(("parallel",)),
        name="post",
    )(proj, proj, proj, proj, proj, proj, gq, gsel, gwin, gv, g64, g128)


def _cmp_kernel(pt_ref, src_ref, wab_ref, pos_ref, w2_ref, gk_ref, out_ref, c_sc, *, npg, nchunk):
    p = pl.program_id(2)
    row0 = pl.multiple_of(p * 8, 8)
    for j in range(CMP_STRIDE):
        c_sc[pl.ds(row0, 8), j * 128:(j + 1) * 128] = src_ref[pl.ds(j, 8, stride=CMP_STRIDE), :]

    @pl.when(p == npg - 1)
    def _():
        wab = wab_ref[...]
        pq = _dot(c_sc[...].astype(bf16), wab)
        pos = pos_ref[...].astype(bf16)
        pterm = _dot(pos[:, :2048], wab[:, :128]) + _dot(pos[:, 2048:], wab[:, 128:])
        pre = pq[:, :128] + pltpu.roll(pq[:, 128:], nchunk - 1, 0) + pterm[0:1, :]
        y = _dot(_gelu(pre).astype(bf16), w2_ref[...])
        lane = lax.broadcasted_iota(jnp.int32, (1, 128), 1)
        is_k = lane < 64
        ms = jnp.sum(jnp.where(is_k, y * y, 0.0), axis=-1, keepdims=True) * (1.0 / 64)
        out_ref[...] = jnp.where(is_k, y * lax.rsqrt(ms + EPS) * gk_ref[...], y)


def _cmp_call(pt_flat, src4, layer, col0, nb, npg, wab, pos, w2, gk):
    nchunk = npg * 8

    def src_map(b, h, p, pt):
        return (pt[b * npg + p], layer, 0, col0 + h)

    def full(shape):
        return pl.BlockSpec(shape, lambda b, h, p, pt: (0,) * len(shape))

    return pl.pallas_call(
        functools.partial(_cmp_kernel, npg=npg, nchunk=nchunk),
        grid_spec=pltpu.PrefetchScalarGridSpec(
            num_scalar_prefetch=1,
            grid=(nb, N_KV_HEADS, npg),
            in_specs=[pl.BlockSpec((None, None, PAGE_SIZE, 128), src_map),
                      full((2048, 256)), full((8, 4096)), full((128, 128)), full((1, 128))],
            out_specs=pl.BlockSpec((None, nchunk, 128), lambda b, h, p, pt: (b, 0, h)),
            scratch_shapes=[pltpu.VMEM((nchunk, 2048), f32)],
        ),
        out_shape=jax.ShapeDtypeStruct((nb, nchunk, 512), f32),
        compiler_params=_cparams(("parallel", "parallel", "arbitrary")),
        name="compress",
    )(pt_flat, src4, wab, pos, w2, gk)


def _topk_neg_lanes(score, lanef, k):
    neg = jnp.full(score.shape, NEG, f32)
    work = score
    for _ in range(k):
        m = jnp.max(work, axis=-1, keepdims=True)
        idx = jnp.min(jnp.where(work == m, lanef, 1e9), axis=-1, keepdims=True)
        hit = lanef == idx
        neg = jnp.where(hit, 0.0, neg)
        work = jnp.where(hit, -3e38, work)
    return neg


def _pattn_kernel(q_ref, kvc_ref, ksel_ref, kwin_ref, gate_ref, cb_ref, tz_ref, ov_ref, ex_ref, out_ref,
                  kb_sel, va_sel, kb_win, va_win, vca, sb_sc, m_sc, acc_sel, acc_win, *, tq, t):
    h = pl.program_id(1)
    i = pl.program_id(2)
    r4 = GQA * tq
    nsel = t // SEL_BLOCK
    ntile = t // tq
    nwin = WINDOW // tq

    @pl.when(i == 0)
    def _():
        lane = lax.broadcasted_iota(jnp.int32, (1, 128), 1)
        for src, kb, va in ((ksel_ref, kb_sel, va_sel), (kwin_ref, kb_win, va_win)):
            blk = src[...]
            kb[...] = blk[:, :64].astype(bf16)
            va[...] = jnp.where(lane < 64, pltpu.roll(blk, 64, 1), 1.0).astype(bf16)
        vca[...] = jnp.where(lane < 64, pltpu.roll(kvc_ref[...], 64, 1), 0.0).astype(bf16)

    q = q_ref[...]
    q4 = jnp.concatenate([q[:, g * 64:(g + 1) * 64] for g in range(GQA)], axis=0)

    lane = lax.broadcasted_iota(jnp.int32, (tq, 128), 1)
    row = lax.broadcasted_iota(jnp.int32, (tq, 128), 0)
    qpos = i * tq + row
    kc = kvc_ref[:, :64].astype(bf16)
    cvis = (CMP_STRIDE * lane + CMP_BLOCK - 1 <= qpos) & (lane < (t - CMP_BLOCK) // CMP_STRIDE + 1)
    shift = (CMP_STRIDE * tq // 256) * (ntile - 1 - i)
    cb = jnp.concatenate(
        [jnp.where(cvis, pltpu.roll(cb_ref[g], (128 - shift) % 128, 1), NEG) for g in range(GQA)], axis=0)
    vis4 = jnp.concatenate([cvis] * GQA, axis=0)
    s = _dot_nt(q4, kc) + cb
    m = jnp.max(s, axis=-1, keepdims=True)
    e = jnp.where(vis4, jnp.exp(s - m), 0.0)
    p = e / jnp.maximum(jnp.sum(e, axis=-1, keepdims=True), 1e-30)
    o_cmp = _dot(p.astype(bf16), vca[...])
    pg = p[0:tq] + p[tq:2 * tq] + p[2 * tq:3 * tq] + p[3 * tq:4 * tq]
    ov = ov_ref[...]
    hi, mid, lo = _split3(pg)
    score = _dot(hi, ov) + _dot(mid, ov) + _dot(lo, ov)

    cur = qpos // SEL_BLOCK
    forced = (lane == 0) | (lane == cur) | (lane == cur - 1)
    future = lane * SEL_BLOCK > qpos
    score = jnp.where(forced, 1e6, jnp.where(future, -1.0, score))
    score = jnp.where(lane < nsel, score, -2.0)
    selneg = _topk_neg_lanes(score, lane.astype(f32), min(SEL_TOPK, nsel)).astype(bf16)
    for jj in range(ntile):
        sb_sc[jj] = _dot(selneg, ex_ref[:, jj * tq:(jj + 1) * tq])

    def flash(kb, va, acc, j, bias):
        k0 = pl.multiple_of(j * tq, tq)
        s = _dot_nt(q4, kb[pl.ds(k0, tq), :])
        if bias is not None:
            s = s + bias
        m_old = m_sc[...]
        m_new = jnp.maximum(m_old, jnp.max(s, axis=-1, keepdims=True))
        alpha = jnp.exp(m_old - m_new)
        pm = jnp.exp(s - m_new)
        acc[...] = acc[...] * alpha + _dot(pm.astype(bf16), va[pl.ds(k0, tq), :])
        m_sc[...] = m_new

    def x4(a):
        return jnp.concatenate([a] * GQA, axis=0)

    tz0 = tz_ref[:, 0].reshape(r4, tq)
    tz1 = tz_ref[:, 1].reshape(r4, tq)

    m_sc[...] = jnp.full((r4, 1), NEG, f32)
    acc_sel[...] = jnp.zeros((r4, 128), f32)

    def sel_far(j, c):
        flash(kb_sel, va_sel, acc_sel, j, x4(sb_sc[j]))
        return c

    lax.fori_loop(0, jnp.maximum(i - 1, 0), sel_far, 0)

    @pl.when(i >= 1)
    def _():
        flash(kb_sel, va_sel, acc_sel, i - 1, tz1 + x4(sb_sc[i - 1]))

    flash(kb_sel, va_sel, acc_sel, i, tz0 + x4(sb_sc[i]))

    m_sc[...] = jnp.full((r4, 1), NEG, f32)
    acc_win[...] = jnp.zeros((r4, 128), f32)
    colq = lax.broadcasted_iota(jnp.int32, (r4, tq), 1)
    rowq = lax.broadcasted_iota(jnp.int32, (r4, tq), 0) % tq

    @pl.when(i >= nwin)
    def _():
        flash(kb_win, va_win, acc_win, i - nwin, jnp.where(rowq <= colq, 0.0, NEG))

    def win_far(j, c):
        flash(kb_win, va_win, acc_win, j, None)
        return c

    lax.fori_loop(jnp.maximum(i - nwin + 1, 0), jnp.maximum(i - 1, 0), win_far, 0)

    @pl.when(i >= 1)
    def _():
        flash(kb_win, va_win, acc_win, i - 1, tz1)

    flash(kb_win, va_win, acc_win, i, tz0)

    a_sel = acc_sel[...]
    a_win = acc_win[...]
    o_sel = a_sel[:, :64] / a_sel[:, 64:65]
    o_win = a_win[:, :64] / a_win[:, 64:65]
    gts = gate_ref[...]

    def gcol(idx):
        return jnp.sum(jnp.where(lane == idx, gts, 0.0), axis=-1, keepdims=True)

    pieces = []
    for g in range(GQA):
        sl = slice(g * tq, (g + 1) * tq)
        pieces.append(gcol(h * GQA + g) * o_cmp[sl, :64] + gcol(16 + h * GQA + g) * o_sel[sl]
                      + gcol(32 + h * GQA + g) * o_win[sl])
    out_ref[...] = jnp.concatenate(pieces, axis=1)


def _pattn_call(qn, kvc, ksel, kwin, gates, cb, tz, ov, ex, nb, t, tq=256):
    ntile = t // tq
    r4 = GQA * tq
    grid = (nb, N_KV_HEADS, ntile)
    return pl.pallas_call(
        functools.partial(_pattn_kernel, tq=tq, t=t),
        grid=grid,
        in_specs=[
            pl.BlockSpec((tq, 256), lambda b, h, i: (b * ntile + i, h)),
            pl.BlockSpec((None, t // CMP_STRIDE, 128), lambda b, h, i: (b, 0, h)),
            pl.BlockSpec((t, 128), lambda b, h, i: (b, h)),
            pl.BlockSpec((t, 128), lambda b, h, i: (b, h)),
            pl.BlockSpec((tq, 128), lambda b, h, i: (b * ntile + i, 0)),
            pl.BlockSpec((GQA, tq, 128), lambda b, h, i: (h, 0, 0)),
            pl.BlockSpec((GQA, 2, tq, tq), lambda b, h, i: (h, 0, 0, 0)),
            pl.BlockSpec((128, 128), lambda b, h, i: (0, 0)),
            pl.BlockSpec((128, t), lambda b, h, i: (0, 0)),
        ],
        out_specs=pl.BlockSpec((tq, 256), lambda b, h, i: (b * ntile + i, h)),
        out_shape=jax.ShapeDtypeStruct((nb * t, ATTN_WIDTH), f32),
        scratch_shapes=[
            pltpu.VMEM((t, 64), bf16), pltpu.VMEM((t, 128), bf16),
            pltpu.VMEM((t, 64), bf16), pltpu.VMEM((t, 128), bf16),
            pltpu.VMEM((t // CMP_STRIDE, 128), bf16),
            pltpu.VMEM((ntile, tq, tq), f32),
            pltpu.VMEM((r4, 1), f32), pltpu.VMEM((r4, 128), f32), pltpu.VMEM((r4, 128), f32),
        ],
        compiler_params=_cparams(("parallel", "parallel", "arbitrary")),
        name="prompt_attn",
    )(qn, kvc, ksel, kwin, gates, cb, tz, ov, ex)


CH_PAGES = 16


def _topk_neg_rows(score, rowf, k):
    neg = jnp.full(score.shape, NEG, f32)
    work = score
    for _ in range(k):
        m = jnp.max(work, axis=0, keepdims=True)
        idx = jnp.min(jnp.where(work == m, rowf, 1e9), axis=0, keepdims=True)
        hit = rowf == idx
        neg = jnp.where(hit, 0.0, neg)
        work = jnp.where(hit, -3e38, work)
    return neg


def _col_of(vec):
    return jnp.transpose(jnp.broadcast_to(vec, (128, 128)))[:, :1]


def _sattn_kernel(pt_ref, qbig_ref, kvc_ref, cbt_ref, ovt_ref, smat_ref, e2_ref, pool_ref, sblast_ref,
                  nsel_ref, nbsel_ref, wcache_ref, wbt_ref, nwin_ref, nbwin_ref, gt_ref, out_ref,
                  buf, sem, seln_sc, m_sc, l_sc, acc_sc, *, layer, npg, n_new):
    b = pl.program_id(0)
    nch = npg // CH_PAGES
    ck = CH_PAGES * PAGE_SIZE
    past = npg * PAGE_SIZE
    qbig = qbig_ref[...]

    def page_copy(c, k, slot):
        pg = pt_ref[b * npg + c * CH_PAGES + k]
        return pltpu.make_async_copy(pool_ref.at[pg, layer], buf.at[slot, k], sem.at[slot])

    def start_chunk(c, slot):
        for k in range(CH_PAGES):
            page_copy(c, k, slot).start()

    def wait_chunk(c, slot):
        for k in range(CH_PAGES):
            page_copy(c, k, slot).wait()

    start_chunk(0, 0)
    if nch > 1:
        start_chunk(1, 1)

    def reset():
        m_sc[...] = jnp.full((1, 128), NEG, f32)
        l_sc[...] = jnp.zeros((1, 128), f32)
        acc_sc[...] = jnp.zeros((128, 512), f32)

    def attend(rows, bias_t):
        rb = rows.astype(bf16)
        st = _dot(rb, qbig) + bias_t
        m_old = m_sc[...]
        m_new = jnp.maximum(m_old, jnp.max(st, axis=0, keepdims=True))
        alpha = jnp.exp(m_old - m_new)
        pt = jnp.exp(st - m_new)
        l_sc[...] = l_sc[...] * alpha + jnp.sum(pt, axis=0, keepdims=True)
        acc_sc[...] = acc_sc[...] * _col_of(alpha) + _dot(jnp.transpose(pt).astype(bf16), rb)
        m_sc[...] = m_new

    rowhead = lax.broadcasted_iota(jnp.int32, (128, 128), 0) // 32

    def own_head(res):
        out = jnp.zeros((128, 128), f32)
        for hh in range(N_KV_HEADS):
            out = jnp.where(rowhead == hh, res[:, hh * 128:(hh + 1) * 128], out)
        return out

    kvc = kvc_ref[...]
    kb = kvc.astype(bf16)
    st = _dot(kb, qbig) + cbt_ref[...]
    m = jnp.max(st, axis=0, keepdims=True)
    e = jnp.exp(st - m)
    pt = e / jnp.maximum(jnp.sum(e, axis=0, keepdims=True), 1e-30)
    o_cmp = own_head(_dot(jnp.transpose(pt).astype(bf16), kb))
    smat = smat_ref[...]
    hi, mid, lo = _split3(pt)
    pg = _dot(hi, smat) + _dot(mid, smat) + _dot(lo, smat)
    hi, mid, lo = _split3(pg)
    ovt = ovt_ref[...]
    score = _dot(ovt, hi) + _dot(ovt, mid) + _dot(ovt, lo)
    nrow = score.shape[0]
    blk = lax.broadcasted_iota(jnp.int32, (nrow, 128), 0)
    qpos = past + lax.broadcasted_iota(jnp.int32, (nrow, 128), 1) % 8
    cur = qpos // SEL_BLOCK
    forced = (blk == 0) | (blk == cur) | (blk == cur - 1)
    future = blk * SEL_BLOCK > qpos
    score = jnp.where(forced, 1e6, jnp.where(future, -1.0, score))
    nsel = -(-(past + n_new) // SEL_BLOCK)
    score = jnp.where(blk < nsel, score, -2.0)
    seln_sc[0:nrow, :] = _topk_neg_rows(score, blk.astype(f32), min(SEL_TOPK, nsel))
    seln_sc[nrow:, :] = jnp.zeros((seln_sc.shape[0] - nrow, 128), f32)

    reset()
    bpc = ck // SEL_BLOCK

    def sel_bias(c):
        r0 = pl.multiple_of(c * bpc, bpc)
        return _dot(e2_ref[...], seln_sc[pl.ds(r0, 128), :].astype(bf16))

    def chunk_body(c, carry):
        slot = c % 2
        wait_chunk(c, slot)
        attend(buf[slot].reshape(ck, 512), sel_bias(c))

        @pl.when(c + 2 < nch)
        def _():
            start_chunk(c + 2, slot)

        return carry

    lax.fori_loop(0, nch - 1, chunk_body, 0)
    last = nch - 1
    wait_chunk(last, last % 2)
    attend(buf[last % 2].reshape(ck, 512), sel_bias(last) + sblast_ref[...])
    attend(nsel_ref[...], nbsel_ref[...] + seln_sc[past // SEL_BLOCK:past // SEL_BLOCK + 1, :])
    o_sel = own_head(acc_sc[...]) / _col_of(l_sc[...])

    reset()
    attend(wcache_ref[...], wbt_ref[...])
    attend(nwin_ref[...], nbwin_ref[...])
    o_win = own_head(acc_sc[...]) / _col_of(l_sc[...])

    gts = gt_ref[...]
    out_ref[...] = gts[:, 0:1] * o_cmp + gts[:, 1:2] * o_sel + gts[:, 2:3] * o_win


def _sattn_call(pt_flat, qbig, kvc, cbt, ovt, smat, e2, pool, sblast, nsel, nbsel, wcache, wbt, nwin, nbwin,
                gt, layer, nb, npg, n_new):
    nrow = ovt.shape[0]
    ck = CH_PAGES * PAGE_SIZE

    def full(shape):
        return pl.BlockSpec(shape, lambda b, pt: (0,) * len(shape))

    def perb(shape):
        return pl.BlockSpec((None,) + shape, lambda b, pt: (b,) + (0,) * len(shape))

    return pl.pallas_call(
        functools.partial(_sattn_kernel, layer=layer, npg=npg, n_new=n_new),
        grid_spec=pltpu.PrefetchScalarGridSpec(
            num_scalar_prefetch=1,
            grid=(nb,),
            in_specs=[perb((512, 128)), perb((npg * 8, 512)), full(cbt.shape), full(ovt.shape), full((128, 128)),
                      full((ck, 128)), pl.BlockSpec(memory_space=pl.ANY), full((ck, 128)),
                      perb((128, 512)), full((128, 128)),
                      pl.BlockSpec((None, None, WINDOW, 512), lambda b, pt: (b, layer, 0, 0)), full((WINDOW, 128)),
                      perb((128, 512)), full((128, 128)), perb((128, 128))],
            out_specs=perb((128, 128)),
            scratch_shapes=[pltpu.VMEM((2, CH_PAGES, PAGE_SIZE, 512), f32), pltpu.SemaphoreType.DMA((2,)),
                            pltpu.VMEM((nrow + 128, 128), f32), pltpu.VMEM((1, 128), f32), pltpu.VMEM((1, 128), f32),
                            pltpu.VMEM((128, 512), f32)],
        ),
        out_shape=jax.ShapeDtypeStruct((nb, 128, 128), f32),
        compiler_params=_cparams(("arbitrary",)),
        name="sample_attn",
    )(pt_flat, qbig, kvc, cbt, ovt, smat, e2, pool, sblast, nsel, nbsel, wcache, wbt, nwin, nbwin, gt)


def _s5_kernel(*refs, nb, tc):
    u_refs = refs[:nb]
    (h0r_ref, h0i_ref, abr_ref, abi_ref, bre_ref, bim_ref, cre_ref, cim_ref, d_ref, gw_ref, gb_ref,
     o_ref, hr_out, hi_out, xr, xi, hr, hi) = refs[nb:]
    c = pl.program_id(0)

    @pl.when(c == 0)
    def _():
        hr[...] = h0r_ref[...]
        hi[...] = h0i_ref[...]

    for b in range(nb):
        u = u_refs[b][...].astype(bf16)
        for cb in range(4):
            ub = u[:, cb * 128:(cb + 1) * 128]
            pr = _dot(ub, bre_ref[cb])
            pi = _dot(ub, bim_ref[cb])
            for k in range(4):
                xr[cb * 4 + k, b * tc:(b + 1) * tc, :] = pr[:, k * 128:(k + 1) * 128]
                xi[cb * 4 + k, b * tc:(b + 1) * tc, :] = pi[:, k * 128:(k + 1) * 128]

    for lc in range(4):
        slabs = [lc * 4 + k for k in range(4)]
        ar = [abr_ref[0:nb, j * 128:(j + 1) * 128] for j in slabs]
        ai = [abi_ref[0:nb, j * 128:(j + 1) * 128] for j in slabs]

        def body(t, carry, slabs=slabs, ar=ar, ai=ai):
            rows = pl.ds(t, nb, stride=tc)
            new = []
            for k, j in enumerate(slabs):
                cr, ci = carry[2 * k], carry[2 * k + 1]
                nr = ar[k] * cr - ai[k] * ci + xr[j, rows, :]
                ni = ar[k] * ci + ai[k] * cr + xi[j, rows, :]
                xr[j, rows, :] = nr
                xi[j, rows, :] = ni
                new += [nr, ni]
            return tuple(new)

        init = []
        for j in slabs:
            init += [hr[0:nb, j * 128:(j + 1) * 128], hi[0:nb, j * 128:(j + 1) * 128]]
        fin = lax.fori_loop(0, tc, body, tuple(init))
        for k, j in enumerate(slabs):
            hr[0:nb, j * 128:(j + 1) * 128] = fin[2 * k]
            hi[0:nb, j * 128:(j + 1) * 128] = fin[2 * k + 1]

    ys = []
    for cb in range(4):
        hre = jnp.concatenate([xr[cb * 4 + k] for k in range(4)], axis=1).astype(bf16)
        him = jnp.concatenate([xi[cb * 4 + k] for k in range(4)], axis=1).astype(bf16)
        ys.append(_dot(hre, cre_ref[cb]) - _dot(him, cim_ref[cb]))
    u_all = jnp.concatenate([u_refs[b][...] for b in range(nb)], axis=0)
    y = _gelu(jnp.concatenate(ys, axis=1) + d_ref[...] * u_all)
    o = y * _sigmoid(_dot(y.astype(bf16), gw_ref[...]) + gb_ref[...])
    for b in range(nb):
        o_ref[b] = o[b * tc:(b + 1) * tc]

    @pl.when(c == pl.num_programs(0) - 1)
    def _():
        hr_out[...] = hr[...]
        hi_out[...] = hi[...]


def _s5_call(proj, row0, nb, t, tc, h0r, h0i, abr, abi, bre, bim, cre, cim, d, gw, gb):
    def full(shape):
        return pl.BlockSpec(shape, lambda c: (0,) * len(shape))

    u_specs = [pl.BlockSpec((tc, 512), lambda c, b=b: ((row0 + b * t) // tc + c, 5)) for b in range(nb)]
    return pl.pallas_call(
        functools.partial(_s5_kernel, nb=nb, tc=tc),
        grid=(t // tc,),
        in_specs=u_specs + [full((8, 2048)), full((8, 2048)), full((8, 2048)), full((8, 2048)),
                            full((4, 128, 512)), full((4, 128, 512)), full((4, 512, 128)), full((4, 512, 128)),
                            full((1, 512)), full((512, 512)), full((1, 512))],
        out_specs=[pl.BlockSpec((nb, tc, 512), lambda c: (0, c, 0)), full((8, 2048)), full((8, 2048))],
        out_shape=[jax.ShapeDtypeStruct((nb, t, 512), f32), jax.ShapeDtypeStruct((8, 2048), f32),
                   jax.ShapeDtypeStruct((8, 2048), f32)],
        scratch_shapes=[pltpu.VMEM((16, nb * tc, 128), f32), pltpu.VMEM((16, nb * tc, 128), f32),
                        pltpu.VMEM((8, 2048), f32), pltpu.VMEM((8, 2048), f32)],
        compiler_params=_cparams(("arbitrary",)),
        name="s5",
    )(*([proj] * nb), h0r, h0i, abr, abi, bre, bim, cre, cim, d, gw, gb)


def _gmlp_kernel(u_ref, v_ref, w_ref, bs_ref, o_ref, *, rows):
    u = u_ref[...]
    v = v_ref[...]
    outs = []
    for g in range(GMLP_GROUPS):
        vg = v[:, g * 128:(g + 1) * 128]
        if rows < CHUNK:
            vg = jnp.concatenate([vg, jnp.zeros((CHUNK - rows, 128), f32)], axis=0)
        mixed = _dot(w_ref[g][0:rows, :], vg.astype(bf16)) + bs_ref[0:rows, g:g + 1]
        outs.append(u[:, g * 128:(g + 1) * 128] * mixed)
    o_ref[...] = jnp.concatenate(outs, axis=1)


def _gmlp_call(uga, vn, w, bs, row0, nrows, rows):
    blk0 = row0 // rows

    def full(shape):
        return pl.BlockSpec(shape, lambda i: (0,) * len(shape))

    return pl.pallas_call(
        functools.partial(_gmlp_kernel, rows=rows),
        grid=(nrows // rows,),
        in_specs=[pl.BlockSpec((rows, 512), lambda i: (blk0 + i, 0)), pl.BlockSpec((rows, 512), lambda i: (blk0 + i, 0)),
                  full((GMLP_GROUPS, CHUNK, CHUNK)), full((CHUNK, 128))],
        out_specs=pl.BlockSpec((rows, 512), lambda i: (i, 0)),
        out_shape=jax.ShapeDtypeStruct((nrows, 512), f32),
        compiler_params=_cparams(("parallel",)),
        name="gmlp",
    )(uga, vn, w, bs)


def _merge_kernel(x_ref, oa_ref, ob_ref, oc_ref, gn_ref, w_ref, gf_ref, rwh_ref, rwl_ref, rb_ref,
                  x1_ref, xn_ref, rt_ref):
    def rms(v, g):
        return (v * lax.rsqrt(jnp.mean(v * v, axis=-1, keepdims=True) + EPS) * g).astype(bf16)

    acc = _dot(rms(oa_ref[...], gn_ref[:, 0:1024]), w_ref[0:1024, :])
    acc += _dot(rms(ob_ref[...], gn_ref[:, 1024:1536]), w_ref[1024:1536, :])
    acc += _dot(rms(oc_ref[...], gn_ref[:, 1536:2048]), w_ref[1536:2048, :])
    x1 = x_ref[...] + acc
    x1_ref[...] = x1
    xn = x1 * lax.rsqrt(jnp.mean(x1 * x1, axis=-1, keepdims=True) + EPS) * gf_ref[...]
    xn_ref[...] = xn
    hi = xn.astype(bf16)
    lo = (xn - hi.astype(f32)).astype(bf16)
    logits = _dot(hi, rwh_ref[...]) + _dot(lo, rwh_ref[...]) + _dot(hi, rwl_ref[...]) + rb_ref[...]
    lane = lax.broadcasted_iota(jnp.int32, logits.shape, 1)
    lanef = lane.astype(f32)
    is_g = lane < MOE_GROUPS
    gl = jnp.where(is_g, logits, NEG)
    gm = jnp.max(gl, axis=-1, keepdims=True)
    gidx = jnp.min(jnp.where(gl == gm, lanef, 1e9), axis=-1, keepdims=True)
    gprob = 1.0 / jnp.sum(jnp.where(is_g, jnp.exp(logits - gm), 0.0), axis=-1, keepdims=True)
    lo_lane = MOE_GROUPS + EXPERTS_PER_GROUP * gidx
    inl = jnp.where((lanef >= lo_lane) & (lanef < lo_lane + EXPERTS_PER_GROUP), logits, NEG)
    v1 = jnp.max(inl, axis=-1, keepdims=True)
    i1 = jnp.min(jnp.where(inl == v1, lanef, 1e9), axis=-1, keepdims=True)
    inl2 = jnp.where(lanef == i1, NEG, inl)
    v2 = jnp.max(inl2, axis=-1, keepdims=True)
    i2 = jnp.min(jnp.where(inl2 == v2, lanef, 1e9), axis=-1, keepdims=True)
    e2 = jnp.exp(v2 - v1)
    w1 = gprob / (1.0 + e2)
    w2 = gprob * e2 / (1.0 + e2)
    rt_ref[...] = jnp.where(lane == 0, i1 - MOE_GROUPS,
                            jnp.where(lane == 1, i2 - MOE_GROUPS,
                                      jnp.where(lane == 2, w1, jnp.where(lane == 3, w2, 0.0))))


def _merge_call(x, oa, ob, oc, gn, w, gf, rwh, rwl, rb, tm=320):
    nt = x.shape[0]

    def row(width):
        return pl.BlockSpec((tm, width), lambda i: (i, 0))

    def full(shape):
        return pl.BlockSpec(shape, lambda i: (0,) * len(shape))

    return pl.pallas_call(
        _merge_kernel,
        grid=(nt // tm,),
        in_specs=[row(2048), row(1024), row(512), row(512), full((1, 2048)), full((2048, 2048)), full((1, 2048)),
                  full((2048, 128)), full((2048, 128)), full((1, 128))],
        out_specs=[row(2048), row(2048), row(128)],
        out_shape=[jax.ShapeDtypeStruct((nt, 2048), f32), jax.ShapeDtypeStruct((nt, 2048), f32),
                   jax.ShapeDtypeStruct((nt, 128), f32)],
        compiler_params=_cparams(("parallel",)),
        name="merge_router",
    )(x, oa, ob, oc, gn, w, gf, rwh, rwl, rb)


def _expert_kernel(te_ref, nu_ref, src_ref, dst_ref, xn_hbm, wg_ref, w1_ref, w3_ref, w2_ref, y_hbm,
                   xbuf, obuf, gsem, ssem, *, tm, ntiles):
    t = pl.program_id(0)
    n_used = nu_ref[0]
    slot = t % 2

    def gather(tt, sl):
        def body(r, c):
            tok = src_ref[tt * tm + r]
            pltpu.make_async_copy(xn_hbm.at[pl.ds(tok, 1)], xbuf.at[sl, pl.ds(r, 1)], gsem.at[sl]).start()
            return c
        lax.fori_loop(0, tm, body, 0)

    def gather_wait(sl):
        pltpu.make_async_copy(xn_hbm.at[pl.ds(0, tm)], xbuf.at[sl], gsem.at[sl]).wait()

    def scatter(tt, sl):
        def body(r, c):
            row = dst_ref[tt * tm + r]
            pltpu.make_async_copy(obuf.at[sl, pl.ds(r, 1)], y_hbm.at[pl.ds(row, 1)], ssem.at[sl]).start()
            return c
        lax.fori_loop(0, tm, body, 0)

    def scatter_wait(sl):
        pltpu.make_async_copy(obuf.at[sl], y_hbm.at[pl.ds(0, tm)], ssem.at[sl]).wait()

    @pl.when(t == 0)
    def _():
        gather(0, 0)
        npair = y_hbm.shape[0] - 2 * tm
        for sl in range(2):
            obuf[sl] = jnp.zeros((tm, D_MODEL), f32)
            cp = pltpu.make_async_copy(obuf.at[sl], y_hbm.at[pl.ds(npair + sl * tm, tm)], ssem.at[sl])
            cp.start()
            cp.wait()

    @pl.when(t + 1 < n_used)
    def _():
        gather(t + 1, 1 - slot)

    @pl.when(t < n_used)
    def _():
        gather_wait(slot)

        @pl.when(t >= 2)
        def _():
            scatter_wait(slot)

        x = xbuf[slot].astype(bf16)
        a = _dot(x, w1_ref[...])
        hmid = a * _sigmoid(a) * _dot(x, w3_ref[...]) * wg_ref[...]
        obuf[slot] = _dot(hmid.astype(bf16), w2_ref[...])
        scatter(t, slot)

    @pl.when(t == ntiles - 1)
    def _():
        scatter_wait((n_used - 1) % 2)

        @pl.when(n_used >= 2)
        def _():
            scatter_wait(n_used % 2)


def _expert_call(tile_e, n_used, src_tok, dst_row, xn, wgt, w1, w3, w2, tm, ntiles, nrows_out):
    def wmap(t, te, nu, src, dst):
        return (te[t], 0, 0)

    return pl.pallas_call(
        functools.partial(_expert_kernel, tm=tm, ntiles=ntiles),
        grid_spec=pltpu.PrefetchScalarGridSpec(
            num_scalar_prefetch=4,
            grid=(ntiles,),
            in_specs=[pl.BlockSpec(memory_space=pl.ANY),
                      pl.BlockSpec((tm, 1), lambda t, te, nu, src, dst: (t, 0)),
                      pl.BlockSpec((None, D_MODEL, D_EXPERT), wmap),
                      pl.BlockSpec((None, D_MODEL, D_EXPERT), wmap),
                      pl.BlockSpec((None, D_EXPERT, D_MODEL), wmap)],
            out_specs=pl.BlockSpec(memory_space=pl.ANY),
            scratch_shapes=[pltpu.VMEM((2, tm, D_MODEL), f32), pltpu.VMEM((2, tm, D_MODEL), f32),
                            pltpu.SemaphoreType.DMA((2,)), pltpu.SemaphoreType.DMA((2,))],
        ),
        out_shape=jax.ShapeDtypeStruct((nrows_out, D_MODEL), f32),
        compiler_params=_cparams(("arbitrary",)),
        name="experts",
    )(tile_e, n_used, src_tok, dst_row, xn, wgt, w1, w3, w2)


def _add3_kernel(x_ref, ya_ref, yb_ref, o_ref):
    o_ref[...] = x_ref[...] + ya_ref[...] + yb_ref[...]


def _add3_call(x, y2, tm=640):
    nt = x.shape[0]
    nrow = nt // tm
    return pl.pallas_call(
        _add3_kernel,
        grid=(nrow,),
        in_specs=[pl.BlockSpec((tm, D_MODEL), lambda i: (i, 0)), pl.BlockSpec((tm, D_MODEL), lambda i: (i, 0)),
                  pl.BlockSpec((tm, D_MODEL), lambda i: (i + nrow, 0))],
        out_specs=pl.BlockSpec((tm, D_MODEL), lambda i: (i, 0)),
        out_shape=jax.ShapeDtypeStruct((nt, D_MODEL), f32),
        compiler_params=_cparams(("parallel",)),
        name="residual_add",
    )(x, y2, y2)


EXPERT_TM = 256
PROMPT_TQ = 256


def _bucket(n):
    max_exact = NUM_BUCKETS // 2
    nf = jnp.maximum(n, max_exact).astype(f32)
    large = max_exact + (jnp.log(nf / max_exact) / math.log(MAX_DISTANCE / max_exact)
                         * (NUM_BUCKETS - max_exact)).astype(jnp.int32)
    return jnp.where(n < max_exact, n, jnp.minimum(large, NUM_BUCKETS - 1))


def _bias_tables(rel_bias, tq, t, past, n_new):
    bd = rel_bias[_bucket(jnp.arange(128))] - rel_bias[NUM_BUCKETS - 1][None, :]

    def look(dist):
        oh = jax.nn.one_hot(jnp.clip(dist, 0, 127), 128, dtype=f32)
        return jnp.einsum('...d,dh->...h', oh, bd, precision=lax.Precision.HIGHEST)

    ntile = t // tq
    r = jnp.arange(tq)
    d0 = r[:, None] - r[None, :]
    t0 = jnp.where((d0 >= 0)[..., None], look(d0), NEG)
    t1 = look(tq + d0)
    tz = jnp.transpose(jnp.stack([t0, t1], axis=0), (3, 0, 1, 2))
    n = jnp.arange(128)
    cbl = jnp.transpose(look(tq * (ntile - 1) + r[:, None] - CMP_STRIDE * n[None, :] - (CMP_BLOCK - 1)), (2, 0, 1))

    c = jnp.arange(128)
    head_oh = jax.nn.one_hot(4 * (c // 32) + (c % 32) // 8, N_HEADS_A, dtype=f32)
    qq = c % 8

    def look_t(dist, vis):
        tab = jnp.einsum('mch,ch->mc', look(dist), head_oh, precision=lax.Precision.HIGHEST)
        return jnp.where(vis, tab, NEG)

    n_cmp_s = (past + n_new - CMP_BLOCK) // CMP_STRIDE + 1
    nchunk = past // CMP_STRIDE
    nn = jnp.arange(nchunk)[:, None]
    dist = past + qq[None, :] - (CMP_STRIDE * nn + CMP_BLOCK - 1)
    cbt = look_t(dist, (dist >= 0) & (nn < n_cmp_s))
    ck = CH_PAGES * PAGE_SIZE
    rr = jnp.arange(ck)[:, None]
    dist = qq[None, :] + ck - rr
    sblast = look_t(dist, dist >= 0)
    r128 = jnp.arange(128)[:, None]
    dist = qq[None, :] - r128
    nbnew = look_t(dist, (dist >= 0) & (r128 < n_new))
    rw = jnp.arange(WINDOW)[:, None]
    dist = WINDOW + qq[None, :] - rw
    wbt = look_t(dist, (dist >= 0) & (dist <= WINDOW))
    return tz, cbl, cbt, sblast, nbnew, wbt


def _static_mats(t, past, n_new):
    n = np.arange(128)
    n_cmp = (t - CMP_BLOCK) // CMP_STRIDE + 1
    s = np.arange(128)
    ov = ((CMP_STRIDE * n[:, None] < SEL_BLOCK * s[None, :] + SEL_BLOCK)
          & (CMP_STRIDE * n[:, None] + CMP_BLOCK > SEL_BLOCK * s[None, :])
          & (n[:, None] < n_cmp) & (s[None, :] < t // SEL_BLOCK))
    ex = (np.arange(t)[None, :] // SEL_BLOCK == s[:, None])
    nsel_s = -(-(past + n_new) // SEL_BLOCK)
    nrow = -(-nsel_s // 8) * 8
    n_cmp_s = (past + n_new - CMP_BLOCK) // CMP_STRIDE + 1
    ss = np.arange(nrow)[:, None]
    ns = np.arange(past // CMP_STRIDE)[None, :]
    ovt = ((CMP_STRIDE * ns < SEL_BLOCK * ss + SEL_BLOCK) & (CMP_STRIDE * ns + CMP_BLOCK > SEL_BLOCK * ss)
           & (ns < n_cmp_s) & (ss < nsel_s))
    c = np.arange(128)
    smat = (c[:, None] // 32 == c[None, :] // 32) & (c[:, None] % 8 == c[None, :] % 8)
    ck = CH_PAGES * PAGE_SIZE
    e2 = (np.arange(ck)[:, None] // SEL_BLOCK == np.arange(128)[None, :])
    cvt = lambda a: jnp.asarray(a.astype(np.float32), dtype=bf16)
    return cvt(ov), cvt(ex), cvt(ovt), cvt(smat), cvt(e2)


def _route_metadata(route, nt, tm, ntiles):
    e_flat = jnp.concatenate([route[:, 0], route[:, 1]]).astype(jnp.int32)
    w_flat = jnp.concatenate([route[:, 2], route[:, 3]])
    npair = 2 * nt
    order = jnp.argsort(e_flat, stable=True).astype(jnp.int32)
    e_sorted = e_flat[order]
    counts = jnp.sum(jax.nn.one_hot(e_flat, N_EXPERTS, dtype=jnp.int32), axis=0)
    tiles_e = (counts + tm - 1) // tm
    tend = jnp.cumsum(tiles_e)
    tstart = tend - tiles_e
    cstart = jnp.cumsum(counts) - counts
    slot = tstart[e_sorted] * tm + jnp.arange(npair, dtype=jnp.int32) - cstart[e_sorted]
    nslots = ntiles * tm
    sidx = jnp.arange(nslots, dtype=jnp.int32)
    dump = npair + ((sidx // tm) % 2) * tm + sidx % tm
    src_tok = jnp.zeros((nslots,), jnp.int32).at[slot].set(order % nt)
    dst_row = dump.at[slot].set(order)
    wgt = jnp.zeros((nslots,), f32).at[slot].set(w_flat[order])
    n_used = tend[-1]
    tt = jnp.minimum(jnp.arange(ntiles, dtype=jnp.int32), n_used - 1)
    tile_e = jnp.sum((tt[:, None] >= tend[None, :]).astype(jnp.int32), axis=1)
    return tile_e.astype(jnp.int32), n_used.reshape(1).astype(jnp.int32), src_tok, dst_row, wgt.reshape(nslots, 1)


def _block_diag(blocks):
    n, r, c = blocks.shape
    return jnp.einsum('grc,gk->grkc', blocks, jnp.eye(n, dtype=blocks.dtype)).reshape(n * r, n * c)


def kernel(x_prompt, x_sample, cache_kv_cmp, cache_kv_sel, cache_kv_win, state_ssm_re, state_ssm_im, page_table, rel_bias, norm_mix, w_in, qk_norm, cmp_pos, cmp_w1, cmp_w2, ssm_a_re, ssm_a_im, ssm_log_dt, ssm_b_re, ssm_b_im, ssm_c_re, ssm_c_im, ssm_d, ssm_glu_w, ssm_glu_b, gmlp_norm, gmlp_ws, gmlp_bs, out_norm, w_out, norm_ffn, router_group_w, router_group_b, router_expert_w, router_expert_b, expert_w1, expert_w3, expert_w2):
    bp, t, _ = x_prompt.shape
    bs, s_new, _ = x_sample.shape
    npg = page_table.shape[1]
    past = npg * PAGE_SIZE
    nphys = cache_kv_cmp.shape[0]
    n_p, n_s = bp * t, bs * s_new
    nt = -(-(n_p + n_s) // 640) * 640
    tq = PROMPT_TQ
    assert t % tq == 0 and WINDOW % tq == 0 and s_new == 8 and bs == 8 and npg % CH_PAGES == 0

    x = jnp.concatenate([x_prompt.reshape(n_p, D_MODEL), x_sample.reshape(n_s, D_MODEL),
                         jnp.zeros((nt - n_p - n_s, D_MODEL), f32)], axis=0)
    y2 = None
    pool_cmp = cache_kv_cmp.reshape(nphys, DEPTH, PAGE_SIZE, 512)
    pool_sel = cache_kv_sel.reshape(nphys, DEPTH, PAGE_SIZE, 512)
    wcache = cache_kv_win.reshape(bs, DEPTH, WINDOW, 512)
    pt_flat = page_table.reshape(-1).astype(jnp.int32)
    pt_ident = jnp.arange(bp * (t // PAGE_SIZE), dtype=jnp.int32)

    tz, cbl, cbt, sblast, nbnew, wbt = _bias_tables(rel_bias, tq, t, past, s_new)
    ov, ex, ovt, smat, e2 = _static_mats(t, past, s_new)
    g64 = _block_diag(jnp.ones((8, 64, 64), bf16))
    g128 = _block_diag(jnp.ones((4, 128, 128), bf16))
    ones64 = jnp.ones((HEAD_DIM,), f32)
    tril = jnp.tril(jnp.ones((CHUNK, CHUNK), f32))
    zeros_state = jnp.zeros((8, SSM_GROUPS * SSM_STATE), f32)
    ntiles = 2 * nt // EXPERT_TM + N_EXPERTS
    pad_rows = lambda a: jnp.concatenate([a, jnp.zeros((nt - a.shape[0],) + a.shape[1:], a.dtype)], axis=0)

    outs = {k: [] for k in ('pc', 'ps', 'pw', 'pr', 'pi', 'pv', 'sc', 'ss', 'sw', 'sr', 'si', 'sv')}
    for l in range(DEPTH):
        wl = w_in[l]
        o1, o2, o3 = ATTN_WIDTH, ATTN_WIDTH + 6 * KV_WIDTH, ATTN_WIDTH + 6 * KV_WIDTH + 3 * N_HEADS_A
        w_pad = jnp.concatenate([wl[:, :o2], wl[:, o3:], wl[:, o2:o3],
                                 jnp.zeros((D_MODEL, 128 - 3 * N_HEADS_A), f32)], axis=1).astype(bf16)
        gq = jnp.tile(qk_norm[l, 0], N_HEADS_A).reshape(1, 1024)
        gsel = jnp.tile(jnp.concatenate([qk_norm[l, 2], ones64]), N_KV_HEADS).reshape(1, 512)
        gwin = jnp.tile(jnp.concatenate([qk_norm[l, 3], ones64]), N_KV_HEADS).reshape(1, 512)
        gk = jnp.concatenate([qk_norm[l, 1], ones64]).reshape(1, 128)
        w1k = cmp_w1[l, 0].reshape(CMP_BLOCK, HEAD_DIM, HEAD_DIM)
        w1v = cmp_w1[l, 1].reshape(CMP_BLOCK, HEAD_DIM, HEAD_DIM)
        zz = jnp.zeros_like(w1k)
        wfull = jnp.concatenate([jnp.concatenate([w1k, zz], axis=2), jnp.concatenate([zz, w1v], axis=2)], axis=1)
        wab = jnp.concatenate([wfull[:16].reshape(2048, 128), wfull[16:].reshape(2048, 128)], axis=1).astype(bf16)
        posf = jnp.concatenate([cmp_pos[l, 0], cmp_pos[l, 1]], axis=1)
        pos8 = jnp.broadcast_to(jnp.concatenate([posf[:16].reshape(1, 2048), posf[16:].reshape(1, 2048)], axis=1),
                                (8, 4096))
        w2bd = _block_diag(cmp_w2[l]).astype(bf16)

        dt = jnp.exp(ssm_log_dt[l])[:, None]
        a_re, a_im = ssm_a_re[l], ssm_a_im[l]
        mag = jnp.exp(dt * a_re)
        ab_re, ab_im = mag * jnp.cos(dt * a_im), mag * jnp.sin(dt * a_im)
        den = a_re * a_re + a_im * a_im
        f_re = ((ab_re - 1.0) * a_re + ab_im * a_im) / den
        f_im = (ab_im * a_re - (ab_re - 1.0) * a_im) / den
        bb_re = f_re[..., None] * ssm_b_re[l] - f_im[..., None] * ssm_b_im[l]
        bb_im = f_re[..., None] * ssm_b_im[l] + f_im[..., None] * ssm_b_re[l]
        abr = jnp.broadcast_to(ab_re.reshape(1, -1), (8, SSM_GROUPS * SSM_STATE))
        abi = jnp.broadcast_to(ab_im.reshape(1, -1), (8, SSM_GROUPS * SSM_STATE))
        eye8 = jnp.eye(8, dtype=f32)

        def in_blocks(bb):
            xx = jnp.transpose(bb, (0, 2, 1)).reshape(4, 8, SSM_GROUP, SSM_STATE)
            return jnp.einsum('agcn,gk->agckn', xx, eye8).reshape(4, 128, 512).astype(bf16)

        def out_blocks(cc):
            yy = jnp.transpose(cc, (0, 2, 1)).reshape(4, 8, SSM_STATE, SSM_GROUP)
            return jnp.einsum('agnc,gk->agnkc', yy, eye8).reshape(4, 512, 128).astype(bf16)

        s5p = (abr, abi, in_blocks(bb_re), in_blocks(bb_im), out_blocks(ssm_c_re[l]), out_blocks(ssm_c_im[l]),
               ssm_d[l].reshape(1, 512), ssm_glu_w[l].astype(bf16), ssm_glu_b[l].reshape(1, 512))
        gw = (gmlp_ws[l] * tril).astype(bf16)
        gbs = jnp.concatenate([gmlp_bs[l].T, jnp.zeros((CHUNK, 128 - GMLP_GROUPS), f32)], axis=1)
        rw = jnp.concatenate([router_group_w[l], jnp.transpose(router_expert_w[l], (1, 0, 2)).reshape(D_MODEL, N_EXPERTS),
                              jnp.zeros((D_MODEL, 128 - MOE_GROUPS - N_EXPERTS), f32)], axis=1)
        rwh = rw.astype(bf16)
        rwl = (rw - rwh.astype(f32)).astype(bf16)
        rb = jnp.concatenate([router_group_b[l], router_expert_b[l].reshape(-1),
                              jnp.zeros((128 - MOE_GROUPS - N_EXPERTS,), f32)]).reshape(1, 128)

        proj, x = _proj_call(x, y2, norm_mix[l].reshape(1, D_MODEL), w_pad)
        qn, ksel, kwin, gates, uga, vn = _post_call(proj, gq, gsel, gwin, gmlp_norm[l].reshape(1, 512), g64, g128)

        src_p = proj.reshape(nt // PAGE_SIZE, 1, PAGE_SIZE, PROJ_W)
        kvc_p = _cmp_call(pt_ident, src_p, 0, ATTN_WIDTH // 128, bp, t // PAGE_SIZE, wab, pos8, w2bd, gk)
        kvc_s = _cmp_call(pt_flat, pool_cmp, l, 0, bs, npg, wab, pos8, w2bd, gk)
        oa_p = _pattn_call(qn, kvc_p, ksel, kwin, gates, cbl, tz, ov, ex, bp, t, tq)

        qs = qn[n_p:n_p + n_s].reshape(bs, s_new, N_KV_HEADS, GQA, HEAD_DIM)
        qa = jnp.transpose(qs, (0, 2, 4, 3, 1)).reshape(bs, N_KV_HEADS, HEAD_DIM, GQA * s_new)
        qbig = jnp.einsum('bhdc,hk->bhdkc', qa, jnp.eye(N_KV_HEADS, dtype=bf16))
        qbig = jnp.pad(qbig, ((0, 0), (0, 0), (0, 64), (0, 0), (0, 0))).reshape(bs, 512, 128)
        gs = gates[n_p:n_p + n_s, :48].reshape(bs, s_new, 3, N_KV_HEADS, GQA)
        gt = jnp.pad(jnp.transpose(gs, (0, 3, 4, 1, 2)).reshape(bs, 128, 3), ((0, 0), (0, 0), (0, 125)))
        new_sel = jnp.pad(ksel[n_p:n_p + n_s].reshape(bs, s_new, 512), ((0, 0), (0, 128 - s_new), (0, 0)))
        new_win = jnp.pad(kwin[n_p:n_p + n_s].reshape(bs, s_new, 512), ((0, 0), (0, 128 - s_new), (0, 0)))
        osmp = _sattn_call(pt_flat, qbig, kvc_s, cbt, ovt, smat, e2, pool_sel, sblast, new_sel, nbnew,
                           wcache, wbt, new_win, nbnew, gt, l, bs, npg, s_new)
        oa_s = jnp.transpose(osmp[:, :, 64:].reshape(bs, N_KV_HEADS, GQA, s_new, HEAD_DIM),
                             (0, 3, 1, 2, 4)).reshape(n_s, ATTN_WIDTH)
        oa = pad_rows(jnp.concatenate([oa_p, oa_s], axis=0))

        ob_p, hr_p, hi_p = _s5_call(proj, 0, bp, t, 256, zeros_state, zeros_state, *s5p)
        ob_s, hr_s, hi_s = _s5_call(proj, n_p, bs, s_new, s_new, state_ssm_re[:, l].reshape(bs, -1),
                                    state_ssm_im[:, l].reshape(bs, -1), *s5p)
        ob = pad_rows(jnp.concatenate([ob_p.reshape(n_p, 512), ob_s.reshape(n_s, 512)], axis=0))

        oc_p = _gmlp_call(uga, vn, gw, gbs, 0, n_p, CHUNK)
        oc_s = _gmlp_call(uga, vn, gw, gbs, n_p, n_s, s_new)
        oc = pad_rows(jnp.concatenate([oc_p, oc_s], axis=0))

        x1, xn2, route = _merge_call(x, oa, ob, oc, out_norm[l].reshape(1, -1), w_out[l].astype(bf16),
                                     norm_ffn[l].reshape(1, -1), rwh, rwl, rb)
        tile_e, n_used, src_tok, dst_row, wgt = _route_metadata(route, nt, EXPERT_TM, ntiles)
        y2 = _expert_call(tile_e, n_used, src_tok, dst_row, xn2, wgt, expert_w1[l].astype(bf16),
                          expert_w3[l].astype(bf16), expert_w2[l].astype(bf16), EXPERT_TM, ntiles,
                          2 * nt + 2 * EXPERT_TM)
        x = x1

        kvshape = (N_KV_HEADS, 2, HEAD_DIM)
        cmp_rows = proj[:, ATTN_WIDTH:ATTN_WIDTH + 512]
        outs['pc'].append(cmp_rows[:n_p].reshape(bp, t, *kvshape))
        outs['ps'].append(ksel[:n_p].reshape(bp, t, *kvshape))
        outs['pw'].append(kwin[:n_p].reshape(bp, t, *kvshape)[:, t - min(WINDOW, t):])
        outs['pr'].append(hr_p[:bp].reshape(bp, SSM_GROUPS, SSM_STATE))
        outs['pi'].append(hi_p[:bp].reshape(bp, SSM_GROUPS, SSM_STATE))
        outs['pv'].append(vn[:n_p].reshape(bp, t, GMLP_WIDTH)[:, (t - 1) // CHUNK * CHUNK:])
        outs['sc'].append(cmp_rows[n_p:n_p + n_s].reshape(bs, s_new, *kvshape))
        outs['ss'].append(ksel[n_p:n_p + n_s].reshape(bs, s_new, *kvshape))
        win_new = kwin[n_p:n_p + n_s].reshape(bs, s_new, *kvshape)
        outs['sw'].append(jnp.concatenate([cache_kv_win[:, l], win_new], axis=1)[:, s_new:])
        outs['sr'].append(hr_s[:bs].reshape(bs, SSM_GROUPS, SSM_STATE))
        outs['si'].append(hi_s[:bs].reshape(bs, SSM_GROUPS, SSM_STATE))
        outs['sv'].append(vn[n_p:n_p + n_s].reshape(bs, s_new, GMLP_WIDTH))

    xf = _add3_call(x, y2)
    st = {k: jnp.stack(v, axis=1) for k, v in outs.items()}
    return (xf[:n_p].reshape(bp, t, D_MODEL), xf[n_p:n_p + n_s].reshape(bs, s_new, D_MODEL),
            st['pc'], st['ps'], st['pw'], st['pr'], st['pi'], st['pv'],
            st['sc'], st['ss'], st['sw'], st['sr'], st['si'], st['sv'])
```

```python
import functools
import math

import numpy as np
import jax
import jax.numpy as jnp
from jax import lax
from jax.experimental import pallas as pl
from jax.experimental.pallas import tpu as pltpu

f32 = jnp.float32
bf16 = jnp.bfloat16

D_MODEL = 2048
DEPTH = 4
PAGE_SIZE = 128
HEAD_DIM = 64
ATTN_WIDTH = 1024
N_HEADS_A = 16
N_KV_HEADS = 4
GQA = 4
KV_WIDTH = 256
CMP_BLOCK = 32
CMP_STRIDE = 16
SEL_BLOCK = 64
SEL_TOPK = 16
WINDOW = 512
NUM_BUCKETS = 32
MAX_DISTANCE = 128
SSM_WIDTH = 512
SSM_GROUP = 16
SSM_GROUPS = 32
SSM_STATE = 64
GMLP_WIDTH = 512
GMLP_GROUPS = 4
CHUNK = 128
MOE_GROUPS = 4
EXPERTS_PER_GROUP = 4
N_EXPERTS = 16
D_EXPERT = 512
SCALE = HEAD_DIM ** -0.5
EPS = 1e-6
NEG = -1e30

PROJ_W = 4224
LANES = 128
VMEM_LIMIT = 56 * 1024 * 1024


def _cparams(sem, **kw):
    return pltpu.CompilerParams(dimension_semantics=sem, vmem_limit_bytes=VMEM_LIMIT, **kw)


def _dot(a, b):
    return jnp.dot(a, b, preferred_element_type=f32)


def _dot_nt(a, b):
    return lax.dot_general(a, b, (((1,), (1,)), ((), ())), preferred_element_type=f32)


def _split3(x):
    hi = x.astype(bf16)
    r = x - hi.astype(f32)
    mid = r.astype(bf16)
    lo = (r - mid.astype(f32)).astype(bf16)
    return hi, mid, lo


def _gelu(x):
    return 0.5 * x * (1.0 + jnp.tanh(math.sqrt(2.0 / math.pi) * (x + 0.044715 * (x * x * x))))


def _sigmoid(x):
    return 1.0 / (1.0 + jnp.exp(-x))


def _group_mean_sq(x, gmat, group):
    x2 = x * x
    hi = x2.astype(bf16)
    lo = (x2 - hi.astype(f32)).astype(bf16)
    return (_dot(hi, gmat) + _dot(lo, gmat)) * (1.0 / group)


def _proj_kernel(*refs, has_y):
    if has_y:
        x_ref, ya_ref, yb_ref, g_ref, w_ref, proj_ref, xs_ref, xn_sc = refs
    else:
        x_ref, g_ref, w_ref, proj_ref, xn_sc = refs

    @pl.when(pl.program_id(1) == 0)
    def _():
        x = x_ref[...]
        if has_y:
            x = x + ya_ref[...] + yb_ref[...]
            xs_ref[...] = x
        ms = jnp.mean(x * x, axis=-1, keepdims=True)
        xn_sc[...] = (x * lax.rsqrt(ms + EPS) * g_ref[...]).astype(bf16)

    proj_ref[...] = _dot(xn_sc[...], w_ref[...])


def _proj_call(x, y2, g, w, tm=640, tn=384):
    nt = x.shape[0]
    has_y = y2 is not None
    nrow = nt // tm
    in_specs = [pl.BlockSpec((tm, D_MODEL), lambda i, j: (i, 0))]
    args = [x]
    if has_y:
        in_specs += [pl.BlockSpec((tm, D_MODEL), lambda i, j: (i, 0)),
                     pl.BlockSpec((tm, D_MODEL), lambda i, j: (i + nrow, 0))]
        args += [y2, y2]
    in_specs += [pl.BlockSpec((1, D_MODEL), lambda i, j: (0, 0)),
                 pl.BlockSpec((D_MODEL, tn), lambda i, j: (0, j))]
    args += [g, w]
    out_shape = [jax.ShapeDtypeStruct((nt, PROJ_W), f32)]
    out_specs = [pl.BlockSpec((tm, tn), lambda i, j: (i, j))]
    if has_y:
        out_shape.append(jax.ShapeDtypeStruct((nt, D_MODEL), f32))
        out_specs.append(pl.BlockSpec((tm, D_MODEL), lambda i, j: (i, 0)))
    res = pl.pallas_call(
        functools.partial(_proj_kernel, has_y=has_y),
        grid=(nrow, PROJ_W // tn),
        in_specs=in_specs, out_specs=out_specs, out_shape=out_shape,
        scratch_shapes=[pltpu.VMEM((tm, D_MODEL), bf16)],
        compiler_params=_cparams(("parallel", "arbitrary")),
        name="proj",
    )(*args)
    return (res[0], res[1]) if has_y else (res[0], x)


def _post_kernel(q_ref, sel_ref, win_ref, ug_ref, vg_ref, gt_ref, gq_ref, gsel_ref, gwin_ref, gv_ref,
                 g64_ref, g128_ref, qn_ref, ksel_ref, kwin_ref, gate_ref, uga_ref, vn_ref):
    g64 = g64_ref[...]
    lane = lax.broadcasted_iota(jnp.int32, (1, 512), 1)
    is_k = (lane % 128) < 64
    for half in range(2):
        q = q_ref[:, half * 512:(half + 1) * 512]
        ms = _group_mean_sq(q, g64, 64)
        qn = q * lax.rsqrt(ms + EPS) * gq_ref[:, half * 512:(half + 1) * 512]
        qn_ref[:, half * 512:(half + 1) * 512] = (qn * SCALE).astype(bf16)
    for src, gref, dst in ((sel_ref, gsel_ref, ksel_ref), (win_ref, gwin_ref, kwin_ref)):
        x = src[...]
        ms = _group_mean_sq(x, g64, 64)
        dst[...] = jnp.where(is_k, x * lax.rsqrt(ms + EPS) * gref[...], x)
    gate_ref[...] = _sigmoid(gt_ref[...])
    uga_ref[...] = _gelu(ug_ref[...])
    v = _gelu(vg_ref[...])
    ms = _group_mean_sq(v, g128_ref[...], 128)
    vn_ref[...] = v * lax.rsqrt(ms + EPS) * gv_ref[...]


def _post_call(proj, gq, gsel, gwin, gv, g64, g128, tm=320):
    nt = proj.shape[0]

    def col(width, idx):
        return pl.BlockSpec((tm, width), lambda i: (i, idx))

    def full(shape):
        return pl.BlockSpec(shape, lambda i: (0,) * len(shape))

    def out(width):
        return pl.BlockSpec((tm, width), lambda i: (i, 0))

    return pl.pallas_call(
        _post_kernel,
        grid=(nt // tm,),
        in_specs=[col(1024, 0), col(512, 3), col(512, 4), col(512, 6), col(512, 7), col(128, 32),
                  full((1, 1024)), full((1, 512)), full((1, 512)), full((1, 512)),
                  full((512, 512)), full((512, 512))],
        out_specs=[out(1024), out(512), out(512), out(128), out(512), out(512)],
        out_shape=[jax.ShapeDtypeStruct((nt, 1024), bf16), jax.ShapeDtypeStruct((nt, 512), f32),
                   jax.ShapeDtypeStruct((nt, 512), f32), jax.ShapeDtypeStruct((nt, 128), f32),
                   jax.ShapeDtypeStruct((nt, 512), f32), jax.ShapeDtypeStruct((nt, 512), f32)],
        compiler_params=_cparams(("parallel",)),
        name="post",
    )(proj, proj, proj, proj, proj, proj, gq, gsel, gwin, gv, g64, g128)


CMP_PAGES_PER_STEP = 32


def _cmp_kernel(pt_ref, *refs, pps, nsteps, nchunk):
    srcs = refs[:pps]
    wab_ref, pos_ref, w2_ref, gk_ref, out_ref, c_sc = refs[pps:]
    s = pl.program_id(2)
    for pp in range(pps // 2):
        row0 = pl.multiple_of(s * (pps * 8) + pp * 16, 16)
        for j in range(CMP_STRIDE):
            a = srcs[2 * pp][pl.ds(j, 8, stride=CMP_STRIDE), :]
            b = srcs[2 * pp + 1][pl.ds(j, 8, stride=CMP_STRIDE), :]
            c_sc[pl.ds(row0, 16), j * 128:(j + 1) * 128] = jnp.concatenate([a, b], axis=0).astype(bf16)

    @pl.when(s == nsteps - 1)
    def _():
        wab = wab_ref[...]
        pq = _dot(c_sc[...], wab)
        pos = pos_ref[...].astype(bf16)
        pterm = _dot(pos[:, :2048], wab[:, :128]) + _dot(pos[:, 2048:], wab[:, 128:])
        pre = pq[:, :128] + pltpu.roll(pq[:, 128:], nchunk - 1, 0) + pterm[0:1, :]
        y = _dot(_gelu(pre).astype(bf16), w2_ref[...])
        lane = lax.broadcasted_iota(jnp.int32, (1, 128), 1)
        is_k = lane < 64
        ms = jnp.sum(jnp.where(is_k, y * y, 0.0), axis=-1, keepdims=True) * (1.0 / 64)
        out_ref[...] = jnp.where(is_k, y * lax.rsqrt(ms + EPS) * gk_ref[...], y)


def _cmp_call(pt_flat, src4, layer, col0, nb, npg, wab, pos, w2, gk):
    nchunk = npg * 8
    pps = min(CMP_PAGES_PER_STEP, npg)
    nsteps = npg // pps

    def src_spec(k):
        return pl.BlockSpec((None, None, PAGE_SIZE, 128),
                            lambda b, h, s, pt: (pt[b * npg + s * pps + k], layer, 0, col0 + h))

    def full(shape):
        return pl.BlockSpec(shape, lambda b, h, s, pt: (0,) * len(shape))

    return pl.pallas_call(
        functools.partial(_cmp_kernel, pps=pps, nsteps=nsteps, nchunk=nchunk),
        grid_spec=pltpu.PrefetchScalarGridSpec(
            num_scalar_prefetch=1,
            grid=(nb, N_KV_HEADS, nsteps),
            in_specs=[src_spec(k) for k in range(pps)]
                     + [full((2048, 256)), full((8, 4096)), full((128, 128)), full((1, 128))],
            out_specs=pl.BlockSpec((None, nchunk, 128), lambda b, h, s, pt: (b, 0, h)),
            scratch_shapes=[pltpu.VMEM((nchunk, 2048), bf16)],
        ),
        out_shape=jax.ShapeDtypeStruct((nb, nchunk, 512), f32),
        compiler_params=_cparams(("parallel", "parallel", "arbitrary")),
        name="compress",
    )(pt_flat, *([src4] * pps), wab, pos, w2, gk)


def _pattn_kernel(q_ref, kvc_ref, ksel_ref, kwin_ref, gate_ref, cbt_ref, tzt_ref, ovt_ref, ext_ref, out_ref,
                  kb_sel, vat_sel, kb_win, vat_win, kcb, vcat, sb_sc, gt_sc, m_sc, acc_sel, acc_win, *, tq, t):
    h = pl.program_id(1)
    i = pl.program_id(2)
    c4 = GQA * tq
    nsel = t // SEL_BLOCK
    ntile = t // tq
    nwin = WINDOW // tq
    ncc = t // CMP_STRIDE
    ksel = min(SEL_TOPK, nsel)

    @pl.when(i == 0)
    def _():
        lane = lax.broadcasted_iota(jnp.int32, (1, 128), 1)
        for src, kb, vat in ((ksel_ref, kb_sel, vat_sel), (kwin_ref, kb_win, vat_win)):
            blk = src[...]
            kb[...] = blk[:, :64].astype(bf16)
            va = jnp.where(lane < 64, pltpu.roll(blk, 64, 1), 1.0)
            for jj in range(ntile):
                vat[jj] = jnp.transpose(va[jj * tq:(jj + 1) * tq]).astype(bf16)
        kvc = kvc_ref[...]
        kcb[...] = kvc[:, :64].astype(bf16)
        vcat[...] = jnp.transpose(jnp.where(lane < 64, pltpu.roll(kvc, 64, 1), 0.0)).astype(bf16)

    qt = jnp.transpose(q_ref[...].astype(f32))
    q4t = jnp.concatenate([qt[g * 64:(g + 1) * 64] for g in range(GQA)], axis=1).astype(bf16)
    gt_sc[...] = jnp.transpose(gate_ref[...])

    def lanes4(a):
        return jnp.concatenate([a] * GQA, axis=1)

    nrow = lax.broadcasted_iota(jnp.int32, (ncc, tq), 0)
    qpos = i * tq + lax.broadcasted_iota(jnp.int32, (ncc, tq), 1)
    cvis = lanes4((CMP_STRIDE * nrow + CMP_BLOCK - 1 <= qpos) & (nrow < (t - CMP_BLOCK) // CMP_STRIDE + 1))
    shift = pl.multiple_of((tq // CMP_STRIDE) * (ntile - 1 - i), tq // CMP_STRIDE)
    cb = jnp.concatenate([cbt_ref[g, pl.ds(shift, ncc), :] for g in range(GQA)], axis=1)
    st = jnp.where(cvis, _dot(kcb[...], q4t) + cb, NEG)
    m = jnp.max(st, axis=0, keepdims=True)
    e = jnp.where(cvis, jnp.exp(st - m), 0.0)
    p = e / jnp.maximum(jnp.sum(e, axis=0, keepdims=True), 1e-30)
    o_cmp = _dot(vcat[...], p.astype(bf16))
    pg = p[:, 0:tq] + p[:, tq:2 * tq] + p[:, 2 * tq:3 * tq] + p[:, 3 * tq:4 * tq]
    ovt = ovt_ref[...]
    hi, mid, lo = _split3(pg)
    score = (_dot(ovt, hi) + _dot(ovt, mid) + _dot(ovt, lo))[0:nsel]

    blk = lax.broadcasted_iota(jnp.int32, (nsel, tq), 0)
    qp = i * tq + lax.broadcasted_iota(jnp.int32, (nsel, tq), 1)
    cur = qp // SEL_BLOCK
    forced = (blk == 0) | (blk == cur) | (blk == cur - 1)
    score = jnp.where(forced, 1e6, jnp.where(blk * SEL_BLOCK > qp, -1.0, score))
    rank = jnp.zeros((nsel, tq), f32)
    for sp in range(nsel):
        other = score[sp:sp + 1, :]
        ahead = (other > score) | ((other == score) & (blk > sp))
        rank = rank + jnp.where(ahead, 1.0, 0.0)
    selneg = jnp.where(rank < ksel, 0.0, NEG)
    selneg = jnp.concatenate([selneg, jnp.zeros((128 - nsel, tq), f32)], axis=0).astype(bf16)
    for jj in range(ntile):
        sb_sc[jj] = _dot(ext_ref[jj * tq:(jj + 1) * tq, :], selneg)

    def flash(kb, vat, acc, j, bias):
        k0 = pl.multiple_of(j * tq, tq)
        st = _dot(kb[pl.ds(k0, tq), :], q4t)
        if bias is not None:
            st = st + bias
        m_old = m_sc[...]
        m_new = jnp.maximum(m_old, jnp.max(st, axis=0, keepdims=True))
        alpha = jnp.exp(m_old - m_new)
        pm = jnp.exp(st - m_new).astype(bf16)
        acc[...] = acc[...] * alpha + _dot(vat[j], pm)
        m_sc[...] = m_new

    m_sc[...] = jnp.full((1, c4), NEG, f32)
    acc_sel[...] = jnp.zeros((128, c4), f32)

    def sel_far(j, c):
        flash(kb_sel, vat_sel, acc_sel, j, lanes4(sb_sc[j]))
        return c

    lax.fori_loop(0, jnp.maximum(i - 1, 0), sel_far, 0)

    @pl.when(i >= 1)
    def _():
        flash(kb_sel, vat_sel, acc_sel, i - 1, tzt_ref[1] + lanes4(sb_sc[i - 1]))

    flash(kb_sel, vat_sel, acc_sel, i, tzt_ref[0] + lanes4(sb_sc[i]))

    m_sc[...] = jnp.full((1, c4), NEG, f32)
    acc_win[...] = jnp.zeros((128, c4), f32)

    @pl.when(i >= nwin)
    def _():
        krow = lax.broadcasted_iota(jnp.int32, (tq, tq), 0)
        qcol = lax.broadcasted_iota(jnp.int32, (tq, tq), 1)
        flash(kb_win, vat_win, acc_win, i - nwin, lanes4(jnp.where(qcol <= krow, 0.0, NEG)))

    def win_far(j, c):
        flash(kb_win, vat_win, acc_win, j, None)
        return c

    lax.fori_loop(jnp.maximum(i - nwin + 1, 0), jnp.maximum(i - 1, 0), win_far, 0)

    @pl.when(i >= 1)
    def _():
        flash(kb_win, vat_win, acc_win, i - 1, tzt_ref[1])

    flash(kb_win, vat_win, acc_win, i, tzt_ref[0])

    a_sel = acc_sel[...]
    a_win = acc_win[...]
    o_sel = a_sel[0:64] / a_sel[64:65]
    o_win = a_win[0:64] / a_win[64:65]

    def grow(branch):
        return jnp.concatenate([gt_sc[pl.ds(branch * N_HEADS_A + h * GQA + g, 1), :] for g in range(GQA)], axis=1)

    comb = grow(0) * o_cmp[0:64] + grow(1) * o_sel + grow(2) * o_win
    out_ref[...] = jnp.concatenate([jnp.transpose(comb[:, g * tq:(g + 1) * tq]) for g in range(GQA)], axis=1)


def _pattn_call(qn, kvc, ksel, kwin, gates, cbt, tzt, ovt, ext, nb, t, tq=256):
    ntile = t // tq
    c4 = GQA * tq
    ncc = t // CMP_STRIDE
    grid = (nb, N_KV_HEADS, ntile)
    return pl.pallas_call(
        functools.partial(_pattn_kernel, tq=tq, t=t),
        grid=grid,
        in_specs=[
            pl.BlockSpec((tq, 256), lambda b, h, i: (b * ntile + i, h)),
            pl.BlockSpec((None, ncc, 128), lambda b, h, i: (b, 0, h)),
            pl.BlockSpec((t, 128), lambda b, h, i: (b, h)),
            pl.BlockSpec((t, 128), lambda b, h, i: (b, h)),
            pl.BlockSpec((tq, 128), lambda b, h, i: (b * ntile + i, 0)),
            pl.BlockSpec((GQA, 2 * ncc, tq), lambda b, h, i: (h, 0, 0)),
            pl.BlockSpec((None, 2, tq, c4), lambda b, h, i: (h, 0, 0, 0)),
            pl.BlockSpec((128, ncc), lambda b, h, i: (0, 0)),
            pl.BlockSpec((t, 128), lambda b, h, i: (0, 0)),
        ],
        out_specs=pl.BlockSpec((tq, 256), lambda b, h, i: (b * ntile + i, h)),
        out_shape=jax.ShapeDtypeStruct((nb * t, ATTN_WIDTH), f32),
        scratch_shapes=[
            pltpu.VMEM((t, 64), bf16), pltpu.VMEM((ntile, 128, tq), bf16),
            pltpu.VMEM((t, 64), bf16), pltpu.VMEM((ntile, 128, tq), bf16),
            pltpu.VMEM((ncc, 64), bf16), pltpu.VMEM((128, ncc), bf16),
            pltpu.VMEM((ntile, tq, tq), f32), pltpu.VMEM((128, tq), f32),
            pltpu.VMEM((1, c4), f32), pltpu.VMEM((128, c4), f32), pltpu.VMEM((128, c4), f32),
        ],
        compiler_params=_cparams(("parallel", "parallel", "arbitrary")),
        name="prompt_attn",
    )(qn, kvc, ksel, kwin, gates, cbt, tzt, ovt, ext)


CH_PAGES = 16


def _topk_neg_rows(score, rowf, k):
    neg = jnp.full(score.shape, NEG, f32)
    work = score
    for _ in range(k):
        m = jnp.max(work, axis=0, keepdims=True)
        idx = jnp.min(jnp.where(work == m, rowf, 1e9), axis=0, keepdims=True)
        hit = rowf == idx
        neg = jnp.where(hit, 0.0, neg)
        work = jnp.where(hit, -3e38, work)
    return neg


def _col_of(vec):
    return jnp.transpose(jnp.broadcast_to(vec, (128, 128)))[:, :1]


def _sattn_kernel(pt_ref, qbig_ref, kvc_ref, cbt_ref, ovt_ref, smat_ref, e2_ref, pool_ref, sblast_ref,
                  nsel_ref, nbsel_ref, wcache_ref, wbt_ref, nwin_ref, nbwin_ref, gt_ref, out_ref,
                  buf, sem, seln_sc, m_sc, l_sc, acc_sc, *, layer, npg, n_new):
    b = pl.program_id(0)
    nch = npg // CH_PAGES
    ck = CH_PAGES * PAGE_SIZE
    past = npg * PAGE_SIZE
    qbig = qbig_ref[...]

    def page_copy(c, k, slot):
        pg = pt_ref[b * npg + c * CH_PAGES + k]
        return pltpu.make_async_copy(pool_ref.at[pg, layer], buf.at[slot, k], sem.at[slot])

    def start_chunk(c, slot):
        for k in range(CH_PAGES):
            page_copy(c, k, slot).start()

    def wait_chunk(c, slot):
        for k in range(CH_PAGES):
            page_copy(c, k, slot).wait()

    start_chunk(0, 0)
    if nch > 1:
        start_chunk(1, 1)

    def reset():
        m_sc[...] = jnp.full((1, 128), NEG, f32)
        l_sc[...] = jnp.zeros((1, 128), f32)
        acc_sc[...] = jnp.zeros((128, 512), f32)

    def attend(rows, bias_t):
        rb = rows.astype(bf16)
        st = _dot(rb, qbig) + bias_t
        m_old = m_sc[...]
        m_new = jnp.maximum(m_old, jnp.max(st, axis=0, keepdims=True))
        alpha = jnp.exp(m_old - m_new)
        pt = jnp.exp(st - m_new)
        l_sc[...] = l_sc[...] * alpha + jnp.sum(pt, axis=0, keepdims=True)
        acc_sc[...] = acc_sc[...] * _col_of(alpha) + _dot(jnp.transpose(pt).astype(bf16), rb)
        m_sc[...] = m_new

    rowhead = lax.broadcasted_iota(jnp.int32, (128, 128), 0) // 32

    def own_head(res):
        out = jnp.zeros((128, 128), f32)
        for hh in range(N_KV_HEADS):
            out = jnp.where(rowhead == hh, res[:, hh * 128:(hh + 1) * 128], out)
        return out

    kvc = kvc_ref[...]
    kb = kvc.astype(bf16)
    st = _dot(kb, qbig) + cbt_ref[...]
    m = jnp.max(st, axis=0, keepdims=True)
    e = jnp.exp(st - m)
    pt = e / jnp.maximum(jnp.sum(e, axis=0, keepdims=True), 1e-30)
    o_cmp = own_head(_dot(jnp.transpose(pt).astype(bf16), kb))
    smat = smat_ref[...]
    hi, mid, lo = _split3(pt)
    pg = _dot(hi, smat) + _dot(mid, smat) + _dot(lo, smat)
    hi, mid, lo = _split3(pg)
    ovt = ovt_ref[...]
    score = _dot(ovt, hi) + _dot(ovt, mid) + _dot(ovt, lo)
    nrow = score.shape[0]
    blk = lax.broadcasted_iota(jnp.int32, (nrow, 128), 0)
    qpos = past + lax.broadcasted_iota(jnp.int32, (nrow, 128), 1) % 8
    cur = qpos // SEL_BLOCK
    forced = (blk == 0) | (blk == cur) | (blk == cur - 1)
    future = blk * SEL_BLOCK > qpos
    score = jnp.where(forced, 1e6, jnp.where(future, -1.0, score))
    nsel = -(-(past + n_new) // SEL_BLOCK)
    score = jnp.where(blk < nsel, score, -2.0)
    seln_sc[0:nrow, :] = _topk_neg_rows(score, blk.astype(f32), min(SEL_TOPK, nsel))
    seln_sc[nrow:, :] = jnp.zeros((seln_sc.shape[0] - nrow, 128), f32)

    reset()
    bpc = ck // SEL_BLOCK

    def sel_bias(c):
        r0 = pl.multiple_of(c * bpc, bpc)
        return _dot(e2_ref[...], seln_sc[pl.ds(r0, 128), :].astype(bf16))

    def chunk_body(c, carry):
        slot = c % 2
        wait_chunk(c, slot)
        attend(buf[slot].reshape(ck, 512), sel_bias(c))

        @pl.when(c + 2 < nch)
        def _():
            start_chunk(c + 2, slot)

        return carry

    lax.fori_loop(0, nch - 1, chunk_body, 0)
    last = nch - 1
    wait_chunk(last, last % 2)
    attend(buf[last % 2].reshape(ck, 512), sel_bias(last) + sblast_ref[...])
    attend(nsel_ref[...], nbsel_ref[...] + seln_sc[past // SEL_BLOCK:past // SEL_BLOCK + 1, :])
    o_sel = own_head(acc_sc[...]) / _col_of(l_sc[...])

    reset()
    attend(wcache_ref[...], wbt_ref[...])
    attend(nwin_ref[...], nbwin_ref[...])
    o_win = own_head(acc_sc[...]) / _col_of(l_sc[...])

    gts = gt_ref[...]
    out_ref[...] = gts[:, 0:1] * o_cmp + gts[:, 1:2] * o_sel + gts[:, 2:3] * o_win


def _sattn_call(pt_flat, qbig, kvc, cbt, ovt, smat, e2, pool, sblast, nsel, nbsel, wcache, wbt, nwin, nbwin,
                gt, layer, nb, npg, n_new):
    nrow = ovt.shape[0]
    ck = CH_PAGES * PAGE_SIZE

    def full(shape):
        return pl.BlockSpec(shape, lambda b, pt: (0,) * len(shape))

    def perb(shape):
        return pl.BlockSpec((None,) + shape, lambda b, pt: (b,) + (0,) * len(shape))

    return pl.pallas_call(
        functools.partial(_sattn_kernel, layer=layer, npg=npg, n_new=n_new),
        grid_spec=pltpu.PrefetchScalarGridSpec(
            num_scalar_prefetch=1,
            grid=(nb,),
            in_specs=[perb((512, 128)), perb((npg * 8, 512)), full(cbt.shape), full(ovt.shape), full((128, 128)),
                      full((ck, 128)), pl.BlockSpec(memory_space=pl.ANY), full((ck, 128)),
                      perb((128, 512)), full((128, 128)),
                      pl.BlockSpec((None, None, WINDOW, 512), lambda b, pt: (b, layer, 0, 0)), full((WINDOW, 128)),
                      perb((128, 512)), full((128, 128)), perb((128, 128))],
            out_specs=perb((128, 128)),
            scratch_shapes=[pltpu.VMEM((2, CH_PAGES, PAGE_SIZE, 512), f32), pltpu.SemaphoreType.DMA((2,)),
                            pltpu.VMEM((nrow + 128, 128), f32), pltpu.VMEM((1, 128), f32), pltpu.VMEM((1, 128), f32),
                            pltpu.VMEM((128, 512), f32)],
        ),
        out_shape=jax.ShapeDtypeStruct((nb, 128, 128), f32),
        compiler_params=_cparams(("arbitrary",)),
        name="sample_attn",
    )(pt_flat, qbig, kvc, cbt, ovt, smat, e2, pool, sblast, nsel, nbsel, wcache, wbt, nwin, nbwin, gt)


def _s5_kernel(*refs, nb, tc):
    u_refs = refs[:nb]
    (h0r_ref, h0i_ref, abr_ref, abi_ref, bre_ref, bim_ref, cre_ref, cim_ref, d_ref, gw_ref, gb_ref,
     o_ref, hr_out, hi_out, xr, xi, hr, hi) = refs[nb:]
    c = pl.program_id(0)

    @pl.when(c == 0)
    def _():
        hr[...] = h0r_ref[...]
        hi[...] = h0i_ref[...]

    for b in range(nb):
        u = u_refs[b][...].astype(bf16)
        for cb in range(4):
            ub = u[:, cb * 128:(cb + 1) * 128]
            pr = _dot(ub, bre_ref[cb])
            pi = _dot(ub, bim_ref[cb])
            for k in range(4):
                xr[cb * 4 + k, b * tc:(b + 1) * tc, :] = pr[:, k * 128:(k + 1) * 128]
                xi[cb * 4 + k, b * tc:(b + 1) * tc, :] = pi[:, k * 128:(k + 1) * 128]

    for lc in range(4):
        slabs = [lc * 4 + k for k in range(4)]
        ar = [abr_ref[0:nb, j * 128:(j + 1) * 128] for j in slabs]
        ai = [abi_ref[0:nb, j * 128:(j + 1) * 128] for j in slabs]

        def body(t, carry, slabs=slabs, ar=ar, ai=ai):
            rows = pl.ds(t, nb, stride=tc)
            new = []
            for k, j in enumerate(slabs):
                cr, ci = carry[2 * k], carry[2 * k + 1]
                nr = ar[k] * cr - ai[k] * ci + xr[j, rows, :]
                ni = ar[k] * ci + ai[k] * cr + xi[j, rows, :]
                xr[j, rows, :] = nr
                xi[j, rows, :] = ni
                new += [nr, ni]
            return tuple(new)

        init = []
        for j in slabs:
            init += [hr[0:nb, j * 128:(j + 1) * 128], hi[0:nb, j * 128:(j + 1) * 128]]
        fin = lax.fori_loop(0, tc, body, tuple(init), unroll=4)
        for k, j in enumerate(slabs):
            hr[0:nb, j * 128:(j + 1) * 128] = fin[2 * k]
            hi[0:nb, j * 128:(j + 1) * 128] = fin[2 * k + 1]

    ys = []
    for cb in range(4):
        hre = jnp.concatenate([xr[cb * 4 + k] for k in range(4)], axis=1).astype(bf16)
        him = jnp.concatenate([xi[cb * 4 + k] for k in range(4)], axis=1).astype(bf16)
        ys.append(_dot(hre, cre_ref[cb]) - _dot(him, cim_ref[cb]))
    u_all = jnp.concatenate([u_refs[b][...] for b in range(nb)], axis=0)
    y = _gelu(jnp.concatenate(ys, axis=1) + d_ref[...] * u_all)
    o = y * _sigmoid(_dot(y.astype(bf16), gw_ref[...]) + gb_ref[...])
    for b in range(nb):
        o_ref[b] = o[b * tc:(b + 1) * tc]

    @pl.when(c == pl.num_programs(0) - 1)
    def _():
        hr_out[...] = hr[...]
        hi_out[...] = hi[...]


def _s5_call(proj, row0, nb, t, tc, h0r, h0i, abr, abi, bre, bim, cre, cim, d, gw, gb):
    def full(shape):
        return pl.BlockSpec(shape, lambda c: (0,) * len(shape))

    u_specs = [pl.BlockSpec((tc, 512), lambda c, b=b: ((row0 + b * t) // tc + c, 5)) for b in range(nb)]
    return pl.pallas_call(
        functools.partial(_s5_kernel, nb=nb, tc=tc),
        grid=(t // tc,),
        in_specs=u_specs + [full((8, 2048)), full((8, 2048)), full((8, 2048)), full((8, 2048)),
                            full((4, 128, 512)), full((4, 128, 512)), full((4, 512, 128)), full((4, 512, 128)),
                            full((1, 512)), full((512, 512)), full((1, 512))],
        out_specs=[pl.BlockSpec((nb, tc, 512), lambda c: (0, c, 0)), full((8, 2048)), full((8, 2048))],
        out_shape=[jax.ShapeDtypeStruct((nb, t, 512), f32), jax.ShapeDtypeStruct((8, 2048), f32),
                   jax.ShapeDtypeStruct((8, 2048), f32)],
        scratch_shapes=[pltpu.VMEM((16, nb * tc, 128), f32), pltpu.VMEM((16, nb * tc, 128), f32),
                        pltpu.VMEM((8, 2048), f32), pltpu.VMEM((8, 2048), f32)],
        compiler_params=_cparams(("arbitrary",)),
        name="s5",
    )(*([proj] * nb), h0r, h0i, abr, abi, bre, bim, cre, cim, d, gw, gb)


def _gmlp_kernel(u_ref, v_ref, w_ref, bs_ref, o_ref, *, rows):
    u = u_ref[...]
    v = v_ref[...]
    outs = []
    for g in range(GMLP_GROUPS):
        vg = v[:, g * 128:(g + 1) * 128]
        if rows < CHUNK:
            vg = jnp.concatenate([vg, jnp.zeros((CHUNK - rows, 128), f32)], axis=0)
        mixed = _dot(w_ref[g][0:rows, :], vg.astype(bf16)) + bs_ref[0:rows, g:g + 1]
        outs.append(u[:, g * 128:(g + 1) * 128] * mixed)
    o_ref[...] = jnp.concatenate(outs, axis=1)


def _gmlp_call(uga, vn, w, bs, row0, nrows, rows):
    blk0 = row0 // rows

    def full(shape):
        return pl.BlockSpec(shape, lambda i: (0,) * len(shape))

    return pl.pallas_call(
        functools.partial(_gmlp_kernel, rows=rows),
        grid=(nrows // rows,),
        in_specs=[pl.BlockSpec((rows, 512), lambda i: (blk0 + i, 0)), pl.BlockSpec((rows, 512), lambda i: (blk0 + i, 0)),
                  full((GMLP_GROUPS, CHUNK, CHUNK)), full((CHUNK, 128))],
        out_specs=pl.BlockSpec((rows, 512), lambda i: (i, 0)),
        out_shape=jax.ShapeDtypeStruct((nrows, 512), f32),
        compiler_params=_cparams(("parallel",)),
        name="gmlp",
    )(uga, vn, w, bs)


def _merge_kernel(x_ref, oa_ref, ob_ref, oc_ref, gn_ref, w_ref, gf_ref, rwh_ref, rwl_ref, rb_ref,
                  x1_ref, xn_ref, rt_ref):
    def rms(v, g):
        return (v * lax.rsqrt(jnp.mean(v * v, axis=-1, keepdims=True) + EPS) * g).astype(bf16)

    acc = _dot(rms(oa_ref[...], gn_ref[:, 0:1024]), w_ref[0:1024, :])
    acc += _dot(rms(ob_ref[...], gn_ref[:, 1024:1536]), w_ref[1024:1536, :])
    acc += _dot(rms(oc_ref[...], gn_ref[:, 1536:2048]), w_ref[1536:2048, :])
    x1 = x_ref[...] + acc
    x1_ref[...] = x1
    xn = x1 * lax.rsqrt(jnp.mean(x1 * x1, axis=-1, keepdims=True) + EPS) * gf_ref[...]
    xn_ref[...] = xn
    hi = xn.astype(bf16)
    lo = (xn - hi.astype(f32)).astype(bf16)
    logits = _dot(hi, rwh_ref[...]) + _dot(lo, rwh_ref[...]) + _dot(hi, rwl_ref[...]) + rb_ref[...]
    lane = lax.broadcasted_iota(jnp.int32, logits.shape, 1)
    lanef = lane.astype(f32)
    is_g = lane < MOE_GROUPS
    gl = jnp.where(is_g, logits, NEG)
    gm = jnp.max(gl, axis=-1, keepdims=True)
    gidx = jnp.min(jnp.where(gl == gm, lanef, 1e9), axis=-1, keepdims=True)
    gprob = 1.0 / jnp.sum(jnp.where(is_g, jnp.exp(logits - gm), 0.0), axis=-1, keepdims=True)
    lo_lane = MOE_GROUPS + EXPERTS_PER_GROUP * gidx
    inl = jnp.where((lanef >= lo_lane) & (lanef < lo_lane + EXPERTS_PER_GROUP), logits, NEG)
    v1 = jnp.max(inl, axis=-1, keepdims=True)
    i1 = jnp.min(jnp.where(inl == v1, lanef, 1e9), axis=-1, keepdims=True)
    inl2 = jnp.where(lanef == i1, NEG, inl)
    v2 = jnp.max(inl2, axis=-1, keepdims=True)
    i2 = jnp.min(jnp.where(inl2 == v2, lanef, 1e9), axis=-1, keepdims=True)
    e2 = jnp.exp(v2 - v1)
    w1 = gprob / (1.0 + e2)
    w2 = gprob * e2 / (1.0 + e2)
    rt_ref[...] = jnp.where(lane == 0, i1 - MOE_GROUPS,
                            jnp.where(lane == 1, i2 - MOE_GROUPS,
                                      jnp.where(lane == 2, w1, jnp.where(lane == 3, w2, 0.0))))


def _merge_call(x, oa, ob, oc, gn, w, gf, rwh, rwl, rb, tm=320):
    nt = x.shape[0]

    def row(width):
        return pl.BlockSpec((tm, width), lambda i: (i, 0))

    def full(shape):
        return pl.BlockSpec(shape, lambda i: (0,) * len(shape))

    return pl.pallas_call(
        _merge_kernel,
        grid=(nt // tm,),
        in_specs=[row(2048), row(1024), row(512), row(512), full((1, 2048)), full((2048, 2048)), full((1, 2048)),
                  full((2048, 128)), full((2048, 128)), full((1, 128))],
        out_specs=[row(2048), row(2048), row(128)],
        out_shape=[jax.ShapeDtypeStruct((nt, 2048), f32), jax.ShapeDtypeStruct((nt, 2048), f32),
                   jax.ShapeDtypeStruct((nt, 128), f32)],
        compiler_params=_cparams(("parallel",)),
        name="merge_router",
    )(x, oa, ob, oc, gn, w, gf, rwh, rwl, rb)


def _expert_kernel(te_ref, nu_ref, src_ref, dst_ref, xn_hbm, wg_ref, w1_ref, w3_ref, w2_ref, y_hbm,
                   xbuf, obuf, gsem, ssem, *, tm, ntiles):
    t = pl.program_id(0)
    n_used = nu_ref[0]
    slot = t % 2

    def gather(tt, sl):
        def body(r, c):
            tok = src_ref[tt * tm + r]
            pltpu.make_async_copy(xn_hbm.at[pl.ds(tok, 1)], xbuf.at[sl, pl.ds(r, 1)], gsem.at[sl]).start()
            return c
        lax.fori_loop(0, tm, body, 0, unroll=8)

    def gather_wait(sl):
        pltpu.make_async_copy(xn_hbm.at[pl.ds(0, tm)], xbuf.at[sl], gsem.at[sl]).wait()

    def scatter(tt, sl):
        def body(r, c):
            row = dst_ref[tt * tm + r]
            pltpu.make_async_copy(obuf.at[sl, pl.ds(r, 1)], y_hbm.at[pl.ds(row, 1)], ssem.at[sl]).start()
            return c
        lax.fori_loop(0, tm, body, 0, unroll=8)

    def scatter_wait(sl):
        pltpu.make_async_copy(obuf.at[sl], y_hbm.at[pl.ds(0, tm)], ssem.at[sl]).wait()

    @pl.when(t == 0)
    def _():
        gather(0, 0)
        npair = y_hbm.shape[0] - 2 * tm
        for sl in range(2):
            obuf[sl] = jnp.zeros((tm, D_MODEL), f32)
            cp = pltpu.make_async_copy(obuf.at[sl], y_hbm.at[pl.ds(npair + sl * tm, tm)], ssem.at[sl])
            cp.start()
            cp.wait()

    @pl.when(t + 1 < n_used)
    def _():
        gather(t + 1, 1 - slot)

    @pl.when(t < n_used)
    def _():
        gather_wait(slot)

        @pl.when(t >= 2)
        def _():
            scatter_wait(slot)

        x = xbuf[slot].astype(bf16)
        a = _dot(x, w1_ref[...])
        hmid = a * _sigmoid(a) * _dot(x, w3_ref[...]) * wg_ref[...]
        obuf[slot] = _dot(hmid.astype(bf16), w2_ref[...])
        scatter(t, slot)

    @pl.when(t == ntiles - 1)
    def _():
        scatter_wait((n_used - 1) % 2)

        @pl.when(n_used >= 2)
        def _():
            scatter_wait(n_used % 2)


def _expert_call(tile_e, n_used, src_tok, dst_row, xn, wgt, w1, w3, w2, tm, ntiles, nrows_out):
    def wmap(t, te, nu, src, dst):
        return (te[t], 0, 0)

    return pl.pallas_call(
        functools.partial(_expert_kernel, tm=tm, ntiles=ntiles),
        grid_spec=pltpu.PrefetchScalarGridSpec(
            num_scalar_prefetch=4,
            grid=(ntiles,),
            in_specs=[pl.BlockSpec(memory_space=pl.ANY),
                      pl.BlockSpec((tm, 1), lambda t, te, nu, src, dst: (t, 0)),
                      pl.BlockSpec((None, D_MODEL, D_EXPERT), wmap),
                      pl.BlockSpec((None, D_MODEL, D_EXPERT), wmap),
                      pl.BlockSpec((None, D_EXPERT, D_MODEL), wmap)],
            out_specs=pl.BlockSpec(memory_space=pl.ANY),
            scratch_shapes=[pltpu.VMEM((2, tm, D_MODEL), f32), pltpu.VMEM((2, tm, D_MODEL), f32),
                            pltpu.SemaphoreType.DMA((2,)), pltpu.SemaphoreType.DMA((2,))],
        ),
        out_shape=jax.ShapeDtypeStruct((nrows_out, D_MODEL), f32),
        compiler_params=_cparams(("arbitrary",), disable_bounds_checks=True),
        name="experts",
    )(tile_e, n_used, src_tok, dst_row, xn, wgt, w1, w3, w2)


def _add3_kernel(x_ref, ya_ref, yb_ref, o_ref):
    o_ref[...] = x_ref[...] + ya_ref[...] + yb_ref[...]


def _add3_call(x, y2, tm=640):
    nt = x.shape[0]
    nrow = nt // tm
    return pl.pallas_call(
        _add3_kernel,
        grid=(nrow,),
        in_specs=[pl.BlockSpec((tm, D_MODEL), lambda i: (i, 0)), pl.BlockSpec((tm, D_MODEL), lambda i: (i, 0)),
                  pl.BlockSpec((tm, D_MODEL), lambda i: (i + nrow, 0))],
        out_specs=pl.BlockSpec((tm, D_MODEL), lambda i: (i, 0)),
        out_shape=jax.ShapeDtypeStruct((nt, D_MODEL), f32),
        compiler_params=_cparams(("parallel",)),
        name="residual_add",
    )(x, y2, y2)


EXPERT_TM = 256
PROMPT_TQ = 256


def _bucket(n):
    max_exact = NUM_BUCKETS // 2
    nf = jnp.maximum(n, max_exact).astype(f32)
    large = max_exact + (jnp.log(nf / max_exact) / math.log(MAX_DISTANCE / max_exact)
                         * (NUM_BUCKETS - max_exact)).astype(jnp.int32)
    return jnp.where(n < max_exact, n, jnp.minimum(large, NUM_BUCKETS - 1))


def _bias_tables(rel_bias, tq, t, past, n_new):
    bd = rel_bias[_bucket(jnp.arange(128))] - rel_bias[NUM_BUCKETS - 1][None, :]

    def look(dist):
        oh = jax.nn.one_hot(jnp.clip(dist, 0, 127), 128, dtype=f32)
        return jnp.einsum('...d,dh->...h', oh, bd, precision=lax.Precision.HIGHEST)

    ntile = t // tq
    r = jnp.arange(tq)
    d0 = r[:, None] - r[None, :]
    t0 = jnp.where((d0 >= 0)[..., None], look(d0), NEG)
    t1 = look(tq + d0)
    tz = jnp.stack([t0, t1], axis=0).reshape(2, tq, tq, N_KV_HEADS, GQA)
    tz = jnp.transpose(tz, (3, 0, 2, 4, 1)).reshape(N_KV_HEADS, 2, tq, GQA * tq)
    n = jnp.arange(2 * (t // CMP_STRIDE))
    cbl = jnp.transpose(look(tq * (ntile - 1) + r[None, :] - CMP_STRIDE * n[:, None] - (CMP_BLOCK - 1)), (2, 0, 1))

    c = jnp.arange(128)
    head_oh = jax.nn.one_hot(4 * (c // 32) + (c % 32) // 8, N_HEADS_A, dtype=f32)
    qq = c % 8

    def look_t(dist, vis):
        tab = jnp.einsum('mch,ch->mc', look(dist), head_oh, precision=lax.Precision.HIGHEST)
        return jnp.where(vis, tab, NEG)

    n_cmp_s = (past + n_new - CMP_BLOCK) // CMP_STRIDE + 1
    nchunk = past // CMP_STRIDE
    nn = jnp.arange(nchunk)[:, None]
    dist = past + qq[None, :] - (CMP_STRIDE * nn + CMP_BLOCK - 1)
    cbt = look_t(dist, (dist >= 0) & (nn < n_cmp_s))
    ck = CH_PAGES * PAGE_SIZE
    rr = jnp.arange(ck)[:, None]
    dist = qq[None, :] + ck - rr
    sblast = look_t(dist, dist >= 0)
    r128 = jnp.arange(128)[:, None]
    dist = qq[None, :] - r128
    nbnew = look_t(dist, (dist >= 0) & (r128 < n_new))
    rw = jnp.arange(WINDOW)[:, None]
    dist = WINDOW + qq[None, :] - rw
    wbt = look_t(dist, (dist >= 0) & (dist <= WINDOW))
    return tz, cbl, cbt, sblast, nbnew, wbt


def _static_mats(t, past, n_new):
    n = np.arange(128)
    n_cmp = (t - CMP_BLOCK) // CMP_STRIDE + 1
    s = np.arange(128)
    ov = ((CMP_STRIDE * n[:, None] < SEL_BLOCK * s[None, :] + SEL_BLOCK)
          & (CMP_STRIDE * n[:, None] + CMP_BLOCK > SEL_BLOCK * s[None, :])
          & (n[:, None] < n_cmp) & (s[None, :] < t // SEL_BLOCK))
    ex = (np.arange(t)[None, :] // SEL_BLOCK == s[:, None])
    nsel_s = -(-(past + n_new) // SEL_BLOCK)
    nrow = -(-nsel_s // 8) * 8
    n_cmp_s = (past + n_new - CMP_BLOCK) // CMP_STRIDE + 1
    ss = np.arange(nrow)[:, None]
    ns = np.arange(past // CMP_STRIDE)[None, :]
    ovt = ((CMP_STRIDE * ns < SEL_BLOCK * ss + SEL_BLOCK) & (CMP_STRIDE * ns + CMP_BLOCK > SEL_BLOCK * ss)
           & (ns < n_cmp_s) & (ss < nsel_s))
    c = np.arange(128)
    smat = (c[:, None] // 32 == c[None, :] // 32) & (c[:, None] % 8 == c[None, :] % 8)
    ck = CH_PAGES * PAGE_SIZE
    e2 = (np.arange(ck)[:, None] // SEL_BLOCK == np.arange(128)[None, :])
    cvt = lambda a: jnp.asarray(a.astype(np.float32), dtype=bf16)
    return cvt(ov.T), cvt(ex.T), cvt(ovt), cvt(smat), cvt(e2)


def _route_metadata(route, nt, tm, ntiles):
    e_flat = jnp.concatenate([route[:, 0], route[:, 1]]).astype(jnp.int32)
    w_flat = jnp.concatenate([route[:, 2], route[:, 3]])
    npair = 2 * nt
    order = jnp.argsort(e_flat, stable=True).astype(jnp.int32)
    counts = jnp.sum(jax.nn.one_hot(e_flat, N_EXPERTS, dtype=jnp.int32), axis=0)
    tiles_e = (counts + tm - 1) // tm
    tend = jnp.cumsum(tiles_e)
    tstart = tend - tiles_e
    cstart = jnp.cumsum(counts) - counts
    n_used = tend[-1]
    tidx = jnp.arange(ntiles, dtype=jnp.int32)
    tile_e = jnp.sum((jnp.minimum(tidx, n_used - 1)[:, None] >= tend[None, :]).astype(jnp.int32), axis=1)
    tile_oh = jax.nn.one_hot(tile_e, N_EXPERTS, dtype=jnp.int32)
    t_cnt = jnp.sum(tile_oh * counts[None, :], axis=1)
    t_first = jnp.sum(tile_oh * (cstart - tstart * tm)[None, :], axis=1) + tidx * tm
    rows = jnp.arange(tm, dtype=jnp.int32)[None, :]
    rank = tidx[:, None] * tm + rows - jnp.sum(tile_oh * tstart[None, :], axis=1)[:, None] * tm
    valid = (rank < t_cnt[:, None]) & (tidx[:, None] < n_used)
    pair = order[jnp.clip(t_first[:, None] + rows, 0, npair - 1)]
    dump = npair + (tidx[:, None] % 2) * tm + rows
    src_tok = jnp.where(valid, pair % nt, 0).reshape(-1)
    dst_row = jnp.where(valid, pair, dump).reshape(-1)
    wgt = jnp.where(valid, w_flat[pair], 0.0).reshape(-1, 1)
    return tile_e.astype(jnp.int32), n_used.reshape(1).astype(jnp.int32), src_tok, dst_row, wgt


def _block_diag(blocks):
    n, r, c = blocks.shape
    return jnp.einsum('grc,gk->grkc', blocks, jnp.eye(n, dtype=blocks.dtype)).reshape(n * r, n * c)


def kernel(x_prompt, x_sample, cache_kv_cmp, cache_kv_sel, cache_kv_win, state_ssm_re, state_ssm_im, page_table, rel_bias, norm_mix, w_in, qk_norm, cmp_pos, cmp_w1, cmp_w2, ssm_a_re, ssm_a_im, ssm_log_dt, ssm_b_re, ssm_b_im, ssm_c_re, ssm_c_im, ssm_d, ssm_glu_w, ssm_glu_b, gmlp_norm, gmlp_ws, gmlp_bs, out_norm, w_out, norm_ffn, router_group_w, router_group_b, router_expert_w, router_expert_b, expert_w1, expert_w3, expert_w2):
    bp, t, _ = x_prompt.shape
    bs, s_new, _ = x_sample.shape
    npg = page_table.shape[1]
    past = npg * PAGE_SIZE
    nphys = cache_kv_cmp.shape[0]
    n_p, n_s = bp * t, bs * s_new
    nt = -(-(n_p + n_s) // 640) * 640
    tq = PROMPT_TQ
    assert t % tq == 0 and WINDOW % tq == 0 and s_new == 8 and bs == 8 and npg % CH_PAGES == 0

    x = jnp.concatenate([x_prompt.reshape(n_p, D_MODEL), x_sample.reshape(n_s, D_MODEL),
                         jnp.zeros((nt - n_p - n_s, D_MODEL), f32)], axis=0)
    y2 = None
    pool_cmp = cache_kv_cmp.reshape(nphys, DEPTH, PAGE_SIZE, 512)
    pool_sel = cache_kv_sel.reshape(nphys, DEPTH, PAGE_SIZE, 512)
    wcache = cache_kv_win.reshape(bs, DEPTH, WINDOW, 512)
    pt_flat = page_table.reshape(-1).astype(jnp.int32)
    pt_ident = jnp.arange(bp * (t // PAGE_SIZE), dtype=jnp.int32)

    tz, cbl, cbt, sblast, nbnew, wbt = _bias_tables(rel_bias, tq, t, past, s_new)
    ov, ex, ovt, smat, e2 = _static_mats(t, past, s_new)
    g64 = _block_diag(jnp.ones((8, 64, 64), bf16))
    g128 = _block_diag(jnp.ones((4, 128, 128), bf16))
    ones64 = jnp.ones((HEAD_DIM,), f32)
    tril = jnp.tril(jnp.ones((CHUNK, CHUNK), f32))
    zeros_state = jnp.zeros((8, SSM_GROUPS * SSM_STATE), f32)
    ntiles = 2 * nt // EXPERT_TM + N_EXPERTS
    pad_rows = lambda a: jnp.concatenate([a, jnp.zeros((nt - a.shape[0],) + a.shape[1:], a.dtype)], axis=0)

    outs = {k: [] for k in ('pc', 'ps', 'pw', 'pr', 'pi', 'pv', 'sc', 'ss', 'sw', 'sr', 'si', 'sv')}
    for l in range(DEPTH):
        wl = w_in[l]
        o1, o2, o3 = ATTN_WIDTH, ATTN_WIDTH + 6 * KV_WIDTH, ATTN_WIDTH + 6 * KV_WIDTH + 3 * N_HEADS_A
        w_pad = jnp.concatenate([wl[:, :o2], wl[:, o3:], wl[:, o2:o3],
                                 jnp.zeros((D_MODEL, 128 - 3 * N_HEADS_A), f32)], axis=1).astype(bf16)
        gq = jnp.tile(qk_norm[l, 0], N_HEADS_A).reshape(1, 1024)
        gsel = jnp.tile(jnp.concatenate([qk_norm[l, 2], ones64]), N_KV_HEADS).reshape(1, 512)
        gwin = jnp.tile(jnp.concatenate([qk_norm[l, 3], ones64]), N_KV_HEADS).reshape(1, 512)
        gk = jnp.concatenate([qk_norm[l, 1], ones64]).reshape(1, 128)
        w1k = cmp_w1[l, 0].reshape(CMP_BLOCK, HEAD_DIM, HEAD_DIM)
        w1v = cmp_w1[l, 1].reshape(CMP_BLOCK, HEAD_DIM, HEAD_DIM)
        zz = jnp.zeros_like(w1k)
        wfull = jnp.concatenate([jnp.concatenate([w1k, zz], axis=2), jnp.concatenate([zz, w1v], axis=2)], axis=1)
        wab = jnp.concatenate([wfull[:16].reshape(2048, 128), wfull[16:].reshape(2048, 128)], axis=1).astype(bf16)
        posf = jnp.concatenate([cmp_pos[l, 0], cmp_pos[l, 1]], axis=1)
        pos8 = jnp.broadcast_to(jnp.concatenate([posf[:16].reshape(1, 2048), posf[16:].reshape(1, 2048)], axis=1),
                                (8, 4096))
        w2bd = _block_diag(cmp_w2[l]).astype(bf16)

        dt = jnp.exp(ssm_log_dt[l])[:, None]
        a_re, a_im = ssm_a_re[l], ssm_a_im[l]
        mag = jnp.exp(dt * a_re)
        ab_re, ab_im = mag * jnp.cos(dt * a_im), mag * jnp.sin(dt * a_im)
        den = a_re * a_re + a_im * a_im
        f_re = ((ab_re - 1.0) * a_re + ab_im * a_im) / den
        f_im = (ab_im * a_re - (ab_re - 1.0) * a_im) / den
        bb_re = f_re[..., None] * ssm_b_re[l] - f_im[..., None] * ssm_b_im[l]
        bb_im = f_re[..., None] * ssm_b_im[l] + f_im[..., None] * ssm_b_re[l]
        abr = jnp.broadcast_to(ab_re.reshape(1, -1), (8, SSM_GROUPS * SSM_STATE))
        abi = jnp.broadcast_to(ab_im.reshape(1, -1), (8, SSM_GROUPS * SSM_STATE))
        eye8 = jnp.eye(8, dtype=f32)

        def in_blocks(bb):
            xx = jnp.transpose(bb, (0, 2, 1)).reshape(4, 8, SSM_GROUP, SSM_STATE)
            return jnp.einsum('agcn,gk->agckn', xx, eye8).reshape(4, 128, 512).astype(bf16)

        def out_blocks(cc):
            yy = jnp.transpose(cc, (0, 2, 1)).reshape(4, 8, SSM_STATE, SSM_GROUP)
            return jnp.einsum('agnc,gk->agnkc', yy, eye8).reshape(4, 512, 128).astype(bf16)

        s5p = (abr, abi, in_blocks(bb_re), in_blocks(bb_im), out_blocks(ssm_c_re[l]), out_blocks(ssm_c_im[l]),
               ssm_d[l].reshape(1, 512), ssm_glu_w[l].astype(bf16), ssm_glu_b[l].reshape(1, 512))
        gw = (gmlp_ws[l] * tril).astype(bf16)
        gbs = jnp.concatenate([gmlp_bs[l].T, jnp.zeros((CHUNK, 128 - GMLP_GROUPS), f32)], axis=1)
        rw = jnp.concatenate([router_group_w[l], jnp.transpose(router_expert_w[l], (1, 0, 2)).reshape(D_MODEL, N_EXPERTS),
                              jnp.zeros((D_MODEL, 128 - MOE_GROUPS - N_EXPERTS), f32)], axis=1)
        rwh = rw.astype(bf16)
        rwl = (rw - rwh.astype(f32)).astype(bf16)
        rb = jnp.concatenate([router_group_b[l], router_expert_b[l].reshape(-1),
                              jnp.zeros((128 - MOE_GROUPS - N_EXPERTS,), f32)]).reshape(1, 128)

        proj, x = _proj_call(x, y2, norm_mix[l].reshape(1, D_MODEL), w_pad)
        qn, ksel, kwin, gates, uga, vn = _post_call(proj, gq, gsel, gwin, gmlp_norm[l].reshape(1, 512), g64, g128)

        src_p = proj.reshape(nt // PAGE_SIZE, 1, PAGE_SIZE, PROJ_W)
        kvc_p = _cmp_call(pt_ident, src_p, 0, ATTN_WIDTH // 128, bp, t // PAGE_SIZE, wab, pos8, w2bd, gk)
        kvc_s = _cmp_call(pt_flat, pool_cmp, l, 0, bs, npg, wab, pos8, w2bd, gk)
        oa_p = _pattn_call(qn, kvc_p, ksel, kwin, gates, cbl, tz, ov, ex, bp, t, tq)

        qs = qn[n_p:n_p + n_s].reshape(bs, s_new, N_KV_HEADS, GQA, HEAD_DIM)
        qa = jnp.transpose(qs, (0, 2, 4, 3, 1)).reshape(bs, N_KV_HEADS, HEAD_DIM, GQA * s_new)
        qbig = jnp.einsum('bhdc,hk->bhdkc', qa, jnp.eye(N_KV_HEADS, dtype=bf16))
        qbig = jnp.pad(qbig, ((0, 0), (0, 0), (0, 64), (0, 0), (0, 0))).reshape(bs, 512, 128)
        gs = gates[n_p:n_p + n_s, :48].reshape(bs, s_new, 3, N_KV_HEADS, GQA)
        gt = jnp.pad(jnp.transpose(gs, (0, 3, 4, 1, 2)).reshape(bs, 128, 3), ((0, 0), (0, 0), (0, 125)))
        new_sel = jnp.pad(ksel[n_p:n_p + n_s].reshape(bs, s_new, 512), ((0, 0), (0, 128 - s_new), (0, 0)))
        new_win = jnp.pad(kwin[n_p:n_p + n_s].reshape(bs, s_new, 512), ((0, 0), (0, 128 - s_new), (0, 0)))
        osmp = _sattn_call(pt_flat, qbig, kvc_s, cbt, ovt, smat, e2, pool_sel, sblast, new_sel, nbnew,
                           wcache, wbt, new_win, nbnew, gt, l, bs, npg, s_new)
        oa_s = jnp.transpose(osmp[:, :, 64:].reshape(bs, N_KV_HEADS, GQA, s_new, HEAD_DIM),
                             (0, 3, 1, 2, 4)).reshape(n_s, ATTN_WIDTH)
        oa = pad_rows(jnp.concatenate([oa_p, oa_s], axis=0))

        ob_p, hr_p, hi_p = _s5_call(proj, 0, bp, t, 256, zeros_state, zeros_state, *s5p)
        ob_s, hr_s, hi_s = _s5_call(proj, n_p, bs, s_new, s_new, state_ssm_re[:, l].reshape(bs, -1),
                                    state_ssm_im[:, l].reshape(bs, -1), *s5p)
        ob = pad_rows(jnp.concatenate([ob_p.reshape(n_p, 512), ob_s.reshape(n_s, 512)], axis=0))

        oc_p = _gmlp_call(uga, vn, gw, gbs, 0, n_p, CHUNK)
        oc_s = _gmlp_call(uga, vn, gw, gbs, n_p, n_s, s_new)
        oc = pad_rows(jnp.concatenate([oc_p, oc_s], axis=0))

        x1, xn2, route = _merge_call(x, oa, ob, oc, out_norm[l].reshape(1, -1), w_out[l].astype(bf16),
                                     norm_ffn[l].reshape(1, -1), rwh, rwl, rb)
        tile_e, n_used, src_tok, dst_row, wgt = _route_metadata(route, nt, EXPERT_TM, ntiles)
        y2 = _expert_call(tile_e, n_used, src_tok, dst_row, xn2, wgt, expert_w1[l].astype(bf16),
                          expert_w3[l].astype(bf16), expert_w2[l].astype(bf16), EXPERT_TM, ntiles,
                          2 * nt + 2 * EXPERT_TM)
        x = x1

        kvshape = (N_KV_HEADS, 2, HEAD_DIM)
        cmp_rows = proj[:, ATTN_WIDTH:ATTN_WIDTH + 512]
        outs['pc'].append(cmp_rows[:n_p].reshape(bp, t, *kvshape))
        outs['ps'].append(ksel[:n_p].reshape(bp, t, *kvshape))
        outs['pw'].append(kwin[:n_p].reshape(bp, t, *kvshape)[:, t - min(WINDOW, t):])
        outs['pr'].append(hr_p[:bp].reshape(bp, SSM_GROUPS, SSM_STATE))
        outs['pi'].append(hi_p[:bp].reshape(bp, SSM_GROUPS, SSM_STATE))
        outs['pv'].append(vn[:n_p].reshape(bp, t, GMLP_WIDTH)[:, (t - 1) // CHUNK * CHUNK:])
        outs['sc'].append(cmp_rows[n_p:n_p + n_s].reshape(bs, s_new, *kvshape))
        outs['ss'].append(ksel[n_p:n_p + n_s].reshape(bs, s_new, *kvshape))
        win_new = kwin[n_p:n_p + n_s].reshape(bs, s_new, *kvshape)
        outs['sw'].append(jnp.concatenate([cache_kv_win[:, l], win_new], axis=1)[:, s_new:])
        outs['sr'].append(hr_s[:bs].reshape(bs, SSM_GROUPS, SSM_STATE))
        outs['si'].append(hi_s[:bs].reshape(bs, SSM_GROUPS, SSM_STATE))
        outs['sv'].append(vn[n_p:n_p + n_s].reshape(bs, s_new, GMLP_WIDTH))

    xf = _add3_call(x, y2)
    st = {k: jnp.stack(v, axis=1) for k, v in outs.items()}
    return (xf[:n_p].reshape(bp, t, D_MODEL), xf[n_p:n_p + n_s].reshape(bs, s_new, D_MODEL),
            st['pc'], st['ps'], st['pw'], st['pr'], st['pi'], st['pv'],
            st['sc'], st['ss'], st['sw'], st['sr'], st['si'], st['sv'])
```

```python
import functools
import math

import numpy as np
import jax
import jax.numpy as jnp
from jax import lax
from jax.experimental import pallas as pl
from jax.experimental.pallas import tpu as pltpu

f32 = jnp.float32
bf16 = jnp.bfloat16

D_MODEL = 2048
DEPTH = 4
PAGE_SIZE = 128
HEAD_DIM = 64
ATTN_WIDTH = 1024
N_HEADS_A = 16
N_KV_HEADS = 4
GQA = 4
KV_WIDTH = 256
CMP_BLOCK = 32
CMP_STRIDE = 16
SEL_BLOCK = 64
SEL_TOPK = 16
WINDOW = 512
NUM_BUCKETS = 32
MAX_DISTANCE = 128
SSM_WIDTH = 512
SSM_GROUP = 16
SSM_GROUPS = 32
SSM_STATE = 64
GMLP_WIDTH = 512
GMLP_GROUPS = 4
CHUNK = 128
MOE_GROUPS = 4
EXPERTS_PER_GROUP = 4
N_EXPERTS = 16
D_EXPERT = 512
SCALE = HEAD_DIM ** -0.5
LOG2E = math.log2(math.e)
EPS = 1e-6
NEG = -1e30

PROJ_W = 4224
LANES = 128
VMEM_LIMIT = 56 * 1024 * 1024


def _cparams(sem, **kw):
    return pltpu.CompilerParams(dimension_semantics=sem, vmem_limit_bytes=VMEM_LIMIT, **kw)


def _dot(a, b):
    return jnp.dot(a, b, preferred_element_type=f32)


def _dot_nt(a, b):
    return lax.dot_general(a, b, (((1,), (1,)), ((), ())), preferred_element_type=f32)


def _split3(x):
    hi = x.astype(bf16)
    r = x - hi.astype(f32)
    mid = r.astype(bf16)
    lo = (r - mid.astype(f32)).astype(bf16)
    return hi, mid, lo


def _gelu(x):
    return 0.5 * x * (1.0 + jnp.tanh(math.sqrt(2.0 / math.pi) * (x + 0.044715 * (x * x * x))))


def _sigmoid(x):
    return 1.0 / (1.0 + jnp.exp(-x))


def _group_mean_sq(x, gmat, group):
    x2 = x * x
    hi = x2.astype(bf16)
    lo = (x2 - hi.astype(f32)).astype(bf16)
    return (_dot(hi, gmat) + _dot(lo, gmat)) * (1.0 / group)


def _residual(refs, has_y):
    if has_y:
        return refs[0][...] + refs[1][...] + refs[2][...]
    return refs[0][...]


def _residual_specs(x, y2, tm, index_of):
    nrow = x.shape[0] // tm
    specs = [pl.BlockSpec((tm, D_MODEL), lambda *ids: (index_of(*ids), 0))]
    args = [x]
    if y2 is not None:
        specs += [pl.BlockSpec((tm, D_MODEL), lambda *ids: (index_of(*ids), 0)),
                  pl.BlockSpec((tm, D_MODEL), lambda *ids: (index_of(*ids) + nrow, 0))]
        args += [y2, y2]
    return specs, args


def _proj_kernel(*refs, has_y):
    nres = 3 if has_y else 1
    g_ref, w_ref, proj_ref, xn_sc = refs[nres:]

    @pl.when(pl.program_id(1) == 0)
    def _():
        x = _residual(refs, has_y)
        ms = jnp.mean(x * x, axis=-1, keepdims=True)
        xn_sc[...] = (x * lax.rsqrt(ms + EPS) * g_ref[...]).astype(bf16)

    proj_ref[...] = _dot(xn_sc[...], w_ref[...])


def _proj_call(x, y2, g, w, tm=320, tn=1408):
    nt = x.shape[0]
    res_specs, res_args = _residual_specs(x, y2, tm, lambda i, j: i)
    return pl.pallas_call(
        functools.partial(_proj_kernel, has_y=y2 is not None),
        grid=(nt // tm, PROJ_W // tn),
        in_specs=res_specs + [pl.BlockSpec((1, D_MODEL), lambda i, j: (0, 0)),
                              pl.BlockSpec((D_MODEL, tn), lambda i, j: (0, j))],
        out_specs=pl.BlockSpec((tm, tn), lambda i, j: (i, j)),
        out_shape=jax.ShapeDtypeStruct((nt, PROJ_W), f32),
        scratch_shapes=[pltpu.VMEM((tm, D_MODEL), bf16)],
        compiler_params=_cparams(("parallel", "arbitrary")),
        name="proj",
    )(*res_args, g, w)


def _post_kernel(q_ref, sel_ref, win_ref, ug_ref, vg_ref, gt_ref, gq_ref, gsel_ref, gwin_ref, gv_ref,
                 g64_ref, g128_ref, qn_ref, ksel_ref, kwin_ref, gate_ref, uga_ref, vn_ref):
    g64 = g64_ref[...]
    lane = lax.broadcasted_iota(jnp.int32, (1, 512), 1)
    is_k = (lane % 128) < 64
    for half in range(2):
        q = q_ref[:, half * 512:(half + 1) * 512]
        ms = _group_mean_sq(q, g64, 64)
        qn = q * lax.rsqrt(ms + EPS) * gq_ref[:, half * 512:(half + 1) * 512]
        qn_ref[:, half * 512:(half + 1) * 512] = (qn * (SCALE * LOG2E)).astype(bf16)
    for src, gref, dst in ((sel_ref, gsel_ref, ksel_ref), (win_ref, gwin_ref, kwin_ref)):
        x = src[...]
        ms = _group_mean_sq(x, g64, 64)
        dst[...] = jnp.where(is_k, x * lax.rsqrt(ms + EPS) * gref[...], x)
    gate_ref[...] = _sigmoid(gt_ref[...])
    uga_ref[...] = _gelu(ug_ref[...])
    v = _gelu(vg_ref[...])
    ms = _group_mean_sq(v, g128_ref[...], 128)
    vn_ref[...] = v * lax.rsqrt(ms + EPS) * gv_ref[...]


def _post_call(proj, gq, gsel, gwin, gv, g64, g128, tm=320):
    nt = proj.shape[0]

    def col(width, idx):
        return pl.BlockSpec((tm, width), lambda i: (i, idx))

    def full(shape):
        return pl.BlockSpec(shape, lambda i: (0,) * len(shape))

    def out(width):
        return pl.BlockSpec((tm, width), lambda i: (i, 0))

    return pl.pallas_call(
        _post_kernel,
        grid=(nt // tm,),
        in_specs=[col(1024, 0), col(512, 3), col(512, 4), col(512, 6), col(512, 7), col(128, 32),
                  full((1, 1024)), full((1, 512)), full((1, 512)), full((1, 512)),
                  full((512, 512)), full((512, 512))],
        out_specs=[out(1024), out(512), out(512), out(128), out(512), out(512)],
        out_shape=[jax.ShapeDtypeStruct((nt, 1024), bf16), jax.ShapeDtypeStruct((nt, 512), f32),
                   jax.ShapeDtypeStruct((nt, 512), f32), jax.ShapeDtypeStruct((nt, 128), f32),
                   jax.ShapeDtypeStruct((nt, 512), f32), jax.ShapeDtypeStruct((nt, 512), f32)],
        compiler_params=_cparams(("parallel",)),
        name="post",
    )(proj, proj, proj, proj, proj, proj, gq, gsel, gwin, gv, g64, g128)


CMP_PAGES_PER_STEP = 32


def _cmp_kernel(pt_ref, *refs, pps, nsteps, nchunk):
    srcs = refs[:pps]
    wab_ref, pos_ref, w2_ref, gk_ref, out_ref, c_sc = refs[pps:]
    s = pl.program_id(2)
    for pp in range(pps // 2):
        row0 = pl.multiple_of(s * (pps * 8) + pp * 16, 16)
        for j in range(CMP_STRIDE):
            a = srcs[2 * pp][pl.ds(j, 8, stride=CMP_STRIDE), :]
            b = srcs[2 * pp + 1][pl.ds(j, 8, stride=CMP_STRIDE), :]
            c_sc[pl.ds(row0, 16), j * 128:(j + 1) * 128] = jnp.concatenate([a, b], axis=0).astype(bf16)

    @pl.when(s == nsteps - 1)
    def _():
        wab = wab_ref[...]
        pq = _dot(c_sc[...], wab)
        pos = pos_ref[...].astype(bf16)
        pterm = _dot(pos[:, :2048], wab[:, :128]) + _dot(pos[:, 2048:], wab[:, 128:])
        pre = pq[:, :128] + pltpu.roll(pq[:, 128:], nchunk - 1, 0) + pterm[0:1, :]
        y = _dot(_gelu(pre).astype(bf16), w2_ref[...])
        lane = lax.broadcasted_iota(jnp.int32, (1, 128), 1)
        is_k = lane < 64
        ms = jnp.sum(jnp.where(is_k, y * y, 0.0), axis=-1, keepdims=True) * (1.0 / 64)
        out_ref[...] = jnp.where(is_k, y * lax.rsqrt(ms + EPS) * gk_ref[...], y)


def _cmp_call(pt_flat, src4, layer, col0, nb, npg, wab, pos, w2, gk):
    nchunk = npg * 8
    pps = min(CMP_PAGES_PER_STEP, npg)
    nsteps = npg // pps

    def src_spec(k):
        return pl.BlockSpec((None, None, PAGE_SIZE, 128),
                            lambda b, h, s, pt: (pt[b * npg + s * pps + k], layer, 0, col0 + h))

    def full(shape):
        return pl.BlockSpec(shape, lambda b, h, s, pt: (0,) * len(shape))

    return pl.pallas_call(
        functools.partial(_cmp_kernel, pps=pps, nsteps=nsteps, nchunk=nchunk),
        grid_spec=pltpu.PrefetchScalarGridSpec(
            num_scalar_prefetch=1,
            grid=(nb, N_KV_HEADS, nsteps),
            in_specs=[src_spec(k) for k in range(pps)]
                     + [full((2048, 256)), full((8, 4096)), full((128, 128)), full((1, 128))],
            out_specs=pl.BlockSpec((None, nchunk, 128), lambda b, h, s, pt: (b, 0, h)),
            scratch_shapes=[pltpu.VMEM((nchunk, 2048), bf16)],
        ),
        out_shape=jax.ShapeDtypeStruct((nb, nchunk, 512), f32),
        compiler_params=_cparams(("parallel", "parallel", "arbitrary")),
        name="compress",
    )(pt_flat, *([src4] * pps), wab, pos, w2, gk)


def _pattn_kernel(q_ref, kvc_ref, ksel_ref, kwin_ref, gate_ref, cbt_ref, tzt_ref, ovt_ref, ext_ref, out_ref,
                  kb_sel, vat_sel, kb_win, vat_win, kcb, vcat, sb_sc, gt_sc, m_sc, acc_sel, acc_win, *, tq, t):
    h = pl.program_id(1)
    i = pl.program_id(2)
    c4 = GQA * tq
    nsel = t // SEL_BLOCK
    ntile = t // tq
    nwin = WINDOW // tq
    ncc = t // CMP_STRIDE
    ksel = min(SEL_TOPK, nsel)

    @pl.when(i == 0)
    def _():
        lane = lax.broadcasted_iota(jnp.int32, (1, 128), 1)
        for src, kb, vat in ((ksel_ref, kb_sel, vat_sel), (kwin_ref, kb_win, vat_win)):
            blk = src[...]
            kb[...] = blk[:, :64].astype(bf16)
            va = jnp.where(lane < 64, pltpu.roll(blk, 64, 1), 1.0)
            for jj in range(ntile):
                vat[jj] = jnp.transpose(va[jj * tq:(jj + 1) * tq]).astype(bf16)
        kvc = kvc_ref[...]
        kcb[...] = kvc[:, :64].astype(bf16)
        vcat[...] = jnp.transpose(jnp.where(lane < 64, pltpu.roll(kvc, 64, 1), 0.0)).astype(bf16)

    qt = jnp.transpose(q_ref[...].astype(f32))
    q4t = jnp.concatenate([qt[g * 64:(g + 1) * 64] for g in range(GQA)], axis=1).astype(bf16)
    gt_sc[...] = jnp.transpose(gate_ref[...])

    def lanes4(a):
        return jnp.concatenate([a] * GQA, axis=1)

    nrow = lax.broadcasted_iota(jnp.int32, (ncc, tq), 0)
    qpos = i * tq + lax.broadcasted_iota(jnp.int32, (ncc, tq), 1)
    cvis = lanes4((CMP_STRIDE * nrow + CMP_BLOCK - 1 <= qpos) & (nrow < (t - CMP_BLOCK) // CMP_STRIDE + 1))
    shift = pl.multiple_of((tq // CMP_STRIDE) * (ntile - 1 - i), tq // CMP_STRIDE)
    cb = jnp.concatenate([cbt_ref[g, pl.ds(shift, ncc), :] for g in range(GQA)], axis=1)
    st = jnp.where(cvis, _dot(kcb[...], q4t) + cb, NEG)
    m = jnp.max(st, axis=0, keepdims=True)
    e = jnp.where(cvis, jnp.exp2(st - m), 0.0)
    p = e / jnp.maximum(jnp.sum(e, axis=0, keepdims=True), 1e-30)
    o_cmp = _dot(vcat[...], p.astype(bf16))
    pg = p[:, 0:tq] + p[:, tq:2 * tq] + p[:, 2 * tq:3 * tq] + p[:, 3 * tq:4 * tq]
    ovt = ovt_ref[...]
    hi, mid, lo = _split3(pg)
    score = (_dot(ovt, hi) + _dot(ovt, mid) + _dot(ovt, lo))[0:nsel]

    blk = lax.broadcasted_iota(jnp.int32, (nsel, tq), 0)
    qp = i * tq + lax.broadcasted_iota(jnp.int32, (nsel, tq), 1)
    cur = qp // SEL_BLOCK
    forced = (blk == 0) | (blk == cur) | (blk == cur - 1)
    score = jnp.where(forced, 1e6, jnp.where(blk * SEL_BLOCK > qp, -1.0, score))
    rank = jnp.zeros((nsel, tq), f32)
    for sp in range(nsel):
        other = score[sp:sp + 1, :]
        ahead = (other > score) | ((other == score) & (blk > sp))
        rank = rank + jnp.where(ahead, 1.0, 0.0)
    selneg = jnp.where(rank < ksel, 0.0, NEG)
    selneg = jnp.concatenate([selneg, jnp.zeros((128 - nsel, tq), f32)], axis=0).astype(bf16)
    for jj in range(ntile):
        sb_sc[jj] = _dot(ext_ref[jj * tq:(jj + 1) * tq, :], selneg)

    def flash(kb, vat, acc, tiles):
        sts = []
        for j, bias in tiles:
            k0 = pl.multiple_of(j * tq, tq)
            st = _dot(kb[pl.ds(k0, tq), :], q4t)
            sts.append(st if bias is None else st + bias)
        m_old = m_sc[...]
        m_new = m_old
        for st in sts:
            m_new = jnp.maximum(m_new, jnp.max(st, axis=0, keepdims=True))
        upd = acc[...] * jnp.exp2(m_old - m_new)
        for (j, _), st in zip(tiles, sts):
            upd = upd + _dot(vat[j], jnp.exp2(st - m_new).astype(bf16))
        acc[...] = upd
        m_sc[...] = m_new

    def sel_mask(j):
        return lanes4(sb_sc[j])

    m_sc[...] = jnp.full((1, c4), NEG, f32)
    acc_sel[...] = jnp.zeros((128, c4), f32)
    nfar = jnp.maximum(i - 1, 0)

    def sel_far(p, c):
        flash(kb_sel, vat_sel, acc_sel, [(2 * p, sel_mask(2 * p)), (2 * p + 1, sel_mask(2 * p + 1))])
        return c

    lax.fori_loop(0, nfar // 2, sel_far, 0)

    @pl.when(nfar % 2 == 1)
    def _():
        flash(kb_sel, vat_sel, acc_sel, [(nfar - 1, sel_mask(nfar - 1))])

    @pl.when(i >= 1)
    def _():
        flash(kb_sel, vat_sel, acc_sel, [(i - 1, tzt_ref[1] + sel_mask(i - 1)), (i, tzt_ref[0] + sel_mask(i))])

    @pl.when(i == 0)
    def _():
        flash(kb_sel, vat_sel, acc_sel, [(0, tzt_ref[0] + sel_mask(0))])

    m_sc[...] = jnp.full((1, c4), NEG, f32)
    acc_win[...] = jnp.zeros((128, c4), f32)

    def win_far(j, c):
        flash(kb_win, vat_win, acc_win, [(j, None)])
        return c

    lax.fori_loop(jnp.maximum(i - nwin + 1, 0), jnp.maximum(i - 1, 0), win_far, 0)

    @pl.when(i >= nwin)
    def _():
        krow = lax.broadcasted_iota(jnp.int32, (tq, tq), 0)
        qcol = lax.broadcasted_iota(jnp.int32, (tq, tq), 1)
        edge = lanes4(jnp.where(qcol <= krow, 0.0, NEG))
        flash(kb_win, vat_win, acc_win, [(i - nwin, edge), (i - 1, tzt_ref[1]), (i, tzt_ref[0])])

    @pl.when((i >= 1) & (i < nwin))
    def _():
        flash(kb_win, vat_win, acc_win, [(i - 1, tzt_ref[1]), (i, tzt_ref[0])])

    @pl.when(i == 0)
    def _():
        flash(kb_win, vat_win, acc_win, [(0, tzt_ref[0])])

    a_sel = acc_sel[...]
    a_win = acc_win[...]
    o_sel = a_sel[0:64] / a_sel[64:65]
    o_win = a_win[0:64] / a_win[64:65]

    def grow(branch):
        return jnp.concatenate([gt_sc[pl.ds(branch * N_HEADS_A + h * GQA + g, 1), :] for g in range(GQA)], axis=1)

    comb = grow(0) * o_cmp[0:64] + grow(1) * o_sel + grow(2) * o_win
    out_ref[...] = jnp.concatenate([jnp.transpose(comb[:, g * tq:(g + 1) * tq]) for g in range(GQA)], axis=1)


def _pattn_call(qn, kvc, ksel, kwin, gates, cbt, tzt, ovt, ext, nb, t, tq=256):
    ntile = t // tq
    c4 = GQA * tq
    ncc = t // CMP_STRIDE
    grid = (nb, N_KV_HEADS, ntile)
    return pl.pallas_call(
        functools.partial(_pattn_kernel, tq=tq, t=t),
        grid=grid,
        in_specs=[
            pl.BlockSpec((tq, 256), lambda b, h, i: (b * ntile + i, h)),
            pl.BlockSpec((None, ncc, 128), lambda b, h, i: (b, 0, h)),
            pl.BlockSpec((t, 128), lambda b, h, i: (b, h)),
            pl.BlockSpec((t, 128), lambda b, h, i: (b, h)),
            pl.BlockSpec((tq, 128), lambda b, h, i: (b * ntile + i, 0)),
            pl.BlockSpec((GQA, 2 * ncc, tq), lambda b, h, i: (h, 0, 0)),
            pl.BlockSpec((None, 2, tq, c4), lambda b, h, i: (h, 0, 0, 0)),
            pl.BlockSpec((128, ncc), lambda b, h, i: (0, 0)),
            pl.BlockSpec((t, 128), lambda b, h, i: (0, 0)),
        ],
        out_specs=pl.BlockSpec((tq, 256), lambda b, h, i: (b * ntile + i, h)),
        out_shape=jax.ShapeDtypeStruct((nb * t, ATTN_WIDTH), f32),
        scratch_shapes=[
            pltpu.VMEM((t, 64), bf16), pltpu.VMEM((ntile, 128, tq), bf16),
            pltpu.VMEM((t, 64), bf16), pltpu.VMEM((ntile, 128, tq), bf16),
            pltpu.VMEM((ncc, 64), bf16), pltpu.VMEM((128, ncc), bf16),
            pltpu.VMEM((ntile, tq, tq), f32), pltpu.VMEM((128, tq), f32),
            pltpu.VMEM((1, c4), f32), pltpu.VMEM((128, c4), f32), pltpu.VMEM((128, c4), f32),
        ],
        compiler_params=_cparams(("parallel", "parallel", "arbitrary")),
        name="prompt_attn",
    )(qn, kvc, ksel, kwin, gates, cbt, tzt, ovt, ext)


CH_PAGES = 16


def _topk_neg_rows(score, rowf, k):
    neg = jnp.full(score.shape, NEG, f32)
    work = score
    for _ in range(k):
        m = jnp.max(work, axis=0, keepdims=True)
        idx = jnp.min(jnp.where(work == m, rowf, 1e9), axis=0, keepdims=True)
        hit = rowf == idx
        neg = jnp.where(hit, 0.0, neg)
        work = jnp.where(hit, -3e38, work)
    return neg


def _col_of(vec):
    return jnp.transpose(jnp.broadcast_to(vec, (128, 128)))[:, :1]


def _sattn_kernel(pt_ref, qbig_ref, kvc_ref, cbt_ref, ovt_ref, smat_ref, e2_ref, pool_ref, sblast_ref,
                  nsel_ref, nbsel_ref, wcache_ref, wbt_ref, nwin_ref, nbwin_ref, gt_ref, out_ref,
                  buf, sem, seln_sc, m_sc, l_sc, acc_sc, *, layer, npg, n_new):
    b = pl.program_id(0)
    nch = npg // CH_PAGES
    ck = CH_PAGES * PAGE_SIZE
    past = npg * PAGE_SIZE
    qbig = qbig_ref[...]

    def page_copy(c, k, slot):
        pg = pt_ref[b * npg + c * CH_PAGES + k]
        return pltpu.make_async_copy(pool_ref.at[pg, layer], buf.at[slot, k], sem.at[slot])

    def start_chunk(c, slot):
        for k in range(CH_PAGES):
            page_copy(c, k, slot).start()

    def wait_chunk(c, slot):
        for k in range(CH_PAGES):
            page_copy(c, k, slot).wait()

    start_chunk(0, 0)
    if nch > 1:
        start_chunk(1, 1)

    def reset():
        m_sc[...] = jnp.full((1, 128), NEG, f32)
        l_sc[...] = jnp.zeros((1, 128), f32)
        acc_sc[...] = jnp.zeros((128, 512), f32)

    def attend(rows, bias_t):
        rb = rows.astype(bf16)
        st = _dot(rb, qbig) + bias_t
        m_old = m_sc[...]
        m_new = jnp.maximum(m_old, jnp.max(st, axis=0, keepdims=True))
        alpha = jnp.exp2(m_old - m_new)
        pt = jnp.exp2(st - m_new)
        l_sc[...] = l_sc[...] * alpha + jnp.sum(pt, axis=0, keepdims=True)
        acc_sc[...] = acc_sc[...] * _col_of(alpha) + _dot(jnp.transpose(pt).astype(bf16), rb)
        m_sc[...] = m_new

    rowhead = lax.broadcasted_iota(jnp.int32, (128, 128), 0) // 32

    def own_head(res):
        out = jnp.zeros((128, 128), f32)
        for hh in range(N_KV_HEADS):
            out = jnp.where(rowhead == hh, res[:, hh * 128:(hh + 1) * 128], out)
        return out

    kvc = kvc_ref[...]
    kb = kvc.astype(bf16)
    st = _dot(kb, qbig) + cbt_ref[...]
    m = jnp.max(st, axis=0, keepdims=True)
    e = jnp.exp2(st - m)
    pt = e / jnp.maximum(jnp.sum(e, axis=0, keepdims=True), 1e-30)
    o_cmp = own_head(_dot(jnp.transpose(pt).astype(bf16), kb))
    smat = smat_ref[...]
    hi, mid, lo = _split3(pt)
    pg = _dot(hi, smat) + _dot(mid, smat) + _dot(lo, smat)
    hi, mid, lo = _split3(pg)
    ovt = ovt_ref[...]
    score = _dot(ovt, hi) + _dot(ovt, mid) + _dot(ovt, lo)
    nrow = score.shape[0]
    blk = lax.broadcasted_iota(jnp.int32, (nrow, 128), 0)
    qpos = past + lax.broadcasted_iota(jnp.int32, (nrow, 128), 1) % 8
    cur = qpos // SEL_BLOCK
    forced = (blk == 0) | (blk == cur) | (blk == cur - 1)
    future = blk * SEL_BLOCK > qpos
    score = jnp.where(forced, 1e6, jnp.where(future, -1.0, score))
    nsel = -(-(past + n_new) // SEL_BLOCK)
    score = jnp.where(blk < nsel, score, -2.0)
    seln_sc[0:nrow, :] = _topk_neg_rows(score, blk.astype(f32), min(SEL_TOPK, nsel))
    seln_sc[nrow:, :] = jnp.zeros((seln_sc.shape[0] - nrow, 128), f32)

    reset()
    bpc = ck // SEL_BLOCK

    def sel_bias(c):
        r0 = pl.multiple_of(c * bpc, bpc)
        return _dot(e2_ref[...], seln_sc[pl.ds(r0, 128), :].astype(bf16))

    def chunk_body(c, carry):
        slot = c % 2
        wait_chunk(c, slot)
        attend(buf[slot].reshape(ck, 512), sel_bias(c))

        @pl.when(c + 2 < nch)
        def _():
            start_chunk(c + 2, slot)

        return carry

    lax.fori_loop(0, nch - 1, chunk_body, 0)
    last = nch - 1
    wait_chunk(last, last % 2)
    attend(buf[last % 2].reshape(ck, 512), sel_bias(last) + sblast_ref[...])
    attend(nsel_ref[...], nbsel_ref[...] + seln_sc[past // SEL_BLOCK:past // SEL_BLOCK + 1, :])
    o_sel = own_head(acc_sc[...]) / _col_of(l_sc[...])

    reset()
    attend(wcache_ref[...], wbt_ref[...])
    attend(nwin_ref[...], nbwin_ref[...])
    o_win = own_head(acc_sc[...]) / _col_of(l_sc[...])

    gts = gt_ref[...]
    out_ref[...] = gts[:, 0:1] * o_cmp + gts[:, 1:2] * o_sel + gts[:, 2:3] * o_win


def _sattn_call(pt_flat, qbig, kvc, cbt, ovt, smat, e2, pool, sblast, nsel, nbsel, wcache, wbt, nwin, nbwin,
                gt, layer, nb, npg, n_new):
    nrow = ovt.shape[0]
    ck = CH_PAGES * PAGE_SIZE

    def full(shape):
        return pl.BlockSpec(shape, lambda b, pt: (0,) * len(shape))

    def perb(shape):
        return pl.BlockSpec((None,) + shape, lambda b, pt: (b,) + (0,) * len(shape))

    return pl.pallas_call(
        functools.partial(_sattn_kernel, layer=layer, npg=npg, n_new=n_new),
        grid_spec=pltpu.PrefetchScalarGridSpec(
            num_scalar_prefetch=1,
            grid=(nb,),
            in_specs=[perb((512, 128)), perb((npg * 8, 512)), full(cbt.shape), full(ovt.shape), full((128, 128)),
                      full((ck, 128)), pl.BlockSpec(memory_space=pl.ANY), full((ck, 128)),
                      perb((128, 512)), full((128, 128)),
                      pl.BlockSpec((None, None, WINDOW, 512), lambda b, pt: (b, layer, 0, 0)), full((WINDOW, 128)),
                      perb((128, 512)), full((128, 128)), perb((128, 128))],
            out_specs=perb((128, 128)),
            scratch_shapes=[pltpu.VMEM((2, CH_PAGES, PAGE_SIZE, 512), f32), pltpu.SemaphoreType.DMA((2,)),
                            pltpu.VMEM((nrow + 128, 128), f32), pltpu.VMEM((1, 128), f32), pltpu.VMEM((1, 128), f32),
                            pltpu.VMEM((128, 512), f32)],
        ),
        out_shape=jax.ShapeDtypeStruct((nb, 128, 128), f32),
        compiler_params=_cparams(("arbitrary",)),
        name="sample_attn",
    )(pt_flat, qbig, kvc, cbt, ovt, smat, e2, pool, sblast, nsel, nbsel, wcache, wbt, nwin, nbwin, gt)


def _s5_kernel(*refs, nb, tc):
    u_refs = refs[:nb]
    (h0r_ref, h0i_ref, abr_ref, abi_ref, bre_ref, bim_ref, cre_ref, cim_ref, d_ref, gw_ref, gb_ref,
     o_ref, hr_out, hi_out, xr, xi, hr, hi) = refs[nb:]
    c = pl.program_id(0)

    @pl.when(c == 0)
    def _():
        hr[...] = h0r_ref[...]
        hi[...] = h0i_ref[...]

    for b in range(nb):
        u = u_refs[b][...].astype(bf16)
        for cb in range(4):
            ub = u[:, cb * 128:(cb + 1) * 128]
            pr = _dot(ub, bre_ref[cb])
            pi = _dot(ub, bim_ref[cb])
            for k in range(4):
                xr[cb * 4 + k, b * tc:(b + 1) * tc, :] = pr[:, k * 128:(k + 1) * 128]
                xi[cb * 4 + k, b * tc:(b + 1) * tc, :] = pi[:, k * 128:(k + 1) * 128]

    for lc in range(4):
        slabs = [lc * 4 + k for k in range(4)]
        ar = [abr_ref[0:nb, j * 128:(j + 1) * 128] for j in slabs]
        ai = [abi_ref[0:nb, j * 128:(j + 1) * 128] for j in slabs]

        def body(t, carry, slabs=slabs, ar=ar, ai=ai):
            rows = pl.ds(t, nb, stride=tc)
            new = []
            for k, j in enumerate(slabs):
                cr, ci = carry[2 * k], carry[2 * k + 1]
                nr = ar[k] * cr - ai[k] * ci + xr[j, rows, :]
                ni = ar[k] * ci + ai[k] * cr + xi[j, rows, :]
                xr[j, rows, :] = nr
                xi[j, rows, :] = ni
                new += [nr, ni]
            return tuple(new)

        init = []
        for j in slabs:
            init += [hr[0:nb, j * 128:(j + 1) * 128], hi[0:nb, j * 128:(j + 1) * 128]]
        fin = lax.fori_loop(0, tc, body, tuple(init), unroll=4)
        for k, j in enumerate(slabs):
            hr[0:nb, j * 128:(j + 1) * 128] = fin[2 * k]
            hi[0:nb, j * 128:(j + 1) * 128] = fin[2 * k + 1]

    ys = []
    for cb in range(4):
        hre = jnp.concatenate([xr[cb * 4 + k] for k in range(4)], axis=1).astype(bf16)
        him = jnp.concatenate([xi[cb * 4 + k] for k in range(4)], axis=1).astype(bf16)
        ys.append(_dot(hre, cre_ref[cb]) - _dot(him, cim_ref[cb]))
    u_all = jnp.concatenate([u_refs[b][...] for b in range(nb)], axis=0)
    y = _gelu(jnp.concatenate(ys, axis=1) + d_ref[...] * u_all)
    o = y * _sigmoid(_dot(y.astype(bf16), gw_ref[...]) + gb_ref[...])
    for b in range(nb):
        o_ref[b] = o[b * tc:(b + 1) * tc]

    @pl.when(c == pl.num_programs(0) - 1)
    def _():
        hr_out[...] = hr[...]
        hi_out[...] = hi[...]


def _s5_call(proj, row0, nb, t, tc, h0r, h0i, abr, abi, bre, bim, cre, cim, d, gw, gb):
    def full(shape):
        return pl.BlockSpec(shape, lambda c: (0,) * len(shape))

    u_specs = [pl.BlockSpec((tc, 512), lambda c, b=b: ((row0 + b * t) // tc + c, 5)) for b in range(nb)]
    return pl.pallas_call(
        functools.partial(_s5_kernel, nb=nb, tc=tc),
        grid=(t // tc,),
        in_specs=u_specs + [full((8, 2048)), full((8, 2048)), full((8, 2048)), full((8, 2048)),
                            full((4, 128, 512)), full((4, 128, 512)), full((4, 512, 128)), full((4, 512, 128)),
                            full((1, 512)), full((512, 512)), full((1, 512))],
        out_specs=[pl.BlockSpec((nb, tc, 512), lambda c: (0, c, 0)), full((8, 2048)), full((8, 2048))],
        out_shape=[jax.ShapeDtypeStruct((nb, t, 512), f32), jax.ShapeDtypeStruct((8, 2048), f32),
                   jax.ShapeDtypeStruct((8, 2048), f32)],
        scratch_shapes=[pltpu.VMEM((16, nb * tc, 128), f32), pltpu.VMEM((16, nb * tc, 128), f32),
                        pltpu.VMEM((8, 2048), f32), pltpu.VMEM((8, 2048), f32)],
        compiler_params=_cparams(("arbitrary",)),
        name="s5",
    )(*([proj] * nb), h0r, h0i, abr, abi, bre, bim, cre, cim, d, gw, gb)


def _gmlp_kernel(u_ref, v_ref, w_ref, bs_ref, o_ref, *, rows):
    u = u_ref[...]
    v = v_ref[...]
    outs = []
    for g in range(GMLP_GROUPS):
        vg = v[:, g * 128:(g + 1) * 128]
        if rows < CHUNK:
            vg = jnp.concatenate([vg, jnp.zeros((CHUNK - rows, 128), f32)], axis=0)
        mixed = _dot(w_ref[g][0:rows, :], vg.astype(bf16)) + bs_ref[0:rows, g:g + 1]
        outs.append(u[:, g * 128:(g + 1) * 128] * mixed)
    o_ref[...] = jnp.concatenate(outs, axis=1)


def _gmlp_call(uga, vn, w, bs, row0, nrows, rows):
    blk0 = row0 // rows

    def full(shape):
        return pl.BlockSpec(shape, lambda i: (0,) * len(shape))

    return pl.pallas_call(
        functools.partial(_gmlp_kernel, rows=rows),
        grid=(nrows // rows,),
        in_specs=[pl.BlockSpec((rows, 512), lambda i: (blk0 + i, 0)), pl.BlockSpec((rows, 512), lambda i: (blk0 + i, 0)),
                  full((GMLP_GROUPS, CHUNK, CHUNK)), full((CHUNK, 128))],
        out_specs=pl.BlockSpec((rows, 512), lambda i: (i, 0)),
        out_shape=jax.ShapeDtypeStruct((nrows, 512), f32),
        compiler_params=_cparams(("parallel",)),
        name="gmlp",
    )(uga, vn, w, bs)


def _merge_kernel(*refs, has_y):
    nres = 3 if has_y else 1
    oa_ref, ob_ref, oc_ref, gn_ref, w_ref, gf_ref, rwh_ref, rwl_ref, rb_ref, x1_ref, xn_ref, rt_ref = refs[nres:]

    def rms(v, g):
        return (v * lax.rsqrt(jnp.mean(v * v, axis=-1, keepdims=True) + EPS) * g).astype(bf16)

    acc = _dot(rms(oa_ref[...], gn_ref[:, 0:1024]), w_ref[0:1024, :])
    acc += _dot(rms(ob_ref[...], gn_ref[:, 1024:1536]), w_ref[1024:1536, :])
    acc += _dot(rms(oc_ref[...], gn_ref[:, 1536:2048]), w_ref[1536:2048, :])
    x1 = _residual(refs, has_y) + acc
    x1_ref[...] = x1
    xn = x1 * lax.rsqrt(jnp.mean(x1 * x1, axis=-1, keepdims=True) + EPS) * gf_ref[...]
    xn_ref[...] = xn
    hi = xn.astype(bf16)
    lo = (xn - hi.astype(f32)).astype(bf16)
    logits = _dot(hi, rwh_ref[...]) + _dot(lo, rwh_ref[...]) + _dot(hi, rwl_ref[...]) + rb_ref[...]
    lane = lax.broadcasted_iota(jnp.int32, logits.shape, 1)
    lanef = lane.astype(f32)
    is_g = lane < MOE_GROUPS
    gl = jnp.where(is_g, logits, NEG)
    gm = jnp.max(gl, axis=-1, keepdims=True)
    gidx = jnp.min(jnp.where(gl == gm, lanef, 1e9), axis=-1, keepdims=True)
    gprob = 1.0 / jnp.sum(jnp.where(is_g, jnp.exp(logits - gm), 0.0), axis=-1, keepdims=True)
    lo_lane = MOE_GROUPS + EXPERTS_PER_GROUP * gidx
    inl = jnp.where((lanef >= lo_lane) & (lanef < lo_lane + EXPERTS_PER_GROUP), logits, NEG)
    v1 = jnp.max(inl, axis=-1, keepdims=True)
    i1 = jnp.min(jnp.where(inl == v1, lanef, 1e9), axis=-1, keepdims=True)
    inl2 = jnp.where(lanef == i1, NEG, inl)
    v2 = jnp.max(inl2, axis=-1, keepdims=True)
    i2 = jnp.min(jnp.where(inl2 == v2, lanef, 1e9), axis=-1, keepdims=True)
    e2 = jnp.exp(v2 - v1)
    w1 = gprob / (1.0 + e2)
    w2 = gprob * e2 / (1.0 + e2)
    rt_ref[...] = jnp.where(lane == 0, i1 - MOE_GROUPS,
                            jnp.where(lane == 1, i2 - MOE_GROUPS,
                                      jnp.where(lane == 2, w1, jnp.where(lane == 3, w2, 0.0))))


def _merge_call(x, y2, oa, ob, oc, gn, w, gf, rwh, rwl, rb, tm=320):
    nt = x.shape[0]
    res_specs, res_args = _residual_specs(x, y2, tm, lambda i: i)

    def row(width):
        return pl.BlockSpec((tm, width), lambda i: (i, 0))

    def full(shape):
        return pl.BlockSpec(shape, lambda i: (0,) * len(shape))

    return pl.pallas_call(
        functools.partial(_merge_kernel, has_y=y2 is not None),
        grid=(nt // tm,),
        in_specs=res_specs + [row(1024), row(512), row(512), full((1, 2048)), full((2048, 2048)), full((1, 2048)),
                              full((2048, 128)), full((2048, 128)), full((1, 128))],
        out_specs=[row(2048), row(2048), row(128)],
        out_shape=[jax.ShapeDtypeStruct((nt, 2048), f32), jax.ShapeDtypeStruct((nt, 2048), f32),
                   jax.ShapeDtypeStruct((nt, 128), f32)],
        compiler_params=_cparams(("parallel",)),
        name="merge_router",
    )(*res_args, oa, ob, oc, gn, w, gf, rwh, rwl, rb)


def _expert_kernel(te_ref, nu_ref, src_ref, dst_ref, xn_hbm, wg_ref, w1_ref, w3_ref, w2_ref, y_hbm,
                   xbuf, obuf, w1b, w3b, w2b, gsem, ssem, *, tm, ntiles):
    t = pl.program_id(0)
    n_used = nu_ref[0]
    slot = t % 2

    def gather(tt, sl):
        def body(r, c):
            tok = src_ref[tt * tm + r]
            pltpu.make_async_copy(xn_hbm.at[pl.ds(tok, 1)], xbuf.at[sl, pl.ds(r, 1)], gsem.at[sl]).start()
            return c
        lax.fori_loop(0, tm, body, 0, unroll=8)

    def gather_wait(sl):
        pltpu.make_async_copy(xn_hbm.at[pl.ds(0, tm)], xbuf.at[sl], gsem.at[sl]).wait()

    def scatter(tt, sl):
        def body(r, c):
            row = dst_ref[tt * tm + r]
            pltpu.make_async_copy(obuf.at[sl, pl.ds(r, 1)], y_hbm.at[pl.ds(row, 1)], ssem.at[sl]).start()
            return c
        lax.fori_loop(0, tm, body, 0, unroll=8)

    def scatter_wait(sl):
        pltpu.make_async_copy(obuf.at[sl], y_hbm.at[pl.ds(0, tm)], ssem.at[sl]).wait()

    @pl.when(t == 0)
    def _():
        gather(0, 0)
        npair = y_hbm.shape[0] - 2 * tm
        for sl in range(2):
            obuf[sl] = jnp.zeros((tm, D_MODEL), f32)
            cp = pltpu.make_async_copy(obuf.at[sl], y_hbm.at[pl.ds(npair + sl * tm, tm)], ssem.at[sl])
            cp.start()
            cp.wait()

    @pl.when(t + 1 < n_used)
    def _():
        gather(t + 1, 1 - slot)

    @pl.when(t < n_used)
    def _():
        gather_wait(slot)

        @pl.when(t >= 2)
        def _():
            scatter_wait(slot)

        @pl.when((t == 0) | (te_ref[t] != te_ref[jnp.maximum(t - 1, 0)]))
        def _():
            w1b[...] = w1_ref[...].astype(bf16)
            w3b[...] = w3_ref[...].astype(bf16)
            w2b[...] = w2_ref[...].astype(bf16)

        x = xbuf[slot].astype(bf16)
        a = _dot(x, w1b[...])
        hmid = a * _sigmoid(a) * _dot(x, w3b[...]) * wg_ref[...]
        obuf[slot] = _dot(hmid.astype(bf16), w2b[...])
        scatter(t, slot)

    @pl.when(t == ntiles - 1)
    def _():
        scatter_wait((n_used - 1) % 2)

        @pl.when(n_used >= 2)
        def _():
            scatter_wait(n_used % 2)


def _expert_call(tile_e, n_used, src_tok, dst_row, xn, wgt, w1, w3, w2, tm, ntiles, nrows_out):
    def wmap(t, te, nu, src, dst):
        return (te[t], 0, 0)

    return pl.pallas_call(
        functools.partial(_expert_kernel, tm=tm, ntiles=ntiles),
        grid_spec=pltpu.PrefetchScalarGridSpec(
            num_scalar_prefetch=4,
            grid=(ntiles,),
            in_specs=[pl.BlockSpec(memory_space=pl.ANY),
                      pl.BlockSpec((tm, 1), lambda t, te, nu, src, dst: (t, 0)),
                      pl.BlockSpec((None, D_MODEL, D_EXPERT), wmap),
                      pl.BlockSpec((None, D_MODEL, D_EXPERT), wmap),
                      pl.BlockSpec((None, D_EXPERT, D_MODEL), wmap)],
            out_specs=pl.BlockSpec(memory_space=pl.ANY),
            scratch_shapes=[pltpu.VMEM((2, tm, D_MODEL), f32), pltpu.VMEM((2, tm, D_MODEL), f32),
                            pltpu.VMEM((D_MODEL, D_EXPERT), bf16), pltpu.VMEM((D_MODEL, D_EXPERT), bf16),
                            pltpu.VMEM((D_EXPERT, D_MODEL), bf16),
                            pltpu.SemaphoreType.DMA((2,)), pltpu.SemaphoreType.DMA((2,))],
        ),
        out_shape=jax.ShapeDtypeStruct((nrows_out, D_MODEL), f32),
        compiler_params=_cparams(("arbitrary",), disable_bounds_checks=True),
        name="experts",
    )(tile_e, n_used, src_tok, dst_row, xn, wgt, w1, w3, w2)


def _add3_kernel(x_ref, ya_ref, yb_ref, o_ref):
    o_ref[...] = x_ref[...] + ya_ref[...] + yb_ref[...]


def _add3_call(x, y2, tm=640):
    nt = x.shape[0]
    nrow = nt // tm
    return pl.pallas_call(
        _add3_kernel,
        grid=(nrow,),
        in_specs=[pl.BlockSpec((tm, D_MODEL), lambda i: (i, 0)), pl.BlockSpec((tm, D_MODEL), lambda i: (i, 0)),
                  pl.BlockSpec((tm, D_MODEL), lambda i: (i + nrow, 0))],
        out_specs=pl.BlockSpec((tm, D_MODEL), lambda i: (i, 0)),
        out_shape=jax.ShapeDtypeStruct((nt, D_MODEL), f32),
        compiler_params=_cparams(("parallel",)),
        name="residual_add",
    )(x, y2, y2)


EXPERT_TM = 256
PROMPT_TQ = 256


def _bucket(n):
    max_exact = NUM_BUCKETS // 2
    nf = jnp.maximum(n, max_exact).astype(f32)
    large = max_exact + (jnp.log(nf / max_exact) / math.log(MAX_DISTANCE / max_exact)
                         * (NUM_BUCKETS - max_exact)).astype(jnp.int32)
    return jnp.where(n < max_exact, n, jnp.minimum(large, NUM_BUCKETS - 1))


def _bias_tables(rel_bias, tq, t, past, n_new):
    bd = (rel_bias[_bucket(jnp.arange(128))] - rel_bias[NUM_BUCKETS - 1][None, :]) * LOG2E

    def look(dist):
        oh = jax.nn.one_hot(jnp.clip(dist, 0, 127), 128, dtype=f32)
        return jnp.einsum('...d,dh->...h', oh, bd, precision=lax.Precision.HIGHEST)

    ntile = t // tq
    r = jnp.arange(tq)
    d0 = r[:, None] - r[None, :]
    t0 = jnp.where((d0 >= 0)[..., None], look(d0), NEG)
    t1 = look(tq + d0)
    tz = jnp.stack([t0, t1], axis=0).reshape(2, tq, tq, N_KV_HEADS, GQA)
    tz = jnp.transpose(tz, (3, 0, 2, 4, 1)).reshape(N_KV_HEADS, 2, tq, GQA * tq)
    n = jnp.arange(2 * (t // CMP_STRIDE))
    cbl = jnp.transpose(look(tq * (ntile - 1) + r[None, :] - CMP_STRIDE * n[:, None] - (CMP_BLOCK - 1)), (2, 0, 1))

    c = jnp.arange(128)
    head_oh = jax.nn.one_hot(4 * (c // 32) + (c % 32) // 8, N_HEADS_A, dtype=f32)
    qq = c % 8

    def look_t(dist, vis):
        tab = jnp.einsum('mch,ch->mc', look(dist), head_oh, precision=lax.Precision.HIGHEST)
        return jnp.where(vis, tab, NEG)

    n_cmp_s = (past + n_new - CMP_BLOCK) // CMP_STRIDE + 1
    nchunk = past // CMP_STRIDE
    nn = jnp.arange(nchunk)[:, None]
    dist = past + qq[None, :] - (CMP_STRIDE * nn + CMP_BLOCK - 1)
    cbt = look_t(dist, (dist >= 0) & (nn < n_cmp_s))
    ck = CH_PAGES * PAGE_SIZE
    rr = jnp.arange(ck)[:, None]
    dist = qq[None, :] + ck - rr
    sblast = look_t(dist, dist >= 0)
    r128 = jnp.arange(128)[:, None]
    dist = qq[None, :] - r128
    nbnew = look_t(dist, (dist >= 0) & (r128 < n_new))
    rw = jnp.arange(WINDOW)[:, None]
    dist = WINDOW + qq[None, :] - rw
    wbt = look_t(dist, (dist >= 0) & (dist <= WINDOW))
    return tz, cbl, cbt, sblast, nbnew, wbt


def _static_mats(t, past, n_new):
    n = np.arange(128)
    n_cmp = (t - CMP_BLOCK) // CMP_STRIDE + 1
    s = np.arange(128)
    ov = ((CMP_STRIDE * n[:, None] < SEL_BLOCK * s[None, :] + SEL_BLOCK)
          & (CMP_STRIDE * n[:, None] + CMP_BLOCK > SEL_BLOCK * s[None, :])
          & (n[:, None] < n_cmp) & (s[None, :] < t // SEL_BLOCK))
    ex = (np.arange(t)[None, :] // SEL_BLOCK == s[:, None])
    nsel_s = -(-(past + n_new) // SEL_BLOCK)
    nrow = -(-nsel_s // 8) * 8
    n_cmp_s = (past + n_new - CMP_BLOCK) // CMP_STRIDE + 1
    ss = np.arange(nrow)[:, None]
    ns = np.arange(past // CMP_STRIDE)[None, :]
    ovt = ((CMP_STRIDE * ns < SEL_BLOCK * ss + SEL_BLOCK) & (CMP_STRIDE * ns + CMP_BLOCK > SEL_BLOCK * ss)
           & (ns < n_cmp_s) & (ss < nsel_s))
    c = np.arange(128)
    smat = (c[:, None] // 32 == c[None, :] // 32) & (c[:, None] % 8 == c[None, :] % 8)
    ck = CH_PAGES * PAGE_SIZE
    e2 = (np.arange(ck)[:, None] // SEL_BLOCK == np.arange(128)[None, :])
    cvt = lambda a: jnp.asarray(a.astype(np.float32), dtype=bf16)
    return cvt(ov.T), cvt(ex.T), cvt(ovt), cvt(smat), cvt(e2)


def _route_metadata(route, nt, tm, ntiles):
    e_flat = jnp.concatenate([route[:, 0], route[:, 1]]).astype(jnp.int32)
    w_flat = jnp.concatenate([route[:, 2], route[:, 3]])
    npair = 2 * nt
    order = jnp.argsort(e_flat, stable=True).astype(jnp.int32)
    counts = jnp.sum(jax.nn.one_hot(e_flat, N_EXPERTS, dtype=jnp.int32), axis=0)
    tiles_e = (counts + tm - 1) // tm
    tend = jnp.cumsum(tiles_e)
    tstart = tend - tiles_e
    cstart = jnp.cumsum(counts) - counts
    n_used = tend[-1]
    tidx = jnp.arange(ntiles, dtype=jnp.int32)
    tile_e = jnp.sum((jnp.minimum(tidx, n_used - 1)[:, None] >= tend[None, :]).astype(jnp.int32), axis=1)
    tile_oh = jax.nn.one_hot(tile_e, N_EXPERTS, dtype=jnp.int32)
    t_cnt = jnp.sum(tile_oh * counts[None, :], axis=1)
    t_first = jnp.sum(tile_oh * (cstart - tstart * tm)[None, :], axis=1) + tidx * tm
    rows = jnp.arange(tm, dtype=jnp.int32)[None, :]
    rank = tidx[:, None] * tm + rows - jnp.sum(tile_oh * tstart[None, :], axis=1)[:, None] * tm
    valid = (rank < t_cnt[:, None]) & (tidx[:, None] < n_used)
    pair = order[jnp.clip(t_first[:, None] + rows, 0, npair - 1)]
    dump = npair + (tidx[:, None] % 2) * tm + rows
    src_tok = jnp.where(valid, pair % nt, 0).reshape(-1)
    dst_row = jnp.where(valid, pair, dump).reshape(-1)
    wgt = jnp.where(valid, w_flat[pair], 0.0).reshape(-1, 1)
    return tile_e.astype(jnp.int32), n_used.reshape(1).astype(jnp.int32), src_tok, dst_row, wgt


def _block_diag(blocks):
    n, r, c = blocks.shape
    return jnp.einsum('grc,gk->grkc', blocks, jnp.eye(n, dtype=blocks.dtype)).reshape(n * r, n * c)


def kernel(x_prompt, x_sample, cache_kv_cmp, cache_kv_sel, cache_kv_win, state_ssm_re, state_ssm_im, page_table, rel_bias, norm_mix, w_in, qk_norm, cmp_pos, cmp_w1, cmp_w2, ssm_a_re, ssm_a_im, ssm_log_dt, ssm_b_re, ssm_b_im, ssm_c_re, ssm_c_im, ssm_d, ssm_glu_w, ssm_glu_b, gmlp_norm, gmlp_ws, gmlp_bs, out_norm, w_out, norm_ffn, router_group_w, router_group_b, router_expert_w, router_expert_b, expert_w1, expert_w3, expert_w2):
    bp, t, _ = x_prompt.shape
    bs, s_new, _ = x_sample.shape
    npg = page_table.shape[1]
    past = npg * PAGE_SIZE
    nphys = cache_kv_cmp.shape[0]
    n_p, n_s = bp * t, bs * s_new
    nt = -(-(n_p + n_s) // 640) * 640
    tq = PROMPT_TQ
    assert t % tq == 0 and WINDOW % tq == 0 and s_new == 8 and bs == 8 and npg % CH_PAGES == 0

    x = jnp.concatenate([x_prompt.reshape(n_p, D_MODEL), x_sample.reshape(n_s, D_MODEL),
                         jnp.zeros((nt - n_p - n_s, D_MODEL), f32)], axis=0)
    y2 = None
    pool_cmp = cache_kv_cmp.reshape(nphys, DEPTH, PAGE_SIZE, 512)
    pool_sel = cache_kv_sel.reshape(nphys, DEPTH, PAGE_SIZE, 512)
    wcache = cache_kv_win.reshape(bs, DEPTH, WINDOW, 512)
    pt_flat = page_table.reshape(-1).astype(jnp.int32)
    pt_ident = jnp.arange(bp * (t // PAGE_SIZE), dtype=jnp.int32)

    tz, cbl, cbt, sblast, nbnew, wbt = _bias_tables(rel_bias, tq, t, past, s_new)
    ov, ex, ovt, smat, e2 = _static_mats(t, past, s_new)
    g64 = _block_diag(jnp.ones((8, 64, 64), bf16))
    g128 = _block_diag(jnp.ones((4, 128, 128), bf16))
    ones64 = jnp.ones((HEAD_DIM,), f32)
    tril = jnp.tril(jnp.ones((CHUNK, CHUNK), f32))
    zeros_state = jnp.zeros((8, SSM_GROUPS * SSM_STATE), f32)
    ntiles = 2 * nt // EXPERT_TM + N_EXPERTS
    pad_rows = lambda a: jnp.concatenate([a, jnp.zeros((nt - a.shape[0],) + a.shape[1:], a.dtype)], axis=0)

    outs = {k: [] for k in ('pc', 'ps', 'pw', 'pr', 'pi', 'pv', 'sc', 'ss', 'sw', 'sr', 'si', 'sv')}
    for l in range(DEPTH):
        wl = w_in[l]
        o1, o2, o3 = ATTN_WIDTH, ATTN_WIDTH + 6 * KV_WIDTH, ATTN_WIDTH + 6 * KV_WIDTH + 3 * N_HEADS_A
        w_pad = jnp.concatenate([wl[:, :o2], wl[:, o3:], wl[:, o2:o3],
                                 jnp.zeros((D_MODEL, 128 - 3 * N_HEADS_A), f32)], axis=1).astype(bf16)
        gq = jnp.tile(qk_norm[l, 0], N_HEADS_A).reshape(1, 1024)
        gsel = jnp.tile(jnp.concatenate([qk_norm[l, 2], ones64]), N_KV_HEADS).reshape(1, 512)
        gwin = jnp.tile(jnp.concatenate([qk_norm[l, 3], ones64]), N_KV_HEADS).reshape(1, 512)
        gk = jnp.concatenate([qk_norm[l, 1], ones64]).reshape(1, 128)
        w1k = cmp_w1[l, 0].reshape(CMP_BLOCK, HEAD_DIM, HEAD_DIM)
        w1v = cmp_w1[l, 1].reshape(CMP_BLOCK, HEAD_DIM, HEAD_DIM)
        zz = jnp.zeros_like(w1k)
        wfull = jnp.concatenate([jnp.concatenate([w1k, zz], axis=2), jnp.concatenate([zz, w1v], axis=2)], axis=1)
        wab = jnp.concatenate([wfull[:16].reshape(2048, 128), wfull[16:].reshape(2048, 128)], axis=1).astype(bf16)
        posf = jnp.concatenate([cmp_pos[l, 0], cmp_pos[l, 1]], axis=1)
        pos8 = jnp.broadcast_to(jnp.concatenate([posf[:16].reshape(1, 2048), posf[16:].reshape(1, 2048)], axis=1),
                                (8, 4096))
        w2bd = _block_diag(cmp_w2[l]).astype(bf16)

        dt = jnp.exp(ssm_log_dt[l])[:, None]
        a_re, a_im = ssm_a_re[l], ssm_a_im[l]
        mag = jnp.exp(dt * a_re)
        ab_re, ab_im = mag * jnp.cos(dt * a_im), mag * jnp.sin(dt * a_im)
        den = a_re * a_re + a_im * a_im
        f_re = ((ab_re - 1.0) * a_re + ab_im * a_im) / den
        f_im = (ab_im * a_re - (ab_re - 1.0) * a_im) / den
        bb_re = f_re[..., None] * ssm_b_re[l] - f_im[..., None] * ssm_b_im[l]
        bb_im = f_re[..., None] * ssm_b_im[l] + f_im[..., None] * ssm_b_re[l]
        abr = jnp.broadcast_to(ab_re.reshape(1, -1), (8, SSM_GROUPS * SSM_STATE))
        abi = jnp.broadcast_to(ab_im.reshape(1, -1), (8, SSM_GROUPS * SSM_STATE))
        eye8 = jnp.eye(8, dtype=f32)

        def in_blocks(bb):
            xx = jnp.transpose(bb, (0, 2, 1)).reshape(4, 8, SSM_GROUP, SSM_STATE)
            return jnp.einsum('agcn,gk->agckn', xx, eye8).reshape(4, 128, 512).astype(bf16)

        def out_blocks(cc):
            yy = jnp.transpose(cc, (0, 2, 1)).reshape(4, 8, SSM_STATE, SSM_GROUP)
            return jnp.einsum('agnc,gk->agnkc', yy, eye8).reshape(4, 512, 128).astype(bf16)

        s5p = (abr, abi, in_blocks(bb_re), in_blocks(bb_im), out_blocks(ssm_c_re[l]), out_blocks(ssm_c_im[l]),
               ssm_d[l].reshape(1, 512), ssm_glu_w[l].astype(bf16), ssm_glu_b[l].reshape(1, 512))
        gw = (gmlp_ws[l] * tril).astype(bf16)
        gbs = jnp.concatenate([gmlp_bs[l].T, jnp.zeros((CHUNK, 128 - GMLP_GROUPS), f32)], axis=1)
        rw = jnp.concatenate([router_group_w[l], jnp.transpose(router_expert_w[l], (1, 0, 2)).reshape(D_MODEL, N_EXPERTS),
                              jnp.zeros((D_MODEL, 128 - MOE_GROUPS - N_EXPERTS), f32)], axis=1)
        rwh = rw.astype(bf16)
        rwl = (rw - rwh.astype(f32)).astype(bf16)
        rb = jnp.concatenate([router_group_b[l], router_expert_b[l].reshape(-1),
                              jnp.zeros((128 - MOE_GROUPS - N_EXPERTS,), f32)]).reshape(1, 128)

        proj = _proj_call(x, y2, norm_mix[l].reshape(1, D_MODEL), w_pad)
        qn, ksel, kwin, gates, uga, vn = _post_call(proj, gq, gsel, gwin, gmlp_norm[l].reshape(1, 512), g64, g128)

        src_p = proj.reshape(nt // PAGE_SIZE, 1, PAGE_SIZE, PROJ_W)
        kvc_p = _cmp_call(pt_ident, src_p, 0, ATTN_WIDTH // 128, bp, t // PAGE_SIZE, wab, pos8, w2bd, gk)
        kvc_s = _cmp_call(pt_flat, pool_cmp, l, 0, bs, npg, wab, pos8, w2bd, gk)
        oa_p = _pattn_call(qn, kvc_p, ksel, kwin, gates, cbl, tz, ov, ex, bp, t, tq)

        qs = qn[n_p:n_p + n_s].reshape(bs, s_new, N_KV_HEADS, GQA, HEAD_DIM)
        qa = jnp.transpose(qs, (0, 2, 4, 3, 1)).reshape(bs, N_KV_HEADS, HEAD_DIM, GQA * s_new)
        qbig = jnp.einsum('bhdc,hk->bhdkc', qa, jnp.eye(N_KV_HEADS, dtype=bf16))
        qbig = jnp.pad(qbig, ((0, 0), (0, 0), (0, 64), (0, 0), (0, 0))).reshape(bs, 512, 128)
        gs = gates[n_p:n_p + n_s, :48].reshape(bs, s_new, 3, N_KV_HEADS, GQA)
        gt = jnp.pad(jnp.transpose(gs, (0, 3, 4, 1, 2)).reshape(bs, 128, 3), ((0, 0), (0, 0), (0, 125)))
        new_sel = jnp.pad(ksel[n_p:n_p + n_s].reshape(bs, s_new, 512), ((0, 0), (0, 128 - s_new), (0, 0)))
        new_win = jnp.pad(kwin[n_p:n_p + n_s].reshape(bs, s_new, 512), ((0, 0), (0, 128 - s_new), (0, 0)))
        osmp = _sattn_call(pt_flat, qbig, kvc_s, cbt, ovt, smat, e2, pool_sel, sblast, new_sel, nbnew,
                           wcache, wbt, new_win, nbnew, gt, l, bs, npg, s_new)
        oa_s = jnp.transpose(osmp[:, :, 64:].reshape(bs, N_KV_HEADS, GQA, s_new, HEAD_DIM),
                             (0, 3, 1, 2, 4)).reshape(n_s, ATTN_WIDTH)
        oa = pad_rows(jnp.concatenate([oa_p, oa_s], axis=0))

        ob_p, hr_p, hi_p = _s5_call(proj, 0, bp, t, 256, zeros_state, zeros_state, *s5p)
        ob_s, hr_s, hi_s = _s5_call(proj, n_p, bs, s_new, s_new, state_ssm_re[:, l].reshape(bs, -1),
                                    state_ssm_im[:, l].reshape(bs, -1), *s5p)
        ob = pad_rows(jnp.concatenate([ob_p.reshape(n_p, 512), ob_s.reshape(n_s, 512)], axis=0))

        oc_p = _gmlp_call(uga, vn, gw, gbs, 0, n_p, CHUNK)
        oc_s = _gmlp_call(uga, vn, gw, gbs, n_p, n_s, s_new)
        oc = pad_rows(jnp.concatenate([oc_p, oc_s], axis=0))

        x1, xn2, route = _merge_call(x, y2, oa, ob, oc, out_norm[l].reshape(1, -1), w_out[l].astype(bf16),
                                     norm_ffn[l].reshape(1, -1), rwh, rwl, rb)
        tile_e, n_used, src_tok, dst_row, wgt = _route_metadata(route, nt, EXPERT_TM, ntiles)
        y2 = _expert_call(tile_e, n_used, src_tok, dst_row, xn2, wgt, expert_w1[l], expert_w3[l], expert_w2[l],
                          EXPERT_TM, ntiles, 2 * nt + 2 * EXPERT_TM)
        x = x1

        kvshape = (N_KV_HEADS, 2, HEAD_DIM)
        cmp_rows = proj[:, ATTN_WIDTH:ATTN_WIDTH + 512]
        outs['pc'].append(cmp_rows[:n_p].reshape(bp, t, *kvshape))
        outs['ps'].append(ksel[:n_p].reshape(bp, t, *kvshape))
        outs['pw'].append(kwin[:n_p].reshape(bp, t, *kvshape)[:, t - min(WINDOW, t):])
        outs['pr'].append(hr_p[:bp].reshape(bp, SSM_GROUPS, SSM_STATE))
        outs['pi'].append(hi_p[:bp].reshape(bp, SSM_GROUPS, SSM_STATE))
        outs['pv'].append(vn[:n_p].reshape(bp, t, GMLP_WIDTH)[:, (t - 1) // CHUNK * CHUNK:])
        outs['sc'].append(cmp_rows[n_p:n_p + n_s].reshape(bs, s_new, *kvshape))
        outs['ss'].append(ksel[n_p:n_p + n_s].reshape(bs, s_new, *kvshape))
        win_new = kwin[n_p:n_p + n_s].reshape(bs, s_new, *kvshape)
        outs['sw'].append(jnp.concatenate([cache_kv_win[:, l], win_new], axis=1)[:, s_new:])
        outs['sr'].append(hr_s[:bs].reshape(bs, SSM_GROUPS, SSM_STATE))
        outs['si'].append(hi_s[:bs].reshape(bs, SSM_GROUPS, SSM_STATE))
        outs['sv'].append(vn[n_p:n_p + n_s].reshape(bs, s_new, GMLP_WIDTH))

    xf = _add3_call(x, y2)
    st = {k: jnp.stack(v, axis=1) for k, v in outs.items()}
    return (xf[:n_p].reshape(bp, t, D_MODEL), xf[n_p:n_p + n_s].reshape(bs, s_new, D_MODEL),
            st['pc'], st['ps'], st['pw'], st['pr'], st['pi'], st['pv'],
            st['sc'], st['ss'], st['sw'], st['sr'], st['si'], st['sv'])
```

```python
import functools
import math

import numpy as np
import jax
import jax.numpy as jnp
from jax import lax
from jax.experimental import pallas as pl
from jax.experimental.pallas import tpu as pltpu

f32 = jnp.float32
bf16 = jnp.bfloat16

D_MODEL = 2048
DEPTH = 4
PAGE_SIZE = 128
HEAD_DIM = 64
ATTN_WIDTH = 1024
N_HEADS_A = 16
N_KV_HEADS = 4
GQA = 4
KV_WIDTH = 256
CMP_BLOCK = 32
CMP_STRIDE = 16
SEL_BLOCK = 64
SEL_TOPK = 16
WINDOW = 512
NUM_BUCKETS = 32
MAX_DISTANCE = 128
SSM_WIDTH = 512
SSM_GROUP = 16
SSM_GROUPS = 32
SSM_STATE = 64
GMLP_WIDTH = 512
GMLP_GROUPS = 4
CHUNK = 128
MOE_GROUPS = 4
EXPERTS_PER_GROUP = 4
N_EXPERTS = 16
D_EXPERT = 512
SCALE = HEAD_DIM ** -0.5
LOG2E = math.log2(math.e)
EPS = 1e-6
NEG = -1e30

PROJ_W = 4224
LANES = 128
VMEM_LIMIT = 56 * 1024 * 1024


def _cparams(sem, **kw):
    return pltpu.CompilerParams(dimension_semantics=sem, vmem_limit_bytes=VMEM_LIMIT, **kw)


def _dot(a, b):
    return jnp.dot(a, b, preferred_element_type=f32)


def _dot_nt(a, b):
    return lax.dot_general(a, b, (((1,), (1,)), ((), ())), preferred_element_type=f32)


def _split3(x):
    hi = x.astype(bf16)
    r = x - hi.astype(f32)
    mid = r.astype(bf16)
    lo = (r - mid.astype(f32)).astype(bf16)
    return hi, mid, lo


def _gelu(x):
    return 0.5 * x * (1.0 + jnp.tanh(math.sqrt(2.0 / math.pi) * (x + 0.044715 * (x * x * x))))


def _sigmoid(x):
    return 1.0 / (1.0 + jnp.exp(-x))


def _group_mean_sq(x, gmat, group):
    x2 = x * x
    hi = x2.astype(bf16)
    lo = (x2 - hi.astype(f32)).astype(bf16)
    return (_dot(hi, gmat) + _dot(lo, gmat)) * (1.0 / group)


def _residual(refs, has_y):
    if has_y:
        return refs[0][...] + refs[1][...] + refs[2][...]
    return refs[0][...]


def _residual_specs(x, y2, tm, index_of):
    nrow = x.shape[0] // tm
    specs = [pl.BlockSpec((tm, D_MODEL), lambda *ids: (index_of(*ids), 0))]
    args = [x]
    if y2 is not None:
        specs += [pl.BlockSpec((tm, D_MODEL), lambda *ids: (index_of(*ids), 0)),
                  pl.BlockSpec((tm, D_MODEL), lambda *ids: (index_of(*ids) + nrow, 0))]
        args += [y2, y2]
    return specs, args


def _proj_kernel(*refs, has_y):
    nres = 3 if has_y else 1
    g_ref, w_ref, proj_ref, xn_sc = refs[nres:]

    @pl.when(pl.program_id(1) == 0)
    def _():
        x = _residual(refs, has_y)
        ms = jnp.mean(x * x, axis=-1, keepdims=True)
        xn_sc[...] = (x * lax.rsqrt(ms + EPS) * g_ref[...]).astype(bf16)

    proj_ref[...] = _dot(xn_sc[...], w_ref[...])


def _proj_call(x, y2, g, w, tm=320, tn=1408):
    nt = x.shape[0]
    res_specs, res_args = _residual_specs(x, y2, tm, lambda i, j: i)
    return pl.pallas_call(
        functools.partial(_proj_kernel, has_y=y2 is not None),
        grid=(nt // tm, PROJ_W // tn),
        in_specs=res_specs + [pl.BlockSpec((1, D_MODEL), lambda i, j: (0, 0)),
                              pl.BlockSpec((D_MODEL, tn), lambda i, j: (0, j))],
        out_specs=pl.BlockSpec((tm, tn), lambda i, j: (i, j)),
        out_shape=jax.ShapeDtypeStruct((nt, PROJ_W), f32),
        scratch_shapes=[pltpu.VMEM((tm, D_MODEL), bf16)],
        compiler_params=_cparams(("parallel", "arbitrary")),
        name="proj",
    )(*res_args, g, w)


def _post_kernel(q_ref, sel_ref, win_ref, ug_ref, vg_ref, gt_ref, gq_ref, gsel_ref, gwin_ref, gv_ref,
                 g64_ref, g128_ref, qn_ref, ksel_ref, kwin_ref, gate_ref, uga_ref, vn_ref):
    g64 = g64_ref[...]
    lane = lax.broadcasted_iota(jnp.int32, (1, 512), 1)
    is_k = (lane % 128) < 64
    for half in range(2):
        q = q_ref[:, half * 512:(half + 1) * 512]
        ms = _group_mean_sq(q, g64, 64)
        qn = q * lax.rsqrt(ms + EPS) * gq_ref[:, half * 512:(half + 1) * 512]
        qn_ref[:, half * 512:(half + 1) * 512] = (qn * (SCALE * LOG2E)).astype(bf16)
    for src, gref, dst in ((sel_ref, gsel_ref, ksel_ref), (win_ref, gwin_ref, kwin_ref)):
        x = src[...]
        ms = _group_mean_sq(x, g64, 64)
        dst[...] = jnp.where(is_k, x * lax.rsqrt(ms + EPS) * gref[...], x)
    gate_ref[...] = _sigmoid(gt_ref[...])
    uga_ref[...] = _gelu(ug_ref[...])
    v = _gelu(vg_ref[...])
    ms = _group_mean_sq(v, g128_ref[...], 128)
    vn_ref[...] = v * lax.rsqrt(ms + EPS) * gv_ref[...]


def _post_call(proj, gq, gsel, gwin, gv, g64, g128, tm=320):
    nt = proj.shape[0]

    def col(width, idx):
        return pl.BlockSpec((tm, width), lambda i: (i, idx))

    def full(shape):
        return pl.BlockSpec(shape, lambda i: (0,) * len(shape))

    def out(width):
        return pl.BlockSpec((tm, width), lambda i: (i, 0))

    return pl.pallas_call(
        _post_kernel,
        grid=(nt // tm,),
        in_specs=[col(1024, 0), col(512, 3), col(512, 4), col(512, 6), col(512, 7), col(128, 32),
                  full((1, 1024)), full((1, 512)), full((1, 512)), full((1, 512)),
                  full((512, 512)), full((512, 512))],
        out_specs=[out(1024), out(512), out(512), out(128), out(512), out(512)],
        out_shape=[jax.ShapeDtypeStruct((nt, 1024), bf16), jax.ShapeDtypeStruct((nt, 512), f32),
                   jax.ShapeDtypeStruct((nt, 512), f32), jax.ShapeDtypeStruct((nt, 128), f32),
                   jax.ShapeDtypeStruct((nt, 512), f32), jax.ShapeDtypeStruct((nt, 512), f32)],
        compiler_params=_cparams(("parallel",)),
        name="post",
    )(proj, proj, proj, proj, proj, proj, gq, gsel, gwin, gv, g64, g128)


CMP_PAGES_PER_STEP = 32


def _cmp_kernel(pt_ref, *refs, pps, nsteps, nchunk):
    srcs = refs[:pps]
    wab_ref, pos_ref, w2_ref, gk_ref, out_ref, c_sc = refs[pps:]
    s = pl.program_id(2)
    for pp in range(pps // 2):
        row0 = pl.multiple_of(s * (pps * 8) + pp * 16, 16)
        for j in range(CMP_STRIDE):
            a = srcs[2 * pp][pl.ds(j, 8, stride=CMP_STRIDE), :]
            b = srcs[2 * pp + 1][pl.ds(j, 8, stride=CMP_STRIDE), :]
            c_sc[pl.ds(row0, 16), j * 128:(j + 1) * 128] = jnp.concatenate([a, b], axis=0).astype(bf16)

    @pl.when(s == nsteps - 1)
    def _():
        wab = wab_ref[...]
        pq = _dot(c_sc[...], wab)
        pos = pos_ref[...].astype(bf16)
        pterm = _dot(pos[:, :2048], wab[:, :128]) + _dot(pos[:, 2048:], wab[:, 128:])
        pre = pq[:, :128] + pltpu.roll(pq[:, 128:], nchunk - 1, 0) + pterm[0:1, :]
        y = _dot(_gelu(pre).astype(bf16), w2_ref[...])
        lane = lax.broadcasted_iota(jnp.int32, (1, 128), 1)
        is_k = lane < 64
        ms = jnp.sum(jnp.where(is_k, y * y, 0.0), axis=-1, keepdims=True) * (1.0 / 64)
        out_ref[...] = jnp.where(is_k, y * lax.rsqrt(ms + EPS) * gk_ref[...], y)


def _cmp_call(pt_flat, src4, layer, col0, nb, npg, wab, pos, w2, gk):
    nchunk = npg * 8
    pps = min(CMP_PAGES_PER_STEP, npg)
    nsteps = npg // pps

    def src_spec(k):
        return pl.BlockSpec((None, None, PAGE_SIZE, 128),
                            lambda b, h, s, pt: (pt[b * npg + s * pps + k], layer, 0, col0 + h))

    def full(shape):
        return pl.BlockSpec(shape, lambda b, h, s, pt: (0,) * len(shape))

    return pl.pallas_call(
        functools.partial(_cmp_kernel, pps=pps, nsteps=nsteps, nchunk=nchunk),
        grid_spec=pltpu.PrefetchScalarGridSpec(
            num_scalar_prefetch=1,
            grid=(nb, N_KV_HEADS, nsteps),
            in_specs=[src_spec(k) for k in range(pps)]
                     + [full((2048, 256)), full((8, 4096)), full((128, 128)), full((1, 128))],
            out_specs=pl.BlockSpec((None, nchunk, 128), lambda b, h, s, pt: (b, 0, h)),
            scratch_shapes=[pltpu.VMEM((nchunk, 2048), bf16)],
        ),
        out_shape=jax.ShapeDtypeStruct((nb, nchunk, 512), f32),
        compiler_params=_cparams(("parallel", "parallel", "arbitrary")),
        name="compress",
    )(pt_flat, *([src4] * pps), wab, pos, w2, gk)


def _pattn_kernel(q_ref, kvc_ref, ksel_ref, kwin_ref, gate_ref, cbt_ref, tzt_ref, ovt_ref, ext_ref, out_ref,
                  kb_sel, vat_sel, kb_win, vat_win, kcb, vcat, sb_sc, gt_sc, m_sc, acc_sel, acc_win, *, tq, t):
    h = pl.program_id(1)
    i = pl.program_id(2)
    c4 = GQA * tq
    nsel = t // SEL_BLOCK
    ntile = t // tq
    nwin = WINDOW // tq
    ncc = t // CMP_STRIDE
    ksel = min(SEL_TOPK, nsel)

    @pl.when(i == 0)
    def _():
        lane = lax.broadcasted_iota(jnp.int32, (1, 128), 1)
        for src, kb, vat in ((ksel_ref, kb_sel, vat_sel), (kwin_ref, kb_win, vat_win)):
            blk = src[...]
            kb[...] = blk[:, :64].astype(bf16)
            va = jnp.where(lane < 64, pltpu.roll(blk, 64, 1), 1.0)
            for jj in range(ntile):
                vat[jj] = jnp.transpose(va[jj * tq:(jj + 1) * tq]).astype(bf16)
        kvc = kvc_ref[...]
        kcb[...] = kvc[:, :64].astype(bf16)
        vcat[...] = jnp.transpose(jnp.where(lane < 64, pltpu.roll(kvc, 64, 1), 0.0)).astype(bf16)

    qt = jnp.transpose(q_ref[...].astype(f32))
    q4t = jnp.concatenate([qt[g * 64:(g + 1) * 64] for g in range(GQA)], axis=1).astype(bf16)
    gt_sc[...] = jnp.transpose(gate_ref[...])

    def lanes4(a):
        return jnp.concatenate([a] * GQA, axis=1)

    nrow = lax.broadcasted_iota(jnp.int32, (ncc, tq), 0)
    qpos = i * tq + lax.broadcasted_iota(jnp.int32, (ncc, tq), 1)
    cvis = lanes4((CMP_STRIDE * nrow + CMP_BLOCK - 1 <= qpos) & (nrow < (t - CMP_BLOCK) // CMP_STRIDE + 1))
    shift = pl.multiple_of((tq // CMP_STRIDE) * (ntile - 1 - i), tq // CMP_STRIDE)
    cb = jnp.concatenate([cbt_ref[g, pl.ds(shift, ncc), :] for g in range(GQA)], axis=1)
    st = jnp.where(cvis, _dot(kcb[...], q4t) + cb, NEG)
    m = jnp.max(st, axis=0, keepdims=True)
    e = jnp.where(cvis, jnp.exp2(st - m), 0.0)
    p = e / jnp.maximum(jnp.sum(e, axis=0, keepdims=True), 1e-30)
    o_cmp = _dot(vcat[...], p.astype(bf16))
    pg = p[:, 0:tq] + p[:, tq:2 * tq] + p[:, 2 * tq:3 * tq] + p[:, 3 * tq:4 * tq]
    ovt = ovt_ref[...]
    hi, mid, lo = _split3(pg)
    score = (_dot(ovt, hi) + _dot(ovt, mid) + _dot(ovt, lo))[0:nsel]

    blk = lax.broadcasted_iota(jnp.int32, (nsel, tq), 0)
    qp = i * tq + lax.broadcasted_iota(jnp.int32, (nsel, tq), 1)
    cur = qp // SEL_BLOCK
    forced = (blk == 0) | (blk == cur) | (blk == cur - 1)
    score = jnp.where(forced, 1e6, jnp.where(blk * SEL_BLOCK > qp, -1.0, score))
    rank = jnp.zeros((nsel, tq), f32)
    for sp in range(nsel):
        other = score[sp:sp + 1, :]
        ahead = (other > score) | ((other == score) & (blk > sp))
        rank = rank + jnp.where(ahead, 1.0, 0.0)
    selneg = jnp.where(rank < ksel, 0.0, NEG)
    selneg = jnp.concatenate([selneg, jnp.zeros((128 - nsel, tq), f32)], axis=0).astype(bf16)
    for jj in range(ntile):
        sb_sc[jj] = _dot(ext_ref[jj * tq:(jj + 1) * tq, :], selneg)

    def flash(kb, vat, acc, tiles):
        sts = []
        for j, bias in tiles:
            k0 = pl.multiple_of(j * tq, tq)
            st = _dot(kb[pl.ds(k0, tq), :], q4t)
            sts.append(st if bias is None else st + bias)
        m_old = m_sc[...]
        m_new = m_old
        for st in sts:
            m_new = jnp.maximum(m_new, jnp.max(st, axis=0, keepdims=True))
        upd = acc[...] * jnp.exp2(m_old - m_new)
        for (j, _), st in zip(tiles, sts):
            upd = upd + _dot(vat[j], jnp.exp2(st - m_new).astype(bf16))
        acc[...] = upd
        m_sc[...] = m_new

    def sel_mask(j):
        return lanes4(sb_sc[j])

    m_sc[...] = jnp.full((1, c4), NEG, f32)
    acc_sel[...] = jnp.zeros((128, c4), f32)
    nfar = jnp.maximum(i - 1, 0)

    def sel_far(p, c):
        flash(kb_sel, vat_sel, acc_sel, [(2 * p, sel_mask(2 * p)), (2 * p + 1, sel_mask(2 * p + 1))])
        return c

    lax.fori_loop(0, nfar // 2, sel_far, 0)

    @pl.when(nfar % 2 == 1)
    def _():
        flash(kb_sel, vat_sel, acc_sel, [(nfar - 1, sel_mask(nfar - 1))])

    @pl.when(i >= 1)
    def _():
        flash(kb_sel, vat_sel, acc_sel, [(i - 1, tzt_ref[1] + sel_mask(i - 1)), (i, tzt_ref[0] + sel_mask(i))])

    @pl.when(i == 0)
    def _():
        flash(kb_sel, vat_sel, acc_sel, [(0, tzt_ref[0] + sel_mask(0))])

    m_sc[...] = jnp.full((1, c4), NEG, f32)
    acc_win[...] = jnp.zeros((128, c4), f32)

    def win_far(j, c):
        flash(kb_win, vat_win, acc_win, [(j, None)])
        return c

    lax.fori_loop(jnp.maximum(i - nwin + 1, 0), jnp.maximum(i - 1, 0), win_far, 0)

    @pl.when(i >= nwin)
    def _():
        krow = lax.broadcasted_iota(jnp.int32, (tq, tq), 0)
        qcol = lax.broadcasted_iota(jnp.int32, (tq, tq), 1)
        edge = lanes4(jnp.where(qcol <= krow, 0.0, NEG))
        flash(kb_win, vat_win, acc_win, [(i - nwin, edge), (i - 1, tzt_ref[1]), (i, tzt_ref[0])])

    @pl.when((i >= 1) & (i < nwin))
    def _():
        flash(kb_win, vat_win, acc_win, [(i - 1, tzt_ref[1]), (i, tzt_ref[0])])

    @pl.when(i == 0)
    def _():
        flash(kb_win, vat_win, acc_win, [(0, tzt_ref[0])])

    a_sel = acc_sel[...]
    a_win = acc_win[...]
    o_sel = a_sel[0:64] / a_sel[64:65]
    o_win = a_win[0:64] / a_win[64:65]

    def grow(branch):
        return jnp.concatenate([gt_sc[pl.ds(branch * N_HEADS_A + h * GQA + g, 1), :] for g in range(GQA)], axis=1)

    comb = grow(0) * o_cmp[0:64] + grow(1) * o_sel + grow(2) * o_win
    out_ref[...] = jnp.concatenate([jnp.transpose(comb[:, g * tq:(g + 1) * tq]) for g in range(GQA)], axis=1)


def _pattn_call(qn, kvc, ksel, kwin, gates, cbt, tzt, ovt, ext, nb, t, tq=256):
    ntile = t // tq
    c4 = GQA * tq
    ncc = t // CMP_STRIDE
    grid = (nb, N_KV_HEADS, ntile)
    return pl.pallas_call(
        functools.partial(_pattn_kernel, tq=tq, t=t),
        grid=grid,
        in_specs=[
            pl.BlockSpec((tq, 256), lambda b, h, i: (b * ntile + i, h)),
            pl.BlockSpec((None, ncc, 128), lambda b, h, i: (b, 0, h)),
            pl.BlockSpec((t, 128), lambda b, h, i: (b, h)),
            pl.BlockSpec((t, 128), lambda b, h, i: (b, h)),
            pl.BlockSpec((tq, 128), lambda b, h, i: (b * ntile + i, 0)),
            pl.BlockSpec((GQA, 2 * ncc, tq), lambda b, h, i: (h, 0, 0)),
            pl.BlockSpec((None, 2, tq, c4), lambda b, h, i: (h, 0, 0, 0)),
            pl.BlockSpec((128, ncc), lambda b, h, i: (0, 0)),
            pl.BlockSpec((t, 128), lambda b, h, i: (0, 0)),
        ],
        out_specs=pl.BlockSpec((tq, 256), lambda b, h, i: (b * ntile + i, h)),
        out_shape=jax.ShapeDtypeStruct((nb * t, ATTN_WIDTH), f32),
        scratch_shapes=[
            pltpu.VMEM((t, 64), bf16), pltpu.VMEM((ntile, 128, tq), bf16),
            pltpu.VMEM((t, 64), bf16), pltpu.VMEM((ntile, 128, tq), bf16),
            pltpu.VMEM((ncc, 64), bf16), pltpu.VMEM((128, ncc), bf16),
            pltpu.VMEM((ntile, tq, tq), f32), pltpu.VMEM((128, tq), f32),
            pltpu.VMEM((1, c4), f32), pltpu.VMEM((128, c4), f32), pltpu.VMEM((128, c4), f32),
        ],
        compiler_params=_cparams(("parallel", "parallel", "arbitrary")),
        name="prompt_attn",
    )(qn, kvc, ksel, kwin, gates, cbt, tzt, ovt, ext)


CH_PAGES = 16


def _topk_neg_rows(score, rowf, k):
    neg = jnp.full(score.shape, NEG, f32)
    work = score
    for _ in range(k):
        m = jnp.max(work, axis=0, keepdims=True)
        idx = jnp.min(jnp.where(work == m, rowf, 1e9), axis=0, keepdims=True)
        hit = rowf == idx
        neg = jnp.where(hit, 0.0, neg)
        work = jnp.where(hit, -3e38, work)
    return neg


def _col_of(vec):
    return jnp.transpose(jnp.broadcast_to(vec, (128, 128)))[:, :1]


def _sattn_kernel(pt_ref, qbig_ref, kvc_ref, cbt_ref, ovt_ref, smat_ref, e2_ref, pool_ref, sblast_ref,
                  nsel_ref, nbsel_ref, wcache_ref, wbt_ref, nwin_ref, nbwin_ref, gt_ref, out_ref,
                  buf, sem, seln_sc, m_sc, l_sc, acc_sc, *, layer, npg, n_new):
    b = pl.program_id(0)
    nch = npg // CH_PAGES
    ck = CH_PAGES * PAGE_SIZE
    past = npg * PAGE_SIZE
    qbig = qbig_ref[...]

    def page_copy(c, k, slot):
        pg = pt_ref[b * npg + c * CH_PAGES + k]
        return pltpu.make_async_copy(pool_ref.at[pg, layer], buf.at[slot, k], sem.at[slot])

    def start_chunk(c, slot):
        for k in range(CH_PAGES):
            page_copy(c, k, slot).start()

    def wait_chunk(c, slot):
        for k in range(CH_PAGES):
            page_copy(c, k, slot).wait()

    start_chunk(0, 0)
    if nch > 1:
        start_chunk(1, 1)

    def reset():
        m_sc[...] = jnp.full((1, 128), NEG, f32)
        l_sc[...] = jnp.zeros((1, 128), f32)
        acc_sc[...] = jnp.zeros((128, 512), f32)

    def attend(rows, bias_t):
        rb = rows.astype(bf16)
        st = _dot(rb, qbig) + bias_t
        m_old = m_sc[...]
        m_new = jnp.maximum(m_old, jnp.max(st, axis=0, keepdims=True))
        alpha = jnp.exp2(m_old - m_new)
        pt = jnp.exp2(st - m_new)
        l_sc[...] = l_sc[...] * alpha + jnp.sum(pt, axis=0, keepdims=True)
        acc_sc[...] = acc_sc[...] * _col_of(alpha) + _dot(jnp.transpose(pt).astype(bf16), rb)
        m_sc[...] = m_new

    rowhead = lax.broadcasted_iota(jnp.int32, (128, 128), 0) // 32

    def own_head(res):
        out = jnp.zeros((128, 128), f32)
        for hh in range(N_KV_HEADS):
            out = jnp.where(rowhead == hh, res[:, hh * 128:(hh + 1) * 128], out)
        return out

    kvc = kvc_ref[...]
    kb = kvc.astype(bf16)
    st = _dot(kb, qbig) + cbt_ref[...]
    m = jnp.max(st, axis=0, keepdims=True)
    e = jnp.exp2(st - m)
    pt = e / jnp.maximum(jnp.sum(e, axis=0, keepdims=True), 1e-30)
    o_cmp = own_head(_dot(jnp.transpose(pt).astype(bf16), kb))
    smat = smat_ref[...]
    hi, mid, lo = _split3(pt)
    pg = _dot(hi, smat) + _dot(mid, smat) + _dot(lo, smat)
    hi, mid, lo = _split3(pg)
    ovt = ovt_ref[...]
    score = _dot(ovt, hi) + _dot(ovt, mid) + _dot(ovt, lo)
    nrow = score.shape[0]
    blk = lax.broadcasted_iota(jnp.int32, (nrow, 128), 0)
    qpos = past + lax.broadcasted_iota(jnp.int32, (nrow, 128), 1) % 8
    cur = qpos // SEL_BLOCK
    forced = (blk == 0) | (blk == cur) | (blk == cur - 1)
    future = blk * SEL_BLOCK > qpos
    score = jnp.where(forced, 1e6, jnp.where(future, -1.0, score))
    nsel = -(-(past + n_new) // SEL_BLOCK)
    score = jnp.where(blk < nsel, score, -2.0)
    seln_sc[0:nrow, :] = _topk_neg_rows(score, blk.astype(f32), min(SEL_TOPK, nsel))
    seln_sc[nrow:, :] = jnp.zeros((seln_sc.shape[0] - nrow, 128), f32)

    reset()
    bpc = ck // SEL_BLOCK

    def sel_bias(c):
        r0 = pl.multiple_of(c * bpc, bpc)
        return _dot(e2_ref[...], seln_sc[pl.ds(r0, 128), :].astype(bf16))

    def chunk_body(c, carry):
        slot = c % 2
        wait_chunk(c, slot)
        attend(buf[slot].reshape(ck, 512), sel_bias(c))

        @pl.when(c + 2 < nch)
        def _():
            start_chunk(c + 2, slot)

        return carry

    lax.fori_loop(0, nch - 1, chunk_body, 0)
    last = nch - 1
    wait_chunk(last, last % 2)
    attend(buf[last % 2].reshape(ck, 512), sel_bias(last) + sblast_ref[...])
    attend(nsel_ref[...], nbsel_ref[...] + seln_sc[past // SEL_BLOCK:past // SEL_BLOCK + 1, :])
    o_sel = own_head(acc_sc[...]) / _col_of(l_sc[...])

    reset()
    attend(wcache_ref[...], wbt_ref[...])
    attend(nwin_ref[...], nbwin_ref[...])
    o_win = own_head(acc_sc[...]) / _col_of(l_sc[...])

    gts = gt_ref[...]
    out_ref[...] = gts[:, 0:1] * o_cmp + gts[:, 1:2] * o_sel + gts[:, 2:3] * o_win


def _sattn_call(pt_flat, qbig, kvc, cbt, ovt, smat, e2, pool, sblast, nsel, nbsel, wcache, wbt, nwin, nbwin,
                gt, layer, nb, npg, n_new):
    nrow = ovt.shape[0]
    ck = CH_PAGES * PAGE_SIZE

    def full(shape):
        return pl.BlockSpec(shape, lambda b, pt: (0,) * len(shape))

    def perb(shape):
        return pl.BlockSpec((None,) + shape, lambda b, pt: (b,) + (0,) * len(shape))

    return pl.pallas_call(
        functools.partial(_sattn_kernel, layer=layer, npg=npg, n_new=n_new),
        grid_spec=pltpu.PrefetchScalarGridSpec(
            num_scalar_prefetch=1,
            grid=(nb,),
            in_specs=[perb((512, 128)), perb((npg * 8, 512)), full(cbt.shape), full(ovt.shape), full((128, 128)),
                      full((ck, 128)), pl.BlockSpec(memory_space=pl.ANY), full((ck, 128)),
                      perb((128, 512)), full((128, 128)),
                      pl.BlockSpec((None, None, WINDOW, 512), lambda b, pt: (b, layer, 0, 0)), full((WINDOW, 128)),
                      perb((128, 512)), full((128, 128)), perb((128, 128))],
            out_specs=perb((128, 128)),
            scratch_shapes=[pltpu.VMEM((2, CH_PAGES, PAGE_SIZE, 512), f32), pltpu.SemaphoreType.DMA((2,)),
                            pltpu.VMEM((nrow + 128, 128), f32), pltpu.VMEM((1, 128), f32), pltpu.VMEM((1, 128), f32),
                            pltpu.VMEM((128, 512), f32)],
        ),
        out_shape=jax.ShapeDtypeStruct((nb, 128, 128), f32),
        compiler_params=_cparams(("arbitrary",)),
        name="sample_attn",
    )(pt_flat, qbig, kvc, cbt, ovt, smat, e2, pool, sblast, nsel, nbsel, wcache, wbt, nwin, nbwin, gt)


def _s5_kernel(*refs, nb, tc):
    u_refs = refs[:nb]
    (h0r_ref, h0i_ref, abr_ref, abi_ref, bre_ref, bim_ref, cre_ref, cim_ref, d_ref, gw_ref, gb_ref,
     o_ref, hr_out, hi_out, xr, xi, hr, hi) = refs[nb:]
    c = pl.program_id(0)

    @pl.when(c == 0)
    def _():
        hr[...] = h0r_ref[...]
        hi[...] = h0i_ref[...]

    for b in range(nb):
        u = u_refs[b][...].astype(bf16)
        for cb in range(4):
            ub = u[:, cb * 128:(cb + 1) * 128]
            pr = _dot(ub, bre_ref[cb])
            pi = _dot(ub, bim_ref[cb])
            for k in range(4):
                xr[cb * 4 + k, b * tc:(b + 1) * tc, :] = pr[:, k * 128:(k + 1) * 128]
                xi[cb * 4 + k, b * tc:(b + 1) * tc, :] = pi[:, k * 128:(k + 1) * 128]

    for lc in range(4):
        slabs = [lc * 4 + k for k in range(4)]
        ar = [abr_ref[0:nb, j * 128:(j + 1) * 128] for j in slabs]
        ai = [abi_ref[0:nb, j * 128:(j + 1) * 128] for j in slabs]

        def body(t, carry, slabs=slabs, ar=ar, ai=ai):
            rows = pl.ds(t, nb, stride=tc)
            new = []
            for k, j in enumerate(slabs):
                cr, ci = carry[2 * k], carry[2 * k + 1]
                nr = ar[k] * cr - ai[k] * ci + xr[j, rows, :]
                ni = ar[k] * ci + ai[k] * cr + xi[j, rows, :]
                xr[j, rows, :] = nr
                xi[j, rows, :] = ni
                new += [nr, ni]
            return tuple(new)

        init = []
        for j in slabs:
            init += [hr[0:nb, j * 128:(j + 1) * 128], hi[0:nb, j * 128:(j + 1) * 128]]
        fin = lax.fori_loop(0, tc, body, tuple(init), unroll=4)
        for k, j in enumerate(slabs):
            hr[0:nb, j * 128:(j + 1) * 128] = fin[2 * k]
            hi[0:nb, j * 128:(j + 1) * 128] = fin[2 * k + 1]

    ys = []
    for cb in range(4):
        hre = jnp.concatenate([xr[cb * 4 + k] for k in range(4)], axis=1).astype(bf16)
        him = jnp.concatenate([xi[cb * 4 + k] for k in range(4)], axis=1).astype(bf16)
        ys.append(_dot(hre, cre_ref[cb]) - _dot(him, cim_ref[cb]))
    u_all = jnp.concatenate([u_refs[b][...] for b in range(nb)], axis=0)
    y = _gelu(jnp.concatenate(ys, axis=1) + d_ref[...] * u_all)
    o = y * _sigmoid(_dot(y.astype(bf16), gw_ref[...]) + gb_ref[...])
    for b in range(nb):
        o_ref[b] = o[b * tc:(b + 1) * tc]

    @pl.when(c == pl.num_programs(0) - 1)
    def _():
        hr_out[...] = hr[...]
        hi_out[...] = hi[...]


def _s5_call(proj, row0, nb, t, tc, h0r, h0i, abr, abi, bre, bim, cre, cim, d, gw, gb):
    def full(shape):
        return pl.BlockSpec(shape, lambda c: (0,) * len(shape))

    u_specs = [pl.BlockSpec((tc, 512), lambda c, b=b: ((row0 + b * t) // tc + c, 5)) for b in range(nb)]
    return pl.pallas_call(
        functools.partial(_s5_kernel, nb=nb, tc=tc),
        grid=(t // tc,),
        in_specs=u_specs + [full((8, 2048)), full((8, 2048)), full((8, 2048)), full((8, 2048)),
                            full((4, 128, 512)), full((4, 128, 512)), full((4, 512, 128)), full((4, 512, 128)),
                            full((1, 512)), full((512, 512)), full((1, 512))],
        out_specs=[pl.BlockSpec((nb, tc, 512), lambda c: (0, c, 0)), full((8, 2048)), full((8, 2048))],
        out_shape=[jax.ShapeDtypeStruct((nb, t, 512), f32), jax.ShapeDtypeStruct((8, 2048), f32),
                   jax.ShapeDtypeStruct((8, 2048), f32)],
        scratch_shapes=[pltpu.VMEM((16, nb * tc, 128), f32), pltpu.VMEM((16, nb * tc, 128), f32),
                        pltpu.VMEM((8, 2048), f32), pltpu.VMEM((8, 2048), f32)],
        compiler_params=_cparams(("arbitrary",)),
        name="s5",
    )(*([proj] * nb), h0r, h0i, abr, abi, bre, bim, cre, cim, d, gw, gb)


def _gmlp_kernel(u_ref, v_ref, w_ref, bs_ref, o_ref, *, rows):
    u = u_ref[...]
    v = v_ref[...]
    outs = []
    for g in range(GMLP_GROUPS):
        vg = v[:, g * 128:(g + 1) * 128]
        if rows < CHUNK:
            vg = jnp.concatenate([vg, jnp.zeros((CHUNK - rows, 128), f32)], axis=0)
        mixed = _dot(w_ref[g][0:rows, :], vg.astype(bf16)) + bs_ref[0:rows, g:g + 1]
        outs.append(u[:, g * 128:(g + 1) * 128] * mixed)
    o_ref[...] = jnp.concatenate(outs, axis=1)


def _gmlp_call(uga, vn, w, bs, row0, nrows, rows):
    blk0 = row0 // rows

    def full(shape):
        return pl.BlockSpec(shape, lambda i: (0,) * len(shape))

    return pl.pallas_call(
        functools.partial(_gmlp_kernel, rows=rows),
        grid=(nrows // rows,),
        in_specs=[pl.BlockSpec((rows, 512), lambda i: (blk0 + i, 0)), pl.BlockSpec((rows, 512), lambda i: (blk0 + i, 0)),
                  full((GMLP_GROUPS, CHUNK, CHUNK)), full((CHUNK, 128))],
        out_specs=pl.BlockSpec((rows, 512), lambda i: (i, 0)),
        out_shape=jax.ShapeDtypeStruct((nrows, 512), f32),
        compiler_params=_cparams(("parallel",)),
        name="gmlp",
    )(uga, vn, w, bs)


def _merge_kernel(*refs, has_y):
    nres = 3 if has_y else 1
    oa_ref, ob_ref, oc_ref, gn_ref, w_ref, gf_ref, rwh_ref, rwl_ref, rb_ref, x1_ref, xn_ref, rt_ref = refs[nres:]

    def rms(v, g):
        return (v * lax.rsqrt(jnp.mean(v * v, axis=-1, keepdims=True) + EPS) * g).astype(bf16)

    acc = _dot(rms(oa_ref[...], gn_ref[:, 0:1024]), w_ref[0:1024, :])
    acc += _dot(rms(ob_ref[...], gn_ref[:, 1024:1536]), w_ref[1024:1536, :])
    acc += _dot(rms(oc_ref[...], gn_ref[:, 1536:2048]), w_ref[1536:2048, :])
    x1 = _residual(refs, has_y) + acc
    x1_ref[...] = x1
    xn = x1 * lax.rsqrt(jnp.mean(x1 * x1, axis=-1, keepdims=True) + EPS) * gf_ref[...]
    xn_ref[...] = xn
    hi = xn.astype(bf16)
    lo = (xn - hi.astype(f32)).astype(bf16)
    logits = _dot(hi, rwh_ref[...]) + _dot(lo, rwh_ref[...]) + _dot(hi, rwl_ref[...]) + rb_ref[...]
    lane = lax.broadcasted_iota(jnp.int32, logits.shape, 1)
    lanef = lane.astype(f32)
    is_g = lane < MOE_GROUPS
    gl = jnp.where(is_g, logits, NEG)
    gm = jnp.max(gl, axis=-1, keepdims=True)
    gidx = jnp.min(jnp.where(gl == gm, lanef, 1e9), axis=-1, keepdims=True)
    gprob = 1.0 / jnp.sum(jnp.where(is_g, jnp.exp(logits - gm), 0.0), axis=-1, keepdims=True)
    lo_lane = MOE_GROUPS + EXPERTS_PER_GROUP * gidx
    inl = jnp.where((lanef >= lo_lane) & (lanef < lo_lane + EXPERTS_PER_GROUP), logits, NEG)
    v1 = jnp.max(inl, axis=-1, keepdims=True)
    i1 = jnp.min(jnp.where(inl == v1, lanef, 1e9), axis=-1, keepdims=True)
    inl2 = jnp.where(lanef == i1, NEG, inl)
    v2 = jnp.max(inl2, axis=-1, keepdims=True)
    i2 = jnp.min(jnp.where(inl2 == v2, lanef, 1e9), axis=-1, keepdims=True)
    e2 = jnp.exp(v2 - v1)
    w1 = gprob / (1.0 + e2)
    w2 = gprob * e2 / (1.0 + e2)
    rt_ref[...] = jnp.where(lane == 0, i1 - MOE_GROUPS,
                            jnp.where(lane == 1, i2 - MOE_GROUPS,
                                      jnp.where(lane == 2, w1, jnp.where(lane == 3, w2, 0.0))))


def _merge_call(x, y2, oa, ob, oc, gn, w, gf, rwh, rwl, rb, tm=320):
    nt = x.shape[0]
    res_specs, res_args = _residual_specs(x, y2, tm, lambda i: i)

    def row(width):
        return pl.BlockSpec((tm, width), lambda i: (i, 0))

    def full(shape):
        return pl.BlockSpec(shape, lambda i: (0,) * len(shape))

    return pl.pallas_call(
        functools.partial(_merge_kernel, has_y=y2 is not None),
        grid=(nt // tm,),
        in_specs=res_specs + [row(1024), row(512), row(512), full((1, 2048)), full((2048, 2048)), full((1, 2048)),
                              full((2048, 128)), full((2048, 128)), full((1, 128))],
        out_specs=[row(2048), row(2048), row(128)],
        out_shape=[jax.ShapeDtypeStruct((nt, 2048), f32), jax.ShapeDtypeStruct((nt, 2048), f32),
                   jax.ShapeDtypeStruct((nt, 128), f32)],
        compiler_params=_cparams(("parallel",)),
        name="merge_router",
    )(*res_args, oa, ob, oc, gn, w, gf, rwh, rwl, rb)


def _expert_kernel(te_ref, nu_ref, src_ref, dst_ref, xn_hbm, wg_ref, w1_ref, w3_ref, w2_ref, y_hbm,
                   xbuf, obuf, w1b, w3b, w2b, gsem, ssem, *, tm, ntiles):
    t = pl.program_id(0)
    n_used = nu_ref[0]
    slot = t % 2

    def gather_row(tt, sl, r):
        tok = src_ref[tt * tm + r]
        pltpu.make_async_copy(xn_hbm.at[pl.ds(tok, 1)], xbuf.at[sl, pl.ds(r, 1)], gsem.at[sl]).start()

    def gather_wait(sl):
        pltpu.make_async_copy(xn_hbm.at[pl.ds(0, tm)], xbuf.at[sl], gsem.at[sl]).wait()

    def scatter_row(tt, sl, r):
        row = dst_ref[tt * tm + r]
        pltpu.make_async_copy(obuf.at[sl, pl.ds(r, 1)], y_hbm.at[pl.ds(row, 1)], ssem.at[sl]).start()

    def scatter_wait(sl):
        pltpu.make_async_copy(obuf.at[sl], y_hbm.at[pl.ds(0, tm)], ssem.at[sl]).wait()

    def rolled(fn, tt, sl):
        def body(r, c):
            fn(tt, sl, r)
            return c
        lax.fori_loop(0, tm, body, 0, unroll=8)

    @pl.when(t == 0)
    def _():
        rolled(gather_row, 0, 0)
        npair = y_hbm.shape[0] - 2 * tm
        for sl in range(2):
            obuf[sl] = jnp.zeros((tm, D_MODEL), f32)
            cp = pltpu.make_async_copy(obuf.at[sl], y_hbm.at[pl.ds(npair + sl * tm, tm)], ssem.at[sl])
            cp.start()
            cp.wait()

    def tile_step(with_scatter):
        gather_wait(slot)

        @pl.when(t >= 3)
        def _():
            scatter_wait(t % 3)

        @pl.when((t == 0) | (te_ref[t] != te_ref[jnp.maximum(t - 1, 0)]))
        def _():
            w1b[...] = w1_ref[...].astype(bf16)
            w3b[...] = w3_ref[...].astype(bf16)
            w2b[...] = w2_ref[...].astype(bf16)

        nxt = jnp.minimum(t + 1, ntiles - 1)
        for r in range(tm):
            gather_row(nxt, 1 - slot, r)
        x = xbuf[slot].astype(bf16)
        a = _dot(x, w1b[...])
        hmid = a * _sigmoid(a) * _dot(x, w3b[...]) * wg_ref[...]
        if with_scatter:
            prev_slot = (t - 1) % 3
            for r in range(tm):
                scatter_row(t - 1, prev_slot, r)
        obuf[t % 3] = _dot(hmid.astype(bf16), w2b[...])

    @pl.when((t < n_used) & (t == 0))
    def _():
        tile_step(False)

    @pl.when((t < n_used) & (t >= 1))
    def _():
        tile_step(True)

    @pl.when(t == ntiles - 1)
    def _():
        last = n_used - 1
        gather_wait(n_used % 2)

        @pl.when(last >= 2)
        def _():
            scatter_wait((last - 2) % 3)

        @pl.when(last >= 1)
        def _():
            scatter_wait((last - 1) % 3)

        rolled(scatter_row, last, last % 3)
        scatter_wait(last % 3)


def _expert_call(tile_e, n_used, src_tok, dst_row, xn, wgt, w1, w3, w2, layer, tm, ntiles, nrows_out):
    def wmap(t, te, nu, src, dst):
        return (layer, te[t], 0, 0)

    return pl.pallas_call(
        functools.partial(_expert_kernel, tm=tm, ntiles=ntiles),
        grid_spec=pltpu.PrefetchScalarGridSpec(
            num_scalar_prefetch=4,
            grid=(ntiles,),
            in_specs=[pl.BlockSpec(memory_space=pl.ANY),
                      pl.BlockSpec((tm, 1), lambda t, te, nu, src, dst: (t, 0)),
                      pl.BlockSpec((None, None, D_MODEL, D_EXPERT), wmap),
                      pl.BlockSpec((None, None, D_MODEL, D_EXPERT), wmap),
                      pl.BlockSpec((None, None, D_EXPERT, D_MODEL), wmap)],
            out_specs=pl.BlockSpec(memory_space=pl.ANY),
            scratch_shapes=[pltpu.VMEM((2, tm, D_MODEL), f32), pltpu.VMEM((3, tm, D_MODEL), f32),
                            pltpu.VMEM((D_MODEL, D_EXPERT), bf16), pltpu.VMEM((D_MODEL, D_EXPERT), bf16),
                            pltpu.VMEM((D_EXPERT, D_MODEL), bf16),
                            pltpu.SemaphoreType.DMA((2,)), pltpu.SemaphoreType.DMA((3,))],
        ),
        out_shape=jax.ShapeDtypeStruct((nrows_out, D_MODEL), f32),
        compiler_params=_cparams(("arbitrary",), disable_bounds_checks=True),
        name="experts",
    )(tile_e, n_used, src_tok, dst_row, xn, wgt, w1, w3, w2)


def _add3_kernel(x_ref, ya_ref, yb_ref, o_ref):
    o_ref[...] = x_ref[...] + ya_ref[...] + yb_ref[...]


def _add3_call(x, y2, tm=640):
    nt = x.shape[0]
    nrow = nt // tm
    return pl.pallas_call(
        _add3_kernel,
        grid=(nrow,),
        in_specs=[pl.BlockSpec((tm, D_MODEL), lambda i: (i, 0)), pl.BlockSpec((tm, D_MODEL), lambda i: (i, 0)),
                  pl.BlockSpec((tm, D_MODEL), lambda i: (i + nrow, 0))],
        out_specs=pl.BlockSpec((tm, D_MODEL), lambda i: (i, 0)),
        out_shape=jax.ShapeDtypeStruct((nt, D_MODEL), f32),
        compiler_params=_cparams(("parallel",)),
        name="residual_add",
    )(x, y2, y2)


EXPERT_TM = 256
PROMPT_TQ = 256


def _bucket(n):
    max_exact = NUM_BUCKETS // 2
    nf = jnp.maximum(n, max_exact).astype(f32)
    large = max_exact + (jnp.log(nf / max_exact) / math.log(MAX_DISTANCE / max_exact)
                         * (NUM_BUCKETS - max_exact)).astype(jnp.int32)
    return jnp.where(n < max_exact, n, jnp.minimum(large, NUM_BUCKETS - 1))


def _bias_tables(rel_bias, tq, t, past, n_new):
    bd = (rel_bias[_bucket(jnp.arange(128))] - rel_bias[NUM_BUCKETS - 1][None, :]) * LOG2E

    def look(dist):
        oh = jax.nn.one_hot(jnp.clip(dist, 0, 127), 128, dtype=f32)
        return jnp.einsum('...d,dh->...h', oh, bd, precision=lax.Precision.HIGHEST)

    ntile = t // tq
    r = jnp.arange(tq)
    d0 = r[:, None] - r[None, :]
    t0 = jnp.where((d0 >= 0)[..., None], look(d0), NEG)
    t1 = look(tq + d0)
    tz = jnp.stack([t0, t1], axis=0).reshape(2, tq, tq, N_KV_HEADS, GQA)
    tz = jnp.transpose(tz, (3, 0, 2, 4, 1)).reshape(N_KV_HEADS, 2, tq, GQA * tq)
    n = jnp.arange(2 * (t // CMP_STRIDE))
    cbl = jnp.transpose(look(tq * (ntile - 1) + r[None, :] - CMP_STRIDE * n[:, None] - (CMP_BLOCK - 1)), (2, 0, 1))

    qq = jnp.arange(8)

    def look_t(dist, vis):
        tab = jnp.where(vis[:, None, :], jnp.transpose(look(dist), (0, 2, 1)), NEG)
        return tab.reshape(dist.shape[0], N_HEADS_A * 8)

    n_cmp_s = (past + n_new - CMP_BLOCK) // CMP_STRIDE + 1
    nchunk = past // CMP_STRIDE
    nn = jnp.arange(nchunk)[:, None]
    dist = past + qq[None, :] - (CMP_STRIDE * nn + CMP_BLOCK - 1)
    cbt = look_t(dist, (dist >= 0) & (nn < n_cmp_s))
    ck = CH_PAGES * PAGE_SIZE
    rr = jnp.arange(ck - 128, ck)[:, None]
    dist = qq[None, :] + ck - rr
    sblast = jnp.concatenate([jnp.zeros((ck - 128, 128), f32), look_t(dist, dist >= 0)], axis=0)
    r128 = jnp.arange(128)[:, None]
    dist = qq[None, :] - r128
    nbnew = look_t(dist, (dist >= 0) & (r128 < n_new))
    rw = jnp.arange(WINDOW)[:, None]
    dist = WINDOW + qq[None, :] - rw
    wbt = look_t(dist, (dist >= 0) & (dist <= WINDOW))
    return tz, cbl, cbt, sblast, nbnew, wbt


def _static_mats(t, past, n_new):
    n = np.arange(128)
    n_cmp = (t - CMP_BLOCK) // CMP_STRIDE + 1
    s = np.arange(128)
    ov = ((CMP_STRIDE * n[:, None] < SEL_BLOCK * s[None, :] + SEL_BLOCK)
          & (CMP_STRIDE * n[:, None] + CMP_BLOCK > SEL_BLOCK * s[None, :])
          & (n[:, None] < n_cmp) & (s[None, :] < t // SEL_BLOCK))
    ex = (np.arange(t)[None, :] // SEL_BLOCK == s[:, None])
    nsel_s = -(-(past + n_new) // SEL_BLOCK)
    nrow = -(-nsel_s // 8) * 8
    n_cmp_s = (past + n_new - CMP_BLOCK) // CMP_STRIDE + 1
    ss = np.arange(nrow)[:, None]
    ns = np.arange(past // CMP_STRIDE)[None, :]
    ovt = ((CMP_STRIDE * ns < SEL_BLOCK * ss + SEL_BLOCK) & (CMP_STRIDE * ns + CMP_BLOCK > SEL_BLOCK * ss)
           & (ns < n_cmp_s) & (ss < nsel_s))
    c = np.arange(128)
    smat = (c[:, None] // 32 == c[None, :] // 32) & (c[:, None] % 8 == c[None, :] % 8)
    ck = CH_PAGES * PAGE_SIZE
    e2 = (np.arange(ck)[:, None] // SEL_BLOCK == np.arange(128)[None, :])
    cvt = lambda a: jnp.asarray(a.astype(np.float32), dtype=bf16)
    return cvt(ov.T), cvt(ex.T), cvt(ovt), cvt(smat), cvt(e2)


def _route_metadata(route, nt, tm, ntiles):
    e_flat = jnp.concatenate([route[:, 0], route[:, 1]]).astype(jnp.int32)
    w_flat = jnp.concatenate([route[:, 2], route[:, 3]])
    npair = 2 * nt
    order = jnp.argsort(e_flat, stable=True).astype(jnp.int32)
    counts = jnp.sum(jax.nn.one_hot(e_flat, N_EXPERTS, dtype=jnp.int32), axis=0)
    tiles_e = (counts + tm - 1) // tm
    tend = jnp.cumsum(tiles_e)
    tstart = tend - tiles_e
    cstart = jnp.cumsum(counts) - counts
    n_used = tend[-1]
    tidx = jnp.arange(ntiles, dtype=jnp.int32)
    tile_e = jnp.sum((jnp.minimum(tidx, n_used - 1)[:, None] >= tend[None, :]).astype(jnp.int32), axis=1)
    tile_oh = jax.nn.one_hot(tile_e, N_EXPERTS, dtype=jnp.int32)
    t_cnt = jnp.sum(tile_oh * counts[None, :], axis=1)
    t_first = jnp.sum(tile_oh * (cstart - tstart * tm)[None, :], axis=1) + tidx * tm
    rows = jnp.arange(tm, dtype=jnp.int32)[None, :]
    rank = tidx[:, None] * tm + rows - jnp.sum(tile_oh * tstart[None, :], axis=1)[:, None] * tm
    valid = (rank < t_cnt[:, None]) & (tidx[:, None] < n_used)
    pair = order[jnp.clip(t_first[:, None] + rows, 0, npair - 1)]
    dump = npair + (tidx[:, None] % 2) * tm + rows
    src_tok = jnp.where(valid, pair % nt, 0).reshape(-1)
    dst_row = jnp.where(valid, pair, dump).reshape(-1)
    wgt = jnp.where(valid, w_flat[pair], 0.0).reshape(-1, 1)
    return tile_e.astype(jnp.int32), n_used.reshape(1).astype(jnp.int32), src_tok, dst_row, wgt


def _block_diag(blocks):
    n, r, c = blocks.shape
    return jnp.einsum('grc,gk->grkc', blocks, jnp.eye(n, dtype=blocks.dtype)).reshape(n * r, n * c)


def kernel(x_prompt, x_sample, cache_kv_cmp, cache_kv_sel, cache_kv_win, state_ssm_re, state_ssm_im, page_table, rel_bias, norm_mix, w_in, qk_norm, cmp_pos, cmp_w1, cmp_w2, ssm_a_re, ssm_a_im, ssm_log_dt, ssm_b_re, ssm_b_im, ssm_c_re, ssm_c_im, ssm_d, ssm_glu_w, ssm_glu_b, gmlp_norm, gmlp_ws, gmlp_bs, out_norm, w_out, norm_ffn, router_group_w, router_group_b, router_expert_w, router_expert_b, expert_w1, expert_w3, expert_w2):
    bp, t, _ = x_prompt.shape
    bs, s_new, _ = x_sample.shape
    npg = page_table.shape[1]
    past = npg * PAGE_SIZE
    nphys = cache_kv_cmp.shape[0]
    n_p, n_s = bp * t, bs * s_new
    nt = -(-(n_p + n_s) // 640) * 640
    tq = PROMPT_TQ
    assert t % tq == 0 and WINDOW % tq == 0 and s_new == 8 and bs == 8 and npg % CH_PAGES == 0

    x = jnp.concatenate([x_prompt.reshape(n_p, D_MODEL), x_sample.reshape(n_s, D_MODEL),
                         jnp.zeros((nt - n_p - n_s, D_MODEL), f32)], axis=0)
    y2 = None
    pool_cmp = cache_kv_cmp.reshape(nphys, DEPTH, PAGE_SIZE, 512)
    pool_sel = cache_kv_sel.reshape(nphys, DEPTH, PAGE_SIZE, 512)
    wcache = cache_kv_win.reshape(bs, DEPTH, WINDOW, 512)
    pt_flat = page_table.reshape(-1).astype(jnp.int32)
    pt_ident = jnp.arange(bp * (t // PAGE_SIZE), dtype=jnp.int32)

    tz, cbl, cbt, sblast, nbnew, wbt = _bias_tables(rel_bias, tq, t, past, s_new)
    ov, ex, ovt, smat, e2 = _static_mats(t, past, s_new)
    g64 = _block_diag(jnp.ones((8, 64, 64), bf16))
    g128 = _block_diag(jnp.ones((4, 128, 128), bf16))
    ones64 = jnp.ones((HEAD_DIM,), f32)
    tril = jnp.tril(jnp.ones((CHUNK, CHUNK), f32))
    zeros_state = jnp.zeros((8, SSM_GROUPS * SSM_STATE), f32)
    ntiles = 2 * nt // EXPERT_TM + N_EXPERTS
    all_rows = lambda p, s: jnp.concatenate([p, s, jnp.zeros((nt - n_p - n_s, p.shape[1]), p.dtype)], axis=0)

    outs = {k: [] for k in ('pc', 'ps', 'pw', 'pr', 'pi', 'pv', 'sc', 'ss', 'sw', 'sr', 'si', 'sv')}
    for l in range(DEPTH):
        wl = w_in[l]
        o1, o2, o3 = ATTN_WIDTH, ATTN_WIDTH + 6 * KV_WIDTH, ATTN_WIDTH + 6 * KV_WIDTH + 3 * N_HEADS_A
        w_pad = jnp.concatenate([wl[:, :o2], wl[:, o3:], wl[:, o2:o3],
                                 jnp.zeros((D_MODEL, 128 - 3 * N_HEADS_A), f32)], axis=1).astype(bf16)
        gq = jnp.tile(qk_norm[l, 0], N_HEADS_A).reshape(1, 1024)
        gsel = jnp.tile(jnp.concatenate([qk_norm[l, 2], ones64]), N_KV_HEADS).reshape(1, 512)
        gwin = jnp.tile(jnp.concatenate([qk_norm[l, 3], ones64]), N_KV_HEADS).reshape(1, 512)
        gk = jnp.concatenate([qk_norm[l, 1], ones64]).reshape(1, 128)
        w1k = cmp_w1[l, 0].reshape(CMP_BLOCK, HEAD_DIM, HEAD_DIM)
        w1v = cmp_w1[l, 1].reshape(CMP_BLOCK, HEAD_DIM, HEAD_DIM)
        zz = jnp.zeros_like(w1k)
        wfull = jnp.concatenate([jnp.concatenate([w1k, zz], axis=2), jnp.concatenate([zz, w1v], axis=2)], axis=1)
        wab = jnp.concatenate([wfull[:16].reshape(2048, 128), wfull[16:].reshape(2048, 128)], axis=1).astype(bf16)
        posf = jnp.concatenate([cmp_pos[l, 0], cmp_pos[l, 1]], axis=1)
        pos8 = jnp.broadcast_to(jnp.concatenate([posf[:16].reshape(1, 2048), posf[16:].reshape(1, 2048)], axis=1),
                                (8, 4096))
        w2bd = _block_diag(cmp_w2[l]).astype(bf16)

        dt = jnp.exp(ssm_log_dt[l])[:, None]
        a_re, a_im = ssm_a_re[l], ssm_a_im[l]
        mag = jnp.exp(dt * a_re)
        ab_re, ab_im = mag * jnp.cos(dt * a_im), mag * jnp.sin(dt * a_im)
        den = a_re * a_re + a_im * a_im
        f_re = ((ab_re - 1.0) * a_re + ab_im * a_im) / den
        f_im = (ab_im * a_re - (ab_re - 1.0) * a_im) / den
        bb_re = f_re[..., None] * ssm_b_re[l] - f_im[..., None] * ssm_b_im[l]
        bb_im = f_re[..., None] * ssm_b_im[l] + f_im[..., None] * ssm_b_re[l]
        abr = jnp.broadcast_to(ab_re.reshape(1, -1), (8, SSM_GROUPS * SSM_STATE))
        abi = jnp.broadcast_to(ab_im.reshape(1, -1), (8, SSM_GROUPS * SSM_STATE))
        eye8 = jnp.eye(8, dtype=f32)

        def in_blocks(bb):
            xx = jnp.transpose(bb, (0, 2, 1)).reshape(4, 8, SSM_GROUP, SSM_STATE)
            return jnp.einsum('agcn,gk->agckn', xx, eye8).reshape(4, 128, 512).astype(bf16)

        def out_blocks(cc):
            yy = jnp.transpose(cc, (0, 2, 1)).reshape(4, 8, SSM_STATE, SSM_GROUP)
            return jnp.einsum('agnc,gk->agnkc', yy, eye8).reshape(4, 512, 128).astype(bf16)

        s5p = (abr, abi, in_blocks(bb_re), in_blocks(bb_im), out_blocks(ssm_c_re[l]), out_blocks(ssm_c_im[l]),
               ssm_d[l].reshape(1, 512), ssm_glu_w[l].astype(bf16), ssm_glu_b[l].reshape(1, 512))
        gw = (gmlp_ws[l] * tril).astype(bf16)
        gbs = jnp.concatenate([gmlp_bs[l].T, jnp.zeros((CHUNK, 128 - GMLP_GROUPS), f32)], axis=1)
        rw = jnp.concatenate([router_group_w[l], jnp.transpose(router_expert_w[l], (1, 0, 2)).reshape(D_MODEL, N_EXPERTS),
                              jnp.zeros((D_MODEL, 128 - MOE_GROUPS - N_EXPERTS), f32)], axis=1)
        rwh = rw.astype(bf16)
        rwl = (rw - rwh.astype(f32)).astype(bf16)
        rb = jnp.concatenate([router_group_b[l], router_expert_b[l].reshape(-1),
                              jnp.zeros((128 - MOE_GROUPS - N_EXPERTS,), f32)]).reshape(1, 128)

        proj = _proj_call(x, y2, norm_mix[l].reshape(1, D_MODEL), w_pad)
        qn, ksel, kwin, gates, uga, vn = _post_call(proj, gq, gsel, gwin, gmlp_norm[l].reshape(1, 512), g64, g128)

        src_p = proj.reshape(nt // PAGE_SIZE, 1, PAGE_SIZE, PROJ_W)
        kvc_p = _cmp_call(pt_ident, src_p, 0, ATTN_WIDTH // 128, bp, t // PAGE_SIZE, wab, pos8, w2bd, gk)
        kvc_s = _cmp_call(pt_flat, pool_cmp, l, 0, bs, npg, wab, pos8, w2bd, gk)
        oa_p = _pattn_call(qn, kvc_p, ksel, kwin, gates, cbl, tz, ov, ex, bp, t, tq)

        qs = qn[n_p:n_p + n_s].reshape(bs, s_new, N_KV_HEADS, GQA, HEAD_DIM)
        qa = jnp.transpose(qs, (0, 2, 4, 3, 1)).reshape(bs, N_KV_HEADS, HEAD_DIM, GQA * s_new)
        qbig = jnp.einsum('bhdc,hk->bhdkc', qa, jnp.eye(N_KV_HEADS, dtype=bf16))
        qbig = jnp.pad(qbig, ((0, 0), (0, 0), (0, 64), (0, 0), (0, 0))).reshape(bs, 512, 128)
        gs = gates[n_p:n_p + n_s, :48].reshape(bs, s_new, 3, N_KV_HEADS, GQA)
        gt = jnp.pad(jnp.transpose(gs, (0, 3, 4, 1, 2)).reshape(bs, 128, 3), ((0, 0), (0, 0), (0, 125)))
        new_sel = jnp.pad(ksel[n_p:n_p + n_s].reshape(bs, s_new, 512), ((0, 0), (0, 128 - s_new), (0, 0)))
        new_win = jnp.pad(kwin[n_p:n_p + n_s].reshape(bs, s_new, 512), ((0, 0), (0, 128 - s_new), (0, 0)))
        osmp = _sattn_call(pt_flat, qbig, kvc_s, cbt, ovt, smat, e2, pool_sel, sblast, new_sel, nbnew,
                           wcache, wbt, new_win, nbnew, gt, l, bs, npg, s_new)
        oa_s = jnp.transpose(osmp[:, :, 64:].reshape(bs, N_KV_HEADS, GQA, s_new, HEAD_DIM),
                             (0, 3, 1, 2, 4)).reshape(n_s, ATTN_WIDTH)
        oa = all_rows(oa_p, oa_s)

        ob_p, hr_p, hi_p = _s5_call(proj, 0, bp, t, 256, zeros_state, zeros_state, *s5p)
        ob_s, hr_s, hi_s = _s5_call(proj, n_p, bs, s_new, s_new, state_ssm_re[:, l].reshape(bs, -1),
                                    state_ssm_im[:, l].reshape(bs, -1), *s5p)
        ob = all_rows(ob_p.reshape(n_p, 512), ob_s.reshape(n_s, 512))

        oc_p = _gmlp_call(uga, vn, gw, gbs, 0, n_p, CHUNK)
        oc_s = _gmlp_call(uga, vn, gw, gbs, n_p, n_s, s_new)
        oc = all_rows(oc_p, oc_s)

        x1, xn2, route = _merge_call(x, y2, oa, ob, oc, out_norm[l].reshape(1, -1), w_out[l].astype(bf16),
                                     norm_ffn[l].reshape(1, -1), rwh, rwl, rb)
        tile_e, n_used, src_tok, dst_row, wgt = _route_metadata(route, nt, EXPERT_TM, ntiles)
        y2 = _expert_call(tile_e, n_used, src_tok, dst_row, xn2, wgt, expert_w1, expert_w3, expert_w2, l,
                          EXPERT_TM, ntiles, 2 * nt + 2 * EXPERT_TM)
        x = x1

        kvshape = (N_KV_HEADS, 2, HEAD_DIM)
        cmp_rows = proj[:, ATTN_WIDTH:ATTN_WIDTH + 512]
        outs['pc'].append(cmp_rows[:n_p].reshape(bp, t, *kvshape))
        outs['ps'].append(ksel[:n_p].reshape(bp, t, *kvshape))
        outs['pw'].append(kwin[:n_p].reshape(bp, t, *kvshape)[:, t - min(WINDOW, t):])
        outs['pr'].append(hr_p[:bp].reshape(bp, SSM_GROUPS, SSM_STATE))
        outs['pi'].append(hi_p[:bp].reshape(bp, SSM_GROUPS, SSM_STATE))
        outs['pv'].append(vn[:n_p].reshape(bp, t, GMLP_WIDTH)[:, (t - 1) // CHUNK * CHUNK:])
        outs['sc'].append(cmp_rows[n_p:n_p + n_s].reshape(bs, s_new, *kvshape))
        outs['ss'].append(ksel[n_p:n_p + n_s].reshape(bs, s_new, *kvshape))
        win_new = kwin[n_p:n_p + n_s].reshape(bs, s_new, *kvshape)
        outs['sw'].append(jnp.concatenate([cache_kv_win[:, l], win_new], axis=1)[:, s_new:])
        outs['sr'].append(hr_s[:bs].reshape(bs, SSM_GROUPS, SSM_STATE))
        outs['si'].append(hi_s[:bs].reshape(bs, SSM_GROUPS, SSM_STATE))
        outs['sv'].append(vn[n_p:n_p + n_s].reshape(bs, s_new, GMLP_WIDTH))

    xf = _add3_call(x, y2)
    st = {k: jnp.stack(v, axis=1) for k, v in outs.items()}
    return (xf[:n_p].reshape(bp, t, D_MODEL), xf[n_p:n_p + n_s].reshape(bs, s_new, D_MODEL),
            st['pc'], st['ps'], st['pw'], st['pr'], st['pi'], st['pv'],
            st['sc'], st['ss'], st['sw'], st['sr'], st['si'], st['sv'])
```

```python
import functools
import math

import numpy as np
import jax
import jax.numpy as jnp
from jax import lax
from jax.experimental import pallas as pl
from jax.experimental.pallas import tpu as pltpu

f32 = jnp.float32
bf16 = jnp.bfloat16

D_MODEL = 2048
DEPTH = 4
PAGE_SIZE = 128
HEAD_DIM = 64
ATTN_WIDTH = 1024
N_HEADS_A = 16
N_KV_HEADS = 4
GQA = 4
KV_WIDTH = 256
CMP_BLOCK = 32
CMP_STRIDE = 16
SEL_BLOCK = 64
SEL_TOPK = 16
WINDOW = 512
NUM_BUCKETS = 32
MAX_DISTANCE = 128
SSM_WIDTH = 512
SSM_GROUP = 16
SSM_GROUPS = 32
SSM_STATE = 64
GMLP_WIDTH = 512
GMLP_GROUPS = 4
CHUNK = 128
MOE_GROUPS = 4
EXPERTS_PER_GROUP = 4
N_EXPERTS = 16
D_EXPERT = 512
SCALE = HEAD_DIM ** -0.5
LOG2E = math.log2(math.e)
EPS = 1e-6
NEG = -1e30

PROJ_W = 4224
LANES = 128
VMEM_LIMIT = 56 * 1024 * 1024


def _cparams(sem, **kw):
    return pltpu.CompilerParams(dimension_semantics=sem, vmem_limit_bytes=VMEM_LIMIT, **kw)


def _dot(a, b):
    return jnp.dot(a, b, preferred_element_type=f32)


def _dot_nt(a, b):
    return lax.dot_general(a, b, (((1,), (1,)), ((), ())), preferred_element_type=f32)


def _split3(x):
    hi = x.astype(bf16)
    r = x - hi.astype(f32)
    mid = r.astype(bf16)
    lo = (r - mid.astype(f32)).astype(bf16)
    return hi, mid, lo


def _gelu(x):
    return 0.5 * x * (1.0 + jnp.tanh(math.sqrt(2.0 / math.pi) * (x + 0.044715 * (x * x * x))))


def _sigmoid(x):
    return 1.0 / (1.0 + jnp.exp(-x))


def _group_mean_sq(x, gmat, group):
    x2 = x * x
    hi = x2.astype(bf16)
    lo = (x2 - hi.astype(f32)).astype(bf16)
    return (_dot(hi, gmat) + _dot(lo, gmat)) * (1.0 / group)


def _residual(refs, has_y):
    if has_y:
        return refs[0][...] + refs[1][...] + refs[2][...]
    return refs[0][...]


def _residual_specs(x, y2, tm, index_of):
    nrow = x.shape[0] // tm
    specs = [pl.BlockSpec((tm, D_MODEL), lambda *ids: (index_of(*ids), 0))]
    args = [x]
    if y2 is not None:
        specs += [pl.BlockSpec((tm, D_MODEL), lambda *ids: (index_of(*ids), 0)),
                  pl.BlockSpec((tm, D_MODEL), lambda *ids: (index_of(*ids) + nrow, 0))]
        args += [y2, y2]
    return specs, args


def _proj_kernel(*refs, has_y):
    nres = 3 if has_y else 1
    g_ref, w_ref, proj_ref, xn_sc = refs[nres:]

    @pl.when(pl.program_id(1) == 0)
    def _():
        x = _residual(refs, has_y)
        ms = jnp.mean(x * x, axis=-1, keepdims=True)
        xn_sc[...] = (x * lax.rsqrt(ms + EPS) * g_ref[...]).astype(bf16)

    proj_ref[...] = _dot(xn_sc[...], w_ref[...])


def _proj_call(x, y2, g, w, tm=320, tn=1408):
    nt = x.shape[0]
    res_specs, res_args = _residual_specs(x, y2, tm, lambda i, j: i)
    return pl.pallas_call(
        functools.partial(_proj_kernel, has_y=y2 is not None),
        grid=(nt // tm, PROJ_W // tn),
        in_specs=res_specs + [pl.BlockSpec((1, D_MODEL), lambda i, j: (0, 0)),
                              pl.BlockSpec((D_MODEL, tn), lambda i, j: (0, j))],
        out_specs=pl.BlockSpec((tm, tn), lambda i, j: (i, j)),
        out_shape=jax.ShapeDtypeStruct((nt, PROJ_W), f32),
        scratch_shapes=[pltpu.VMEM((tm, D_MODEL), bf16)],
        compiler_params=_cparams(("parallel", "arbitrary")),
        name="proj",
    )(*res_args, g, w)


def _post_kernel(q_ref, sel_ref, win_ref, ug_ref, vg_ref, gt_ref, gq_ref, gsel_ref, gwin_ref, gv_ref,
                 g64_ref, g128_ref, qn_ref, ksel_ref, kwin_ref, gate_ref, uga_ref, vn_ref):
    g64 = g64_ref[...]
    lane = lax.broadcasted_iota(jnp.int32, (1, 512), 1)
    is_k = (lane % 128) < 64
    for half in range(2):
        q = q_ref[:, half * 512:(half + 1) * 512]
        ms = _group_mean_sq(q, g64, 64)
        qn = q * lax.rsqrt(ms + EPS) * gq_ref[:, half * 512:(half + 1) * 512]
        qn_ref[:, half * 512:(half + 1) * 512] = (qn * (SCALE * LOG2E)).astype(bf16)
    for src, gref, dst in ((sel_ref, gsel_ref, ksel_ref), (win_ref, gwin_ref, kwin_ref)):
        x = src[...]
        ms = _group_mean_sq(x, g64, 64)
        dst[...] = jnp.where(is_k, x * lax.rsqrt(ms + EPS) * gref[...], x)
    gate_ref[...] = _sigmoid(gt_ref[...])
    uga_ref[...] = _gelu(ug_ref[...])
    v = _gelu(vg_ref[...])
    ms = _group_mean_sq(v, g128_ref[...], 128)
    vn_ref[...] = v * lax.rsqrt(ms + EPS) * gv_ref[...]


def _post_call(proj, gq, gsel, gwin, gv, g64, g128, tm=320):
    nt = proj.shape[0]

    def col(width, idx):
        return pl.BlockSpec((tm, width), lambda i: (i, idx))

    def full(shape):
        return pl.BlockSpec(shape, lambda i: (0,) * len(shape))

    def out(width):
        return pl.BlockSpec((tm, width), lambda i: (i, 0))

    return pl.pallas_call(
        _post_kernel,
        grid=(nt // tm,),
        in_specs=[col(1024, 0), col(512, 3), col(512, 4), col(512, 6), col(512, 7), col(128, 32),
                  full((1, 1024)), full((1, 512)), full((1, 512)), full((1, 512)),
                  full((512, 512)), full((512, 512))],
        out_specs=[out(1024), out(512), out(512), out(128), out(512), out(512)],
        out_shape=[jax.ShapeDtypeStruct((nt, 1024), bf16), jax.ShapeDtypeStruct((nt, 512), f32),
                   jax.ShapeDtypeStruct((nt, 512), f32), jax.ShapeDtypeStruct((nt, 128), f32),
                   jax.ShapeDtypeStruct((nt, 512), f32), jax.ShapeDtypeStruct((nt, 512), f32)],
        compiler_params=_cparams(("parallel",)),
        name="post",
    )(proj, proj, proj, proj, proj, proj, gq, gsel, gwin, gv, g64, g128)


CMP_PAGES_PER_STEP = 32


def _cmp_kernel(pt_ref, *refs, pps, nsteps, nchunk):
    srcs = refs[:pps]
    wab_ref, pos_ref, w2_ref, gk_ref, out_ref, c_sc = refs[pps:]
    s = pl.program_id(2)
    for pp in range(pps // 2):
        row0 = pl.multiple_of(s * (pps * 8) + pp * 16, 16)
        for j in range(CMP_STRIDE):
            a = srcs[2 * pp][pl.ds(j, 8, stride=CMP_STRIDE), :]
            b = srcs[2 * pp + 1][pl.ds(j, 8, stride=CMP_STRIDE), :]
            c_sc[pl.ds(row0, 16), j * 128:(j + 1) * 128] = jnp.concatenate([a, b], axis=0).astype(bf16)

    @pl.when(s == nsteps - 1)
    def _():
        wab = wab_ref[...]
        pq = _dot(c_sc[...], wab)
        pos = pos_ref[...].astype(bf16)
        pterm = _dot(pos[:, :2048], wab[:, :128]) + _dot(pos[:, 2048:], wab[:, 128:])
        pre = pq[:, :128] + pltpu.roll(pq[:, 128:], nchunk - 1, 0) + pterm[0:1, :]
        y = _dot(_gelu(pre).astype(bf16), w2_ref[...])
        lane = lax.broadcasted_iota(jnp.int32, (1, 128), 1)
        is_k = lane < 64
        ms = jnp.sum(jnp.where(is_k, y * y, 0.0), axis=-1, keepdims=True) * (1.0 / 64)
        out_ref[...] = jnp.where(is_k, y * lax.rsqrt(ms + EPS) * gk_ref[...], y)


def _cmp_call(pt_flat, src4, layer, col0, nb, npg, wab, pos, w2, gk):
    nchunk = npg * 8
    pps = min(CMP_PAGES_PER_STEP, npg)
    nsteps = npg // pps

    def src_spec(k):
        return pl.BlockSpec((None, None, PAGE_SIZE, 128),
                            lambda b, h, s, pt: (pt[b * npg + s * pps + k], layer, 0, col0 + h))

    def full(shape):
        return pl.BlockSpec(shape, lambda b, h, s, pt: (0,) * len(shape))

    return pl.pallas_call(
        functools.partial(_cmp_kernel, pps=pps, nsteps=nsteps, nchunk=nchunk),
        grid_spec=pltpu.PrefetchScalarGridSpec(
            num_scalar_prefetch=1,
            grid=(nb, N_KV_HEADS, nsteps),
            in_specs=[src_spec(k) for k in range(pps)]
                     + [full((2048, 256)), full((8, 4096)), full((128, 128)), full((1, 128))],
            out_specs=pl.BlockSpec((None, nchunk, 128), lambda b, h, s, pt: (b, 0, h)),
            scratch_shapes=[pltpu.VMEM((nchunk, 2048), bf16)],
        ),
        out_shape=jax.ShapeDtypeStruct((nb, nchunk, 512), f32),
        compiler_params=_cparams(("parallel", "parallel", "arbitrary")),
        name="compress",
    )(pt_flat, *([src4] * pps), wab, pos, w2, gk)


def _pattn_kernel(q_ref, kvc_ref, ksel_ref, kwin_ref, gate_ref, cbt_ref, tzt_ref, ovt_ref, ext_ref, out_ref,
                  kb_sel, vat_sel, kb_win, vat_win, kcb, vcat, sb_sc, gt_sc, m_sc, acc_sel, acc_win, *, tq, t):
    h = pl.program_id(1)
    i = pl.program_id(2)
    c4 = GQA * tq
    nsel = t // SEL_BLOCK
    ntile = t // tq
    nwin = WINDOW // tq
    ncc = t // CMP_STRIDE
    ksel = min(SEL_TOPK, nsel)

    @pl.when(i == 0)
    def _():
        lane = lax.broadcasted_iota(jnp.int32, (1, 128), 1)
        for src, kb, vat in ((ksel_ref, kb_sel, vat_sel), (kwin_ref, kb_win, vat_win)):
            blk = src[...]
            kb[...] = blk[:, :64].astype(bf16)
            va = jnp.where(lane < 64, pltpu.roll(blk, 64, 1), 1.0)
            for jj in range(ntile):
                vat[jj] = jnp.transpose(va[jj * tq:(jj + 1) * tq]).astype(bf16)
        kvc = kvc_ref[...]
        kcb[...] = kvc[:, :64].astype(bf16)
        vcat[...] = jnp.transpose(jnp.where(lane < 64, pltpu.roll(kvc, 64, 1), 0.0)).astype(bf16)

    qt = jnp.transpose(q_ref[...].astype(f32))
    q4t = jnp.concatenate([qt[g * 64:(g + 1) * 64] for g in range(GQA)], axis=1).astype(bf16)
    gt_sc[...] = jnp.transpose(gate_ref[...])

    def lanes4(a):
        return jnp.concatenate([a] * GQA, axis=1)

    nrow = lax.broadcasted_iota(jnp.int32, (ncc, tq), 0)
    qpos = i * tq + lax.broadcasted_iota(jnp.int32, (ncc, tq), 1)
    cvis = lanes4((CMP_STRIDE * nrow + CMP_BLOCK - 1 <= qpos) & (nrow < (t - CMP_BLOCK) // CMP_STRIDE + 1))
    shift = pl.multiple_of((tq // CMP_STRIDE) * (ntile - 1 - i), tq // CMP_STRIDE)
    cb = jnp.concatenate([cbt_ref[g, pl.ds(shift, ncc), :] for g in range(GQA)], axis=1)
    st = jnp.where(cvis, _dot(kcb[...], q4t) + cb, NEG)
    m = jnp.max(st, axis=0, keepdims=True)
    e = jnp.where(cvis, jnp.exp2(st - m), 0.0)
    p = e / jnp.maximum(jnp.sum(e, axis=0, keepdims=True), 1e-30)
    o_cmp = _dot(vcat[...], p.astype(bf16))
    pg = p[:, 0:tq] + p[:, tq:2 * tq] + p[:, 2 * tq:3 * tq] + p[:, 3 * tq:4 * tq]
    ovt = ovt_ref[...]
    hi, mid, lo = _split3(pg)
    score = (_dot(ovt, hi) + _dot(ovt, mid) + _dot(ovt, lo))[0:nsel]

    blk = lax.broadcasted_iota(jnp.int32, (nsel, tq), 0)
    qp = i * tq + lax.broadcasted_iota(jnp.int32, (nsel, tq), 1)
    cur = qp // SEL_BLOCK
    forced = (blk == 0) | (blk == cur) | (blk == cur - 1)
    score = jnp.where(forced, 1e6, jnp.where(blk * SEL_BLOCK > qp, -1.0, score))
    rank = jnp.zeros((nsel, tq), f32)
    for sp in range(nsel):
        other = score[sp:sp + 1, :]
        ahead = (other > score) | ((other == score) & (blk > sp))
        rank = rank + jnp.where(ahead, 1.0, 0.0)
    selneg = jnp.where(rank < ksel, 0.0, NEG)
    selneg = jnp.concatenate([selneg, jnp.zeros((128 - nsel, tq), f32)], axis=0).astype(bf16)
    for jj in range(ntile):
        sb_sc[jj] = _dot(ext_ref[jj * tq:(jj + 1) * tq, :], selneg)

    def flash(kb, vat, acc, tiles):
        sts = []
        for j, bias in tiles:
            k0 = pl.multiple_of(j * tq, tq)
            st = _dot(kb[pl.ds(k0, tq), :], q4t)
            sts.append(st if bias is None else st + bias)
        m_old = m_sc[...]
        m_new = m_old
        for st in sts:
            m_new = jnp.maximum(m_new, jnp.max(st, axis=0, keepdims=True))
        upd = acc[...] * jnp.exp2(m_old - m_new)
        for (j, _), st in zip(tiles, sts):
            upd = upd + _dot(vat[j], jnp.exp2(st - m_new).astype(bf16))
        acc[...] = upd
        m_sc[...] = m_new

    def sel_mask(j):
        return lanes4(sb_sc[j])

    m_sc[...] = jnp.full((1, c4), NEG, f32)
    acc_sel[...] = jnp.zeros((128, c4), f32)
    nfar = jnp.maximum(i - 1, 0)

    def sel_far(p, c):
        flash(kb_sel, vat_sel, acc_sel, [(2 * p, sel_mask(2 * p)), (2 * p + 1, sel_mask(2 * p + 1))])
        return c

    lax.fori_loop(0, nfar // 2, sel_far, 0)

    @pl.when(nfar % 2 == 1)
    def _():
        flash(kb_sel, vat_sel, acc_sel, [(nfar - 1, sel_mask(nfar - 1))])

    @pl.when(i >= 1)
    def _():
        flash(kb_sel, vat_sel, acc_sel, [(i - 1, tzt_ref[1] + sel_mask(i - 1)), (i, tzt_ref[0] + sel_mask(i))])

    @pl.when(i == 0)
    def _():
        flash(kb_sel, vat_sel, acc_sel, [(0, tzt_ref[0] + sel_mask(0))])

    m_sc[...] = jnp.full((1, c4), NEG, f32)
    acc_win[...] = jnp.zeros((128, c4), f32)

    def win_far(j, c):
        flash(kb_win, vat_win, acc_win, [(j, None)])
        return c

    lax.fori_loop(jnp.maximum(i - nwin + 1, 0), jnp.maximum(i - 1, 0), win_far, 0)

    @pl.when(i >= nwin)
    def _():
        krow = lax.broadcasted_iota(jnp.int32, (tq, tq), 0)
        qcol = lax.broadcasted_iota(jnp.int32, (tq, tq), 1)
        edge = lanes4(jnp.where(qcol <= krow, 0.0, NEG))
        flash(kb_win, vat_win, acc_win, [(i - nwin, edge), (i - 1, tzt_ref[1]), (i, tzt_ref[0])])

    @pl.when((i >= 1) & (i < nwin))
    def _():
        flash(kb_win, vat_win, acc_win, [(i - 1, tzt_ref[1]), (i, tzt_ref[0])])

    @pl.when(i == 0)
    def _():
        flash(kb_win, vat_win, acc_win, [(0, tzt_ref[0])])

    a_sel = acc_sel[...]
    a_win = acc_win[...]
    o_sel = a_sel[0:64] / a_sel[64:65]
    o_win = a_win[0:64] / a_win[64:65]

    def grow(branch):
        return jnp.concatenate([gt_sc[pl.ds(branch * N_HEADS_A + h * GQA + g, 1), :] for g in range(GQA)], axis=1)

    comb = grow(0) * o_cmp[0:64] + grow(1) * o_sel + grow(2) * o_win
    out_ref[...] = jnp.concatenate([jnp.transpose(comb[:, g * tq:(g + 1) * tq]) for g in range(GQA)], axis=1)


def _pattn_call(qn, kvc, ksel, kwin, gates, cbt, tzt, ovt, ext, nb, t, tq=256):
    ntile = t // tq
    c4 = GQA * tq
    ncc = t // CMP_STRIDE
    grid = (nb, N_KV_HEADS, ntile)
    return pl.pallas_call(
        functools.partial(_pattn_kernel, tq=tq, t=t),
        grid=grid,
        in_specs=[
            pl.BlockSpec((tq, 256), lambda b, h, i: (b * ntile + i, h)),
            pl.BlockSpec((None, ncc, 128), lambda b, h, i: (b, 0, h)),
            pl.BlockSpec((t, 128), lambda b, h, i: (b, h)),
            pl.BlockSpec((t, 128), lambda b, h, i: (b, h)),
            pl.BlockSpec((tq, 128), lambda b, h, i: (b * ntile + i, 0)),
            pl.BlockSpec((GQA, 2 * ncc, tq), lambda b, h, i: (h, 0, 0)),
            pl.BlockSpec((None, 2, tq, c4), lambda b, h, i: (h, 0, 0, 0)),
            pl.BlockSpec((128, ncc), lambda b, h, i: (0, 0)),
            pl.BlockSpec((t, 128), lambda b, h, i: (0, 0)),
        ],
        out_specs=pl.BlockSpec((tq, 256), lambda b, h, i: (b * ntile + i, h)),
        out_shape=jax.ShapeDtypeStruct((nb * t, ATTN_WIDTH), f32),
        scratch_shapes=[
            pltpu.VMEM((t, 64), bf16), pltpu.VMEM((ntile, 128, tq), bf16),
            pltpu.VMEM((t, 64), bf16), pltpu.VMEM((ntile, 128, tq), bf16),
            pltpu.VMEM((ncc, 64), bf16), pltpu.VMEM((128, ncc), bf16),
            pltpu.VMEM((ntile, tq, tq), f32), pltpu.VMEM((128, tq), f32),
            pltpu.VMEM((1, c4), f32), pltpu.VMEM((128, c4), f32), pltpu.VMEM((128, c4), f32),
        ],
        compiler_params=_cparams(("parallel", "parallel", "arbitrary")),
        name="prompt_attn",
    )(qn, kvc, ksel, kwin, gates, cbt, tzt, ovt, ext)


CH_PAGES = 16


def _topk_neg_rows(score, rowf, k):
    neg = jnp.full(score.shape, NEG, f32)
    work = score
    for _ in range(k):
        m = jnp.max(work, axis=0, keepdims=True)
        idx = jnp.min(jnp.where(work == m, rowf, 1e9), axis=0, keepdims=True)
        hit = rowf == idx
        neg = jnp.where(hit, 0.0, neg)
        work = jnp.where(hit, -3e38, work)
    return neg


def _col_of(vec):
    return jnp.transpose(jnp.broadcast_to(vec, (128, 128)))[:, :1]


def _sattn_kernel(pt_ref, qbig_ref, kvc_ref, cbt_ref, ovt_ref, smat_ref, e2_ref, pool_ref, sblast_ref,
                  nsel_ref, nbsel_ref, wcache_ref, wbt_ref, nwin_ref, nbwin_ref, gt_ref, out_ref,
                  buf, sem, seln_sc, m_sc, l_sc, acc_sc, *, layer, npg, n_new):
    b = pl.program_id(0)
    nch = npg // CH_PAGES
    ck = CH_PAGES * PAGE_SIZE
    past = npg * PAGE_SIZE
    qbig = qbig_ref[...]

    def page_copy(c, k, slot):
        pg = pt_ref[b * npg + c * CH_PAGES + k]
        return pltpu.make_async_copy(pool_ref.at[pg, layer], buf.at[slot, k], sem.at[slot])

    def start_chunk(c, slot):
        for k in range(CH_PAGES):
            page_copy(c, k, slot).start()

    def wait_chunk(c, slot):
        for k in range(CH_PAGES):
            page_copy(c, k, slot).wait()

    start_chunk(0, 0)
    if nch > 1:
        start_chunk(1, 1)

    def reset():
        m_sc[...] = jnp.full((1, 128), NEG, f32)
        l_sc[...] = jnp.zeros((1, 128), f32)
        acc_sc[...] = jnp.zeros((128, 512), f32)

    def attend(rows, bias_t):
        rb = rows.astype(bf16)
        st = _dot(rb, qbig) + bias_t
        m_old = m_sc[...]
        m_new = jnp.maximum(m_old, jnp.max(st, axis=0, keepdims=True))
        alpha = jnp.exp2(m_old - m_new)
        pt = jnp.exp2(st - m_new)
        l_sc[...] = l_sc[...] * alpha + jnp.sum(pt, axis=0, keepdims=True)
        acc_sc[...] = acc_sc[...] * _col_of(alpha) + _dot(jnp.transpose(pt).astype(bf16), rb)
        m_sc[...] = m_new

    rowhead = lax.broadcasted_iota(jnp.int32, (128, 128), 0) // 32

    def own_head(res):
        out = jnp.zeros((128, 128), f32)
        for hh in range(N_KV_HEADS):
            out = jnp.where(rowhead == hh, res[:, hh * 128:(hh + 1) * 128], out)
        return out

    kvc = kvc_ref[...]
    kb = kvc.astype(bf16)
    st = _dot(kb, qbig) + cbt_ref[...]
    m = jnp.max(st, axis=0, keepdims=True)
    e = jnp.exp2(st - m)
    pt = e / jnp.maximum(jnp.sum(e, axis=0, keepdims=True), 1e-30)
    o_cmp = own_head(_dot(jnp.transpose(pt).astype(bf16), kb))
    smat = smat_ref[...]
    hi, mid, lo = _split3(pt)
    pg = _dot(hi, smat) + _dot(mid, smat) + _dot(lo, smat)
    hi, mid, lo = _split3(pg)
    ovt = ovt_ref[...]
    score = _dot(ovt, hi) + _dot(ovt, mid) + _dot(ovt, lo)
    nrow = score.shape[0]
    blk = lax.broadcasted_iota(jnp.int32, (nrow, 128), 0)
    qpos = past + lax.broadcasted_iota(jnp.int32, (nrow, 128), 1) % 8
    cur = qpos // SEL_BLOCK
    forced = (blk == 0) | (blk == cur) | (blk == cur - 1)
    future = blk * SEL_BLOCK > qpos
    score = jnp.where(forced, 1e6, jnp.where(future, -1.0, score))
    nsel = -(-(past + n_new) // SEL_BLOCK)
    score = jnp.where(blk < nsel, score, -2.0)
    seln_sc[0:nrow, :] = _topk_neg_rows(score, blk.astype(f32), min(SEL_TOPK, nsel))
    seln_sc[nrow:, :] = jnp.zeros((seln_sc.shape[0] - nrow, 128), f32)

    reset()
    bpc = ck // SEL_BLOCK

    def sel_bias(c):
        r0 = pl.multiple_of(c * bpc, bpc)
        return _dot(e2_ref[...], seln_sc[pl.ds(r0, 128), :].astype(bf16))

    def chunk_body(c, carry):
        slot = c % 2
        wait_chunk(c, slot)
        attend(buf[slot].reshape(ck, 512), sel_bias(c))

        @pl.when(c + 2 < nch)
        def _():
            start_chunk(c + 2, slot)

        return carry

    lax.fori_loop(0, nch - 1, chunk_body, 0)
    last = nch - 1
    wait_chunk(last, last % 2)
    attend(buf[last % 2].reshape(ck, 512), sel_bias(last) + sblast_ref[...])
    attend(nsel_ref[...], nbsel_ref[...] + seln_sc[past // SEL_BLOCK:past // SEL_BLOCK + 1, :])
    o_sel = own_head(acc_sc[...]) / _col_of(l_sc[...])

    reset()
    attend(wcache_ref[...], wbt_ref[...])
    attend(nwin_ref[...], nbwin_ref[...])
    o_win = own_head(acc_sc[...]) / _col_of(l_sc[...])

    gts = gt_ref[...]
    out_ref[...] = gts[:, 0:1] * o_cmp + gts[:, 1:2] * o_sel + gts[:, 2:3] * o_win


def _sattn_call(pt_flat, qbig, kvc, cbt, ovt, smat, e2, pool, sblast, nsel, nbsel, wcache, wbt, nwin, nbwin,
                gt, layer, nb, npg, n_new):
    nrow = ovt.shape[0]
    ck = CH_PAGES * PAGE_SIZE

    def full(shape):
        return pl.BlockSpec(shape, lambda b, pt: (0,) * len(shape))

    def perb(shape):
        return pl.BlockSpec((None,) + shape, lambda b, pt: (b,) + (0,) * len(shape))

    return pl.pallas_call(
        functools.partial(_sattn_kernel, layer=layer, npg=npg, n_new=n_new),
        grid_spec=pltpu.PrefetchScalarGridSpec(
            num_scalar_prefetch=1,
            grid=(nb,),
            in_specs=[perb((512, 128)), perb((npg * 8, 512)), full(cbt.shape), full(ovt.shape), full((128, 128)),
                      full((ck, 128)), pl.BlockSpec(memory_space=pl.ANY), full((ck, 128)),
                      perb((128, 512)), full((128, 128)),
                      pl.BlockSpec((None, None, WINDOW, 512), lambda b, pt: (b, layer, 0, 0)), full((WINDOW, 128)),
                      perb((128, 512)), full((128, 128)), perb((128, 128))],
            out_specs=perb((128, 128)),
            scratch_shapes=[pltpu.VMEM((2, CH_PAGES, PAGE_SIZE, 512), pool.dtype), pltpu.SemaphoreType.DMA((2,)),
                            pltpu.VMEM((nrow + 128, 128), f32), pltpu.VMEM((1, 128), f32), pltpu.VMEM((1, 128), f32),
                            pltpu.VMEM((128, 512), f32)],
        ),
        out_shape=jax.ShapeDtypeStruct((nb, 128, 128), f32),
        compiler_params=_cparams(("arbitrary",)),
        name="sample_attn",
    )(pt_flat, qbig, kvc, cbt, ovt, smat, e2, pool, sblast, nsel, nbsel, wcache, wbt, nwin, nbwin, gt)


def _s5_kernel(*refs, nb, tc):
    u_refs = refs[:nb]
    (h0r_ref, h0i_ref, abr_ref, abi_ref, bre_ref, bim_ref, cre_ref, cim_ref, d_ref, gw_ref, gb_ref,
     o_ref, hr_out, hi_out, xr, xi, hr, hi) = refs[nb:]
    c = pl.program_id(0)

    @pl.when(c == 0)
    def _():
        hr[...] = h0r_ref[...]
        hi[...] = h0i_ref[...]

    for b in range(nb):
        u = u_refs[b][...].astype(bf16)
        for cb in range(4):
            ub = u[:, cb * 128:(cb + 1) * 128]
            pr = _dot(ub, bre_ref[cb])
            pi = _dot(ub, bim_ref[cb])
            for k in range(4):
                xr[cb * 4 + k, b * tc:(b + 1) * tc, :] = pr[:, k * 128:(k + 1) * 128]
                xi[cb * 4 + k, b * tc:(b + 1) * tc, :] = pi[:, k * 128:(k + 1) * 128]

    for lc in range(4):
        slabs = [lc * 4 + k for k in range(4)]
        ar = [abr_ref[0:nb, j * 128:(j + 1) * 128] for j in slabs]
        ai = [abi_ref[0:nb, j * 128:(j + 1) * 128] for j in slabs]

        def body(t, carry, slabs=slabs, ar=ar, ai=ai):
            rows = pl.ds(t, nb, stride=tc)
            new = []
            for k, j in enumerate(slabs):
                cr, ci = carry[2 * k], carry[2 * k + 1]
                nr = ar[k] * cr - ai[k] * ci + xr[j, rows, :]
                ni = ar[k] * ci + ai[k] * cr + xi[j, rows, :]
                xr[j, rows, :] = nr
                xi[j, rows, :] = ni
                new += [nr, ni]
            return tuple(new)

        init = []
        for j in slabs:
            init += [hr[0:nb, j * 128:(j + 1) * 128], hi[0:nb, j * 128:(j + 1) * 128]]
        fin = lax.fori_loop(0, tc, body, tuple(init), unroll=4)
        for k, j in enumerate(slabs):
            hr[0:nb, j * 128:(j + 1) * 128] = fin[2 * k]
            hi[0:nb, j * 128:(j + 1) * 128] = fin[2 * k + 1]

    ys = []
    for cb in range(4):
        hre = jnp.concatenate([xr[cb * 4 + k] for k in range(4)], axis=1).astype(bf16)
        him = jnp.concatenate([xi[cb * 4 + k] for k in range(4)], axis=1).astype(bf16)
        ys.append(_dot(hre, cre_ref[cb]) - _dot(him, cim_ref[cb]))
    u_all = jnp.concatenate([u_refs[b][...] for b in range(nb)], axis=0)
    y = _gelu(jnp.concatenate(ys, axis=1) + d_ref[...] * u_all)
    o = y * _sigmoid(_dot(y.astype(bf16), gw_ref[...]) + gb_ref[...])
    for b in range(nb):
        o_ref[b] = o[b * tc:(b + 1) * tc]

    @pl.when(c == pl.num_programs(0) - 1)
    def _():
        hr_out[...] = hr[...]
        hi_out[...] = hi[...]


def _s5_call(proj, row0, nb, t, tc, h0r, h0i, abr, abi, bre, bim, cre, cim, d, gw, gb):
    def full(shape):
        return pl.BlockSpec(shape, lambda c: (0,) * len(shape))

    u_specs = [pl.BlockSpec((tc, 512), lambda c, b=b: ((row0 + b * t) // tc + c, 5)) for b in range(nb)]
    return pl.pallas_call(
        functools.partial(_s5_kernel, nb=nb, tc=tc),
        grid=(t // tc,),
        in_specs=u_specs + [full((8, 2048)), full((8, 2048)), full((8, 2048)), full((8, 2048)),
                            full((4, 128, 512)), full((4, 128, 512)), full((4, 512, 128)), full((4, 512, 128)),
                            full((1, 512)), full((512, 512)), full((1, 512))],
        out_specs=[pl.BlockSpec((nb, tc, 512), lambda c: (0, c, 0)), full((8, 2048)), full((8, 2048))],
        out_shape=[jax.ShapeDtypeStruct((nb, t, 512), f32), jax.ShapeDtypeStruct((8, 2048), f32),
                   jax.ShapeDtypeStruct((8, 2048), f32)],
        scratch_shapes=[pltpu.VMEM((16, nb * tc, 128), f32), pltpu.VMEM((16, nb * tc, 128), f32),
                        pltpu.VMEM((8, 2048), f32), pltpu.VMEM((8, 2048), f32)],
        compiler_params=_cparams(("arbitrary",)),
        name="s5",
    )(*([proj] * nb), h0r, h0i, abr, abi, bre, bim, cre, cim, d, gw, gb)


def _gmlp_kernel(u_ref, v_ref, w_ref, bs_ref, o_ref, *, rows):
    u = u_ref[...]
    v = v_ref[...]
    outs = []
    for g in range(GMLP_GROUPS):
        vg = v[:, g * 128:(g + 1) * 128]
        if rows < CHUNK:
            vg = jnp.concatenate([vg, jnp.zeros((CHUNK - rows, 128), f32)], axis=0)
        mixed = _dot(w_ref[g][0:rows, :], vg.astype(bf16)) + bs_ref[0:rows, g:g + 1]
        outs.append(u[:, g * 128:(g + 1) * 128] * mixed)
    o_ref[...] = jnp.concatenate(outs, axis=1)


def _gmlp_call(uga, vn, w, bs, row0, nrows, rows):
    blk0 = row0 // rows

    def full(shape):
        return pl.BlockSpec(shape, lambda i: (0,) * len(shape))

    return pl.pallas_call(
        functools.partial(_gmlp_kernel, rows=rows),
        grid=(nrows // rows,),
        in_specs=[pl.BlockSpec((rows, 512), lambda i: (blk0 + i, 0)), pl.BlockSpec((rows, 512), lambda i: (blk0 + i, 0)),
                  full((GMLP_GROUPS, CHUNK, CHUNK)), full((CHUNK, 128))],
        out_specs=pl.BlockSpec((rows, 512), lambda i: (i, 0)),
        out_shape=jax.ShapeDtypeStruct((nrows, 512), f32),
        compiler_params=_cparams(("parallel",)),
        name="gmlp",
    )(uga, vn, w, bs)


def _merge_kernel(*refs, has_y):
    nres = 3 if has_y else 1
    oa_ref, ob_ref, oc_ref, gn_ref, w_ref, gf_ref, rwh_ref, rwl_ref, rb_ref, x1_ref, xn_ref, rt_ref = refs[nres:]

    def rms(v, g):
        return (v * lax.rsqrt(jnp.mean(v * v, axis=-1, keepdims=True) + EPS) * g).astype(bf16)

    acc = _dot(rms(oa_ref[...], gn_ref[:, 0:1024]), w_ref[0:1024, :])
    acc += _dot(rms(ob_ref[...], gn_ref[:, 1024:1536]), w_ref[1024:1536, :])
    acc += _dot(rms(oc_ref[...], gn_ref[:, 1536:2048]), w_ref[1536:2048, :])
    x1 = _residual(refs, has_y) + acc
    x1_ref[...] = x1
    xn = x1 * lax.rsqrt(jnp.mean(x1 * x1, axis=-1, keepdims=True) + EPS) * gf_ref[...]
    xn_ref[...] = xn
    hi = xn.astype(bf16)
    lo = (xn - hi.astype(f32)).astype(bf16)
    logits = _dot(hi, rwh_ref[...]) + _dot(lo, rwh_ref[...]) + _dot(hi, rwl_ref[...]) + rb_ref[...]
    lane = lax.broadcasted_iota(jnp.int32, logits.shape, 1)
    lanef = lane.astype(f32)
    is_g = lane < MOE_GROUPS
    gl = jnp.where(is_g, logits, NEG)
    gm = jnp.max(gl, axis=-1, keepdims=True)
    gidx = jnp.min(jnp.where(gl == gm, lanef, 1e9), axis=-1, keepdims=True)
    gprob = 1.0 / jnp.sum(jnp.where(is_g, jnp.exp(logits - gm), 0.0), axis=-1, keepdims=True)
    lo_lane = MOE_GROUPS + EXPERTS_PER_GROUP * gidx
    inl = jnp.where((lanef >= lo_lane) & (lanef < lo_lane + EXPERTS_PER_GROUP), logits, NEG)
    v1 = jnp.max(inl, axis=-1, keepdims=True)
    i1 = jnp.min(jnp.where(inl == v1, lanef, 1e9), axis=-1, keepdims=True)
    inl2 = jnp.where(lanef == i1, NEG, inl)
    v2 = jnp.max(inl2, axis=-1, keepdims=True)
    i2 = jnp.min(jnp.where(inl2 == v2, lanef, 1e9), axis=-1, keepdims=True)
    e2 = jnp.exp(v2 - v1)
    w1 = gprob / (1.0 + e2)
    w2 = gprob * e2 / (1.0 + e2)
    rt_ref[...] = jnp.where(lane == 0, i1 - MOE_GROUPS,
                            jnp.where(lane == 1, i2 - MOE_GROUPS,
                                      jnp.where(lane == 2, w1, jnp.where(lane == 3, w2, 0.0))))


def _merge_call(x, y2, oa, ob, oc, gn, w, gf, rwh, rwl, rb, tm=320):
    nt = x.shape[0]
    res_specs, res_args = _residual_specs(x, y2, tm, lambda i: i)

    def row(width):
        return pl.BlockSpec((tm, width), lambda i: (i, 0))

    def full(shape):
        return pl.BlockSpec(shape, lambda i: (0,) * len(shape))

    return pl.pallas_call(
        functools.partial(_merge_kernel, has_y=y2 is not None),
        grid=(nt // tm,),
        in_specs=res_specs + [row(1024), row(512), row(512), full((1, 2048)), full((2048, 2048)), full((1, 2048)),
                              full((2048, 128)), full((2048, 128)), full((1, 128))],
        out_specs=[row(2048), row(2048), row(128)],
        out_shape=[jax.ShapeDtypeStruct((nt, 2048), f32), jax.ShapeDtypeStruct((nt, 2048), f32),
                   jax.ShapeDtypeStruct((nt, 128), f32)],
        compiler_params=_cparams(("parallel",)),
        name="merge_router",
    )(*res_args, oa, ob, oc, gn, w, gf, rwh, rwl, rb)


def _expert_kernel(te_ref, nu_ref, src_ref, dst_ref, xn_hbm, wg_ref, w1_ref, w3_ref, w2_ref, y_hbm,
                   xbuf, obuf, w1b, w3b, w2b, gsem, ssem, *, tm, ntiles):
    t = pl.program_id(0)
    n_used = nu_ref[0]
    slot = t % 2

    def gather_row(tt, sl, r):
        tok = src_ref[tt * tm + r]
        pltpu.make_async_copy(xn_hbm.at[pl.ds(tok, 1)], xbuf.at[sl, pl.ds(r, 1)], gsem.at[sl]).start()

    def gather_wait(sl):
        pltpu.make_async_copy(xn_hbm.at[pl.ds(0, tm)], xbuf.at[sl], gsem.at[sl]).wait()

    def scatter_row(tt, sl, r):
        row = dst_ref[tt * tm + r]
        pltpu.make_async_copy(obuf.at[sl, pl.ds(r, 1)], y_hbm.at[pl.ds(row, 1)], ssem.at[sl]).start()

    def scatter_wait(sl):
        pltpu.make_async_copy(obuf.at[sl], y_hbm.at[pl.ds(0, tm)], ssem.at[sl]).wait()

    def rolled(fn, tt, sl):
        def body(r, c):
            fn(tt, sl, r)
            return c
        lax.fori_loop(0, tm, body, 0, unroll=8)

    @pl.when(t == 0)
    def _():
        rolled(gather_row, 0, 0)
        npair = y_hbm.shape[0] - 2 * tm
        for sl in range(2):
            obuf[sl] = jnp.zeros((tm, D_MODEL), f32)
            cp = pltpu.make_async_copy(obuf.at[sl], y_hbm.at[pl.ds(npair + sl * tm, tm)], ssem.at[sl])
            cp.start()
            cp.wait()

    def tile_step(with_scatter):
        gather_wait(slot)

        @pl.when(t >= 3)
        def _():
            scatter_wait(t % 3)

        @pl.when((t == 0) | (te_ref[t] != te_ref[jnp.maximum(t - 1, 0)]))
        def _():
            w1b[...] = w1_ref[...].astype(bf16)
            w3b[...] = w3_ref[...].astype(bf16)
            w2b[...] = w2_ref[...].astype(bf16)

        nxt = jnp.minimum(t + 1, ntiles - 1)
        for r in range(tm):
            gather_row(nxt, 1 - slot, r)
        x = xbuf[slot].astype(bf16)
        a = _dot(x, w1b[...])
        hmid = a * _sigmoid(a) * _dot(x, w3b[...]) * wg_ref[...]
        if with_scatter:
            prev_slot = (t - 1) % 3
            for r in range(tm):
                scatter_row(t - 1, prev_slot, r)
        obuf[t % 3] = _dot(hmid.astype(bf16), w2b[...])

    @pl.when((t < n_used) & (t == 0))
    def _():
        tile_step(False)

    @pl.when((t < n_used) & (t >= 1))
    def _():
        tile_step(True)

    @pl.when(t == ntiles - 1)
    def _():
        last = n_used - 1
        gather_wait(n_used % 2)

        @pl.when(last >= 2)
        def _():
            scatter_wait((last - 2) % 3)

        @pl.when(last >= 1)
        def _():
            scatter_wait((last - 1) % 3)

        rolled(scatter_row, last, last % 3)
        scatter_wait(last % 3)


def _expert_call(tile_e, n_used, src_tok, dst_row, xn, wgt, w1, w3, w2, layer, tm, ntiles, nrows_out):
    def wmap(t, te, nu, src, dst):
        return (layer, te[t], 0, 0)

    return pl.pallas_call(
        functools.partial(_expert_kernel, tm=tm, ntiles=ntiles),
        grid_spec=pltpu.PrefetchScalarGridSpec(
            num_scalar_prefetch=4,
            grid=(ntiles,),
            in_specs=[pl.BlockSpec(memory_space=pl.ANY),
                      pl.BlockSpec((tm, 1), lambda t, te, nu, src, dst: (t, 0)),
                      pl.BlockSpec((None, None, D_MODEL, D_EXPERT), wmap),
                      pl.BlockSpec((None, None, D_MODEL, D_EXPERT), wmap),
                      pl.BlockSpec((None, None, D_EXPERT, D_MODEL), wmap)],
            out_specs=pl.BlockSpec(memory_space=pl.ANY),
            scratch_shapes=[pltpu.VMEM((2, tm, D_MODEL), f32), pltpu.VMEM((3, tm, D_MODEL), f32),
                            pltpu.VMEM((D_MODEL, D_EXPERT), bf16), pltpu.VMEM((D_MODEL, D_EXPERT), bf16),
                            pltpu.VMEM((D_EXPERT, D_MODEL), bf16),
                            pltpu.SemaphoreType.DMA((2,)), pltpu.SemaphoreType.DMA((3,))],
        ),
        out_shape=jax.ShapeDtypeStruct((nrows_out, D_MODEL), f32),
        compiler_params=_cparams(("arbitrary",), disable_bounds_checks=True),
        name="experts",
    )(tile_e, n_used, src_tok, dst_row, xn, wgt, w1, w3, w2)


def _add3_kernel(x_ref, ya_ref, yb_ref, o_ref):
    o_ref[...] = x_ref[...] + ya_ref[...] + yb_ref[...]


def _add3_call(x, y2, tm=640):
    nt = x.shape[0]
    nrow = nt // tm
    return pl.pallas_call(
        _add3_kernel,
        grid=(nrow,),
        in_specs=[pl.BlockSpec((tm, D_MODEL), lambda i: (i, 0)), pl.BlockSpec((tm, D_MODEL), lambda i: (i, 0)),
                  pl.BlockSpec((tm, D_MODEL), lambda i: (i + nrow, 0))],
        out_specs=pl.BlockSpec((tm, D_MODEL), lambda i: (i, 0)),
        out_shape=jax.ShapeDtypeStruct((nt, D_MODEL), f32),
        compiler_params=_cparams(("parallel",)),
        name="residual_add",
    )(x, y2, y2)


EXPERT_TM = 256
PROMPT_TQ = 256


def _bucket(n):
    max_exact = NUM_BUCKETS // 2
    nf = jnp.maximum(n, max_exact).astype(f32)
    large = max_exact + (jnp.log(nf / max_exact) / math.log(MAX_DISTANCE / max_exact)
                         * (NUM_BUCKETS - max_exact)).astype(jnp.int32)
    return jnp.where(n < max_exact, n, jnp.minimum(large, NUM_BUCKETS - 1))


def _bias_tables(rel_bias, tq, t, past, n_new):
    bd = (rel_bias[_bucket(jnp.arange(128))] - rel_bias[NUM_BUCKETS - 1][None, :]) * LOG2E

    def look(dist):
        oh = jax.nn.one_hot(jnp.clip(dist, 0, 127), 128, dtype=f32)
        return jnp.einsum('...d,dh->...h', oh, bd, precision=lax.Precision.HIGHEST)

    ntile = t // tq
    r = jnp.arange(tq)
    d0 = r[:, None] - r[None, :]
    t0 = jnp.where((d0 >= 0)[..., None], look(d0), NEG)
    t1 = look(tq + d0)
    tz = jnp.stack([t0, t1], axis=0).reshape(2, tq, tq, N_KV_HEADS, GQA)
    tz = jnp.transpose(tz, (3, 0, 2, 4, 1)).reshape(N_KV_HEADS, 2, tq, GQA * tq)
    n = jnp.arange(2 * (t // CMP_STRIDE))
    cbl = jnp.transpose(look(tq * (ntile - 1) + r[None, :] - CMP_STRIDE * n[:, None] - (CMP_BLOCK - 1)), (2, 0, 1))

    qq = jnp.arange(8)

    def look_t(dist, vis):
        tab = jnp.where(vis[:, None, :], jnp.transpose(look(dist), (0, 2, 1)), NEG)
        return tab.reshape(dist.shape[0], N_HEADS_A * 8)

    n_cmp_s = (past + n_new - CMP_BLOCK) // CMP_STRIDE + 1
    nchunk = past // CMP_STRIDE
    nn = jnp.arange(nchunk)[:, None]
    dist = past + qq[None, :] - (CMP_STRIDE * nn + CMP_BLOCK - 1)
    cbt = look_t(dist, (dist >= 0) & (nn < n_cmp_s))
    ck = CH_PAGES * PAGE_SIZE
    rr = jnp.arange(ck - 128, ck)[:, None]
    dist = qq[None, :] + ck - rr
    sblast = jnp.concatenate([jnp.zeros((ck - 128, 128), f32), look_t(dist, dist >= 0)], axis=0)
    r128 = jnp.arange(128)[:, None]
    dist = qq[None, :] - r128
    nbnew = look_t(dist, (dist >= 0) & (r128 < n_new))
    rw = jnp.arange(WINDOW)[:, None]
    dist = WINDOW + qq[None, :] - rw
    wbt = look_t(dist, (dist >= 0) & (dist <= WINDOW))
    return tz, cbl, cbt, sblast, nbnew, wbt


def _static_mats(t, past, n_new):
    n = np.arange(128)
    n_cmp = (t - CMP_BLOCK) // CMP_STRIDE + 1
    s = np.arange(128)
    ov = ((CMP_STRIDE * n[:, None] < SEL_BLOCK * s[None, :] + SEL_BLOCK)
          & (CMP_STRIDE * n[:, None] + CMP_BLOCK > SEL_BLOCK * s[None, :])
          & (n[:, None] < n_cmp) & (s[None, :] < t // SEL_BLOCK))
    ex = (np.arange(t)[None, :] // SEL_BLOCK == s[:, None])
    nsel_s = -(-(past + n_new) // SEL_BLOCK)
    nrow = -(-nsel_s // 8) * 8
    n_cmp_s = (past + n_new - CMP_BLOCK) // CMP_STRIDE + 1
    ss = np.arange(nrow)[:, None]
    ns = np.arange(past // CMP_STRIDE)[None, :]
    ovt = ((CMP_STRIDE * ns < SEL_BLOCK * ss + SEL_BLOCK) & (CMP_STRIDE * ns + CMP_BLOCK > SEL_BLOCK * ss)
           & (ns < n_cmp_s) & (ss < nsel_s))
    c = np.arange(128)
    smat = (c[:, None] // 32 == c[None, :] // 32) & (c[:, None] % 8 == c[None, :] % 8)
    ck = CH_PAGES * PAGE_SIZE
    e2 = (np.arange(ck)[:, None] // SEL_BLOCK == np.arange(128)[None, :])
    cvt = lambda a: jnp.asarray(a.astype(np.float32), dtype=bf16)
    return cvt(ov.T), cvt(ex.T), cvt(ovt), cvt(smat), cvt(e2)


def _route_metadata(route, nt, tm, ntiles):
    e_flat = jnp.concatenate([route[:, 0], route[:, 1]]).astype(jnp.int32)
    w_flat = jnp.concatenate([route[:, 2], route[:, 3]])
    npair = 2 * nt
    order = jnp.argsort(e_flat, stable=True).astype(jnp.int32)
    counts = jnp.sum(jax.nn.one_hot(e_flat, N_EXPERTS, dtype=jnp.int32), axis=0)
    tiles_e = (counts + tm - 1) // tm
    tend = jnp.cumsum(tiles_e)
    tstart = tend - tiles_e
    cstart = jnp.cumsum(counts) - counts
    n_used = tend[-1]
    tidx = jnp.arange(ntiles, dtype=jnp.int32)
    tile_e = jnp.sum((jnp.minimum(tidx, n_used - 1)[:, None] >= tend[None, :]).astype(jnp.int32), axis=1)
    tile_oh = jax.nn.one_hot(tile_e, N_EXPERTS, dtype=jnp.int32)
    t_cnt = jnp.sum(tile_oh * counts[None, :], axis=1)
    t_first = jnp.sum(tile_oh * (cstart - tstart * tm)[None, :], axis=1) + tidx * tm
    rows = jnp.arange(tm, dtype=jnp.int32)[None, :]
    rank = tidx[:, None] * tm + rows - jnp.sum(tile_oh * tstart[None, :], axis=1)[:, None] * tm
    valid = (rank < t_cnt[:, None]) & (tidx[:, None] < n_used)
    pair = order[jnp.clip(t_first[:, None] + rows, 0, npair - 1)]
    dump = npair + (tidx[:, None] % 2) * tm + rows
    src_tok = jnp.where(valid, pair % nt, 0).reshape(-1)
    dst_row = jnp.where(valid, pair, dump).reshape(-1)
    wgt = jnp.where(valid, w_flat[pair], 0.0).reshape(-1, 1)
    return tile_e.astype(jnp.int32), n_used.reshape(1).astype(jnp.int32), src_tok, dst_row, wgt


def _block_diag(blocks):
    n, r, c = blocks.shape
    return jnp.einsum('grc,gk->grkc', blocks, jnp.eye(n, dtype=blocks.dtype)).reshape(n * r, n * c)


def kernel(x_prompt, x_sample, cache_kv_cmp, cache_kv_sel, cache_kv_win, state_ssm_re, state_ssm_im, page_table, rel_bias, norm_mix, w_in, qk_norm, cmp_pos, cmp_w1, cmp_w2, ssm_a_re, ssm_a_im, ssm_log_dt, ssm_b_re, ssm_b_im, ssm_c_re, ssm_c_im, ssm_d, ssm_glu_w, ssm_glu_b, gmlp_norm, gmlp_ws, gmlp_bs, out_norm, w_out, norm_ffn, router_group_w, router_group_b, router_expert_w, router_expert_b, expert_w1, expert_w3, expert_w2):
    bp, t, _ = x_prompt.shape
    bs, s_new, _ = x_sample.shape
    npg = page_table.shape[1]
    past = npg * PAGE_SIZE
    nphys = cache_kv_cmp.shape[0]
    n_p, n_s = bp * t, bs * s_new
    nt = -(-(n_p + n_s) // 640) * 640
    tq = PROMPT_TQ
    assert t % tq == 0 and WINDOW % tq == 0 and s_new == 8 and bs == 8 and npg % CH_PAGES == 0

    x = jnp.concatenate([x_prompt.reshape(n_p, D_MODEL), x_sample.reshape(n_s, D_MODEL),
                         jnp.zeros((nt - n_p - n_s, D_MODEL), f32)], axis=0)
    y2 = None
    pool_cmp = cache_kv_cmp.reshape(nphys, DEPTH, PAGE_SIZE, 512)
    pool_sel = cache_kv_sel.reshape(nphys, DEPTH, PAGE_SIZE, 512).astype(bf16)
    wcache = cache_kv_win.reshape(bs, DEPTH, WINDOW, 512)
    pt_flat = page_table.reshape(-1).astype(jnp.int32)
    pt_ident = jnp.arange(bp * (t // PAGE_SIZE), dtype=jnp.int32)

    tz, cbl, cbt, sblast, nbnew, wbt = _bias_tables(rel_bias, tq, t, past, s_new)
    ov, ex, ovt, smat, e2 = _static_mats(t, past, s_new)
    g64 = _block_diag(jnp.ones((8, 64, 64), bf16))
    g128 = _block_diag(jnp.ones((4, 128, 128), bf16))
    ones64 = jnp.ones((HEAD_DIM,), f32)
    tril = jnp.tril(jnp.ones((CHUNK, CHUNK), f32))
    zeros_state = jnp.zeros((8, SSM_GROUPS * SSM_STATE), f32)
    ntiles = 2 * nt // EXPERT_TM + N_EXPERTS
    all_rows = lambda p, s: jnp.concatenate([p, s, jnp.zeros((nt - n_p - n_s, p.shape[1]), p.dtype)], axis=0)

    outs = {k: [] for k in ('pc', 'ps', 'pw', 'pr', 'pi', 'pv', 'sc', 'ss', 'sw', 'sr', 'si', 'sv')}
    for l in range(DEPTH):
        wl = w_in[l]
        o1, o2, o3 = ATTN_WIDTH, ATTN_WIDTH + 6 * KV_WIDTH, ATTN_WIDTH + 6 * KV_WIDTH + 3 * N_HEADS_A
        w_pad = jnp.concatenate([wl[:, :o2], wl[:, o3:], wl[:, o2:o3],
                                 jnp.zeros((D_MODEL, 128 - 3 * N_HEADS_A), f32)], axis=1).astype(bf16)
        gq = jnp.tile(qk_norm[l, 0], N_HEADS_A).reshape(1, 1024)
        gsel = jnp.tile(jnp.concatenate([qk_norm[l, 2], ones64]), N_KV_HEADS).reshape(1, 512)
        gwin = jnp.tile(jnp.concatenate([qk_norm[l, 3], ones64]), N_KV_HEADS).reshape(1, 512)
        gk = jnp.concatenate([qk_norm[l, 1], ones64]).reshape(1, 128)
        w1k = cmp_w1[l, 0].reshape(CMP_BLOCK, HEAD_DIM, HEAD_DIM)
        w1v = cmp_w1[l, 1].reshape(CMP_BLOCK, HEAD_DIM, HEAD_DIM)
        zz = jnp.zeros_like(w1k)
        wfull = jnp.concatenate([jnp.concatenate([w1k, zz], axis=2), jnp.concatenate([zz, w1v], axis=2)], axis=1)
        wab = jnp.concatenate([wfull[:16].reshape(2048, 128), wfull[16:].reshape(2048, 128)], axis=1).astype(bf16)
        posf = jnp.concatenate([cmp_pos[l, 0], cmp_pos[l, 1]], axis=1)
        pos8 = jnp.broadcast_to(jnp.concatenate([posf[:16].reshape(1, 2048), posf[16:].reshape(1, 2048)], axis=1),
                                (8, 4096))
        w2bd = _block_diag(cmp_w2[l]).astype(bf16)

        dt = jnp.exp(ssm_log_dt[l])[:, None]
        a_re, a_im = ssm_a_re[l], ssm_a_im[l]
        mag = jnp.exp(dt * a_re)
        ab_re, ab_im = mag * jnp.cos(dt * a_im), mag * jnp.sin(dt * a_im)
        den = a_re * a_re + a_im * a_im
        f_re = ((ab_re - 1.0) * a_re + ab_im * a_im) / den
        f_im = (ab_im * a_re - (ab_re - 1.0) * a_im) / den
        bb_re = f_re[..., None] * ssm_b_re[l] - f_im[..., None] * ssm_b_im[l]
        bb_im = f_re[..., None] * ssm_b_im[l] + f_im[..., None] * ssm_b_re[l]
        abr = jnp.broadcast_to(ab_re.reshape(1, -1), (8, SSM_GROUPS * SSM_STATE))
        abi = jnp.broadcast_to(ab_im.reshape(1, -1), (8, SSM_GROUPS * SSM_STATE))
        eye8 = jnp.eye(8, dtype=f32)

        def in_blocks(bb):
            xx = jnp.transpose(bb, (0, 2, 1)).reshape(4, 8, SSM_GROUP, SSM_STATE)
            return jnp.einsum('agcn,gk->agckn', xx, eye8).reshape(4, 128, 512).astype(bf16)

        def out_blocks(cc):
            yy = jnp.transpose(cc, (0, 2, 1)).reshape(4, 8, SSM_STATE, SSM_GROUP)
            return jnp.einsum('agnc,gk->agnkc', yy, eye8).reshape(4, 512, 128).astype(bf16)

        s5p = (abr, abi, in_blocks(bb_re), in_blocks(bb_im), out_blocks(ssm_c_re[l]), out_blocks(ssm_c_im[l]),
               ssm_d[l].reshape(1, 512), ssm_glu_w[l].astype(bf16), ssm_glu_b[l].reshape(1, 512))
        gw = (gmlp_ws[l] * tril).astype(bf16)
        gbs = jnp.concatenate([gmlp_bs[l].T, jnp.zeros((CHUNK, 128 - GMLP_GROUPS), f32)], axis=1)
        rw = jnp.concatenate([router_group_w[l], jnp.transpose(router_expert_w[l], (1, 0, 2)).reshape(D_MODEL, N_EXPERTS),
                              jnp.zeros((D_MODEL, 128 - MOE_GROUPS - N_EXPERTS), f32)], axis=1)
        rwh = rw.astype(bf16)
        rwl = (rw - rwh.astype(f32)).astype(bf16)
        rb = jnp.concatenate([router_group_b[l], router_expert_b[l].reshape(-1),
                              jnp.zeros((128 - MOE_GROUPS - N_EXPERTS,), f32)]).reshape(1, 128)

        proj = _proj_call(x, y2, norm_mix[l].reshape(1, D_MODEL), w_pad)
        qn, ksel, kwin, gates, uga, vn = _post_call(proj, gq, gsel, gwin, gmlp_norm[l].reshape(1, 512), g64, g128)

        src_p = proj.reshape(nt // PAGE_SIZE, 1, PAGE_SIZE, PROJ_W)
        kvc_p = _cmp_call(pt_ident, src_p, 0, ATTN_WIDTH // 128, bp, t // PAGE_SIZE, wab, pos8, w2bd, gk)
        kvc_s = _cmp_call(pt_flat, pool_cmp, l, 0, bs, npg, wab, pos8, w2bd, gk)
        oa_p = _pattn_call(qn, kvc_p, ksel, kwin, gates, cbl, tz, ov, ex, bp, t, tq)

        qs = qn[n_p:n_p + n_s].reshape(bs, s_new, N_KV_HEADS, GQA, HEAD_DIM)
        qa = jnp.transpose(qs, (0, 2, 4, 3, 1)).reshape(bs, N_KV_HEADS, HEAD_DIM, GQA * s_new)
        qbig = jnp.einsum('bhdc,hk->bhdkc', qa, jnp.eye(N_KV_HEADS, dtype=bf16))
        qbig = jnp.pad(qbig, ((0, 0), (0, 0), (0, 64), (0, 0), (0, 0))).reshape(bs, 512, 128)
        gs = gates[n_p:n_p + n_s, :48].reshape(bs, s_new, 3, N_KV_HEADS, GQA)
        gt = jnp.pad(jnp.transpose(gs, (0, 3, 4, 1, 2)).reshape(bs, 128, 3), ((0, 0), (0, 0), (0, 125)))
        new_sel = jnp.pad(ksel[n_p:n_p + n_s].reshape(bs, s_new, 512), ((0, 0), (0, 128 - s_new), (0, 0)))
        new_win = jnp.pad(kwin[n_p:n_p + n_s].reshape(bs, s_new, 512), ((0, 0), (0, 128 - s_new), (0, 0)))
        osmp = _sattn_call(pt_flat, qbig, kvc_s, cbt, ovt, smat, e2, pool_sel, sblast, new_sel, nbnew,
                           wcache, wbt, new_win, nbnew, gt, l, bs, npg, s_new)
        oa_s = jnp.transpose(osmp[:, :, 64:].reshape(bs, N_KV_HEADS, GQA, s_new, HEAD_DIM),
                             (0, 3, 1, 2, 4)).reshape(n_s, ATTN_WIDTH)
        oa = all_rows(oa_p, oa_s)

        ob_p, hr_p, hi_p = _s5_call(proj, 0, bp, t, 256, zeros_state, zeros_state, *s5p)
        ob_s, hr_s, hi_s = _s5_call(proj, n_p, bs, s_new, s_new, state_ssm_re[:, l].reshape(bs, -1),
                                    state_ssm_im[:, l].reshape(bs, -1), *s5p)
        ob = all_rows(ob_p.reshape(n_p, 512), ob_s.reshape(n_s, 512))

        oc_p = _gmlp_call(uga, vn, gw, gbs, 0, n_p, CHUNK)
        oc_s = _gmlp_call(uga, vn, gw, gbs, n_p, n_s, s_new)
        oc = all_rows(oc_p, oc_s)

        x1, xn2, route = _merge_call(x, y2, oa, ob, oc, out_norm[l].reshape(1, -1), w_out[l].astype(bf16),
                                     norm_ffn[l].reshape(1, -1), rwh, rwl, rb)
        tile_e, n_used, src_tok, dst_row, wgt = _route_metadata(route, nt, EXPERT_TM, ntiles)
        y2 = _expert_call(tile_e, n_used, src_tok, dst_row, xn2, wgt, expert_w1, expert_w3, expert_w2, l,
                          EXPERT_TM, ntiles, 2 * nt + 2 * EXPERT_TM)
        x = x1

        kvshape = (N_KV_HEADS, 2, HEAD_DIM)
        cmp_rows = proj[:, ATTN_WIDTH:ATTN_WIDTH + 512]
        outs['pc'].append(cmp_rows[:n_p].reshape(bp, t, *kvshape))
        outs['ps'].append(ksel[:n_p].reshape(bp, t, *kvshape))
        outs['pw'].append(kwin[:n_p].reshape(bp, t, *kvshape)[:, t - min(WINDOW, t):])
        outs['pr'].append(hr_p[:bp].reshape(bp, SSM_GROUPS, SSM_STATE))
        outs['pi'].append(hi_p[:bp].reshape(bp, SSM_GROUPS, SSM_STATE))
        outs['pv'].append(vn[:n_p].reshape(bp, t, GMLP_WIDTH)[:, (t - 1) // CHUNK * CHUNK:])
        outs['sc'].append(cmp_rows[n_p:n_p + n_s].reshape(bs, s_new, *kvshape))
        outs['ss'].append(ksel[n_p:n_p + n_s].reshape(bs, s_new, *kvshape))
        win_new = kwin[n_p:n_p + n_s].reshape(bs, s_new, *kvshape)
        outs['sw'].append(jnp.concatenate([cache_kv_win[:, l], win_new], axis=1)[:, s_new:])
        outs['sr'].append(hr_s[:bs].reshape(bs, SSM_GROUPS, SSM_STATE))
        outs['si'].append(hi_s[:bs].reshape(bs, SSM_GROUPS, SSM_STATE))
        outs['sv'].append(vn[n_p:n_p + n_s].reshape(bs, s_new, GMLP_WIDTH))

    xf = _add3_call(x, y2)
    st = {k: jnp.stack(v, axis=1) for k, v in outs.items()}
    return (xf[:n_p].reshape(bp, t, D_MODEL), xf[n_p:n_p + n_s].reshape(bs, s_new, D_MODEL),
            st['pc'], st['ps'], st['pw'], st['pr'], st['pi'], st['pv'],
            st['sc'], st['ss'], st['sw'], st['sr'], st['si'], st['sv'])
```

```python
import functools
import math

import numpy as np
import jax
import jax.numpy as jnp
from jax import lax
from jax.experimental import pallas as pl
from jax.experimental.pallas import tpu as pltpu

f32 = jnp.float32
bf16 = jnp.bfloat16

D_MODEL = 2048
DEPTH = 4
PAGE_SIZE = 128
HEAD_DIM = 64
ATTN_WIDTH = 1024
N_HEADS_A = 16
N_KV_HEADS = 4
GQA = 4
KV_WIDTH = 256
CMP_BLOCK = 32
CMP_STRIDE = 16
SEL_BLOCK = 64
SEL_TOPK = 16
WINDOW = 512
NUM_BUCKETS = 32
MAX_DISTANCE = 128
SSM_WIDTH = 512
SSM_GROUP = 16
SSM_GROUPS = 32
SSM_STATE = 64
GMLP_WIDTH = 512
GMLP_GROUPS = 4
CHUNK = 128
MOE_GROUPS = 4
EXPERTS_PER_GROUP = 4
N_EXPERTS = 16
D_EXPERT = 512
SCALE = HEAD_DIM ** -0.5
LOG2E = math.log2(math.e)
EPS = 1e-6
NEG = -1e30

PROJ_W = 4224
LANES = 128
VMEM_LIMIT = 56 * 1024 * 1024


def _cparams(sem, **kw):
    return pltpu.CompilerParams(dimension_semantics=sem, vmem_limit_bytes=VMEM_LIMIT, **kw)


def _dot(a, b):
    return jnp.dot(a, b, preferred_element_type=f32)


def _dot_nt(a, b):
    return lax.dot_general(a, b, (((1,), (1,)), ((), ())), preferred_element_type=f32)


def _split3(x):
    hi = x.astype(bf16)
    r = x - hi.astype(f32)
    mid = r.astype(bf16)
    lo = (r - mid.astype(f32)).astype(bf16)
    return hi, mid, lo


def _gelu(x):
    return 0.5 * x * (1.0 + jnp.tanh(math.sqrt(2.0 / math.pi) * (x + 0.044715 * (x * x * x))))


def _sigmoid(x):
    return 1.0 / (1.0 + jnp.exp(-x))


def _group_mean_sq(x, gmat, group):
    x2 = x * x
    hi = x2.astype(bf16)
    lo = (x2 - hi.astype(f32)).astype(bf16)
    return (_dot(hi, gmat) + _dot(lo, gmat)) * (1.0 / group)


def _residual(refs, has_y):
    if has_y:
        return refs[0][...] + refs[1][...] + refs[2][...]
    return refs[0][...]


def _residual_specs(x, y2, tm, index_of):
    nrow = x.shape[0] // tm
    specs = [pl.BlockSpec((tm, D_MODEL), lambda *ids: (index_of(*ids), 0))]
    args = [x]
    if y2 is not None:
        specs += [pl.BlockSpec((tm, D_MODEL), lambda *ids: (index_of(*ids), 0)),
                  pl.BlockSpec((tm, D_MODEL), lambda *ids: (index_of(*ids) + nrow, 0))]
        args += [y2, y2]
    return specs, args


def _proj_kernel(*refs, has_y):
    nres = 3 if has_y else 1
    g_ref, w_ref, proj_ref, xn_sc = refs[nres:]

    @pl.when(pl.program_id(1) == 0)
    def _():
        x = _residual(refs, has_y)
        ms = jnp.mean(x * x, axis=-1, keepdims=True)
        xn_sc[...] = (x * lax.rsqrt(ms + EPS) * g_ref[...]).astype(bf16)

    proj_ref[...] = _dot(xn_sc[...], w_ref[...])


def _proj_call(x, y2, g, w, tm=320, tn=1408):
    nt = x.shape[0]
    res_specs, res_args = _residual_specs(x, y2, tm, lambda i, j: i)
    return pl.pallas_call(
        functools.partial(_proj_kernel, has_y=y2 is not None),
        grid=(nt // tm, PROJ_W // tn),
        in_specs=res_specs + [pl.BlockSpec((1, D_MODEL), lambda i, j: (0, 0)),
                              pl.BlockSpec((D_MODEL, tn), lambda i, j: (0, j))],
        out_specs=pl.BlockSpec((tm, tn), lambda i, j: (i, j)),
        out_shape=jax.ShapeDtypeStruct((nt, PROJ_W), f32),
        scratch_shapes=[pltpu.VMEM((tm, D_MODEL), bf16)],
        compiler_params=_cparams(("parallel", "arbitrary")),
        name="proj",
    )(*res_args, g, w)


def _post_kernel(q_ref, sel_ref, win_ref, ug_ref, vg_ref, gt_ref, gq_ref, gsel_ref, gwin_ref, gv_ref,
                 g64_ref, g128_ref, qn_ref, ksel_ref, kwin_ref, gate_ref, uga_ref, vn_ref):
    g64 = g64_ref[...]
    lane = lax.broadcasted_iota(jnp.int32, (1, 512), 1)
    is_k = (lane % 128) < 64
    for half in range(2):
        q = q_ref[:, half * 512:(half + 1) * 512]
        ms = _group_mean_sq(q, g64, 64)
        qn = q * lax.rsqrt(ms + EPS) * gq_ref[:, half * 512:(half + 1) * 512]
        qn_ref[:, half * 512:(half + 1) * 512] = (qn * (SCALE * LOG2E)).astype(bf16)
    for src, gref, dst in ((sel_ref, gsel_ref, ksel_ref), (win_ref, gwin_ref, kwin_ref)):
        x = src[...]
        ms = _group_mean_sq(x, g64, 64)
        dst[...] = jnp.where(is_k, x * lax.rsqrt(ms + EPS) * gref[...], x)
    gate_ref[...] = _sigmoid(gt_ref[...])
    uga_ref[...] = _gelu(ug_ref[...])
    v = _gelu(vg_ref[...])
    ms = _group_mean_sq(v, g128_ref[...], 128)
    vn_ref[...] = v * lax.rsqrt(ms + EPS) * gv_ref[...]


def _post_call(proj, gq, gsel, gwin, gv, g64, g128, tm=320):
    nt = proj.shape[0]

    def col(width, idx):
        return pl.BlockSpec((tm, width), lambda i: (i, idx))

    def full(shape):
        return pl.BlockSpec(shape, lambda i: (0,) * len(shape))

    def out(width):
        return pl.BlockSpec((tm, width), lambda i: (i, 0))

    return pl.pallas_call(
        _post_kernel,
        grid=(nt // tm,),
        in_specs=[col(1024, 0), col(512, 3), col(512, 4), col(512, 6), col(512, 7), col(128, 32),
                  full((1, 1024)), full((1, 512)), full((1, 512)), full((1, 512)),
                  full((512, 512)), full((512, 512))],
        out_specs=[out(1024), out(512), out(512), out(128), out(512), out(512)],
        out_shape=[jax.ShapeDtypeStruct((nt, 1024), bf16), jax.ShapeDtypeStruct((nt, 512), f32),
                   jax.ShapeDtypeStruct((nt, 512), f32), jax.ShapeDtypeStruct((nt, 128), f32),
                   jax.ShapeDtypeStruct((nt, 512), f32), jax.ShapeDtypeStruct((nt, 512), f32)],
        compiler_params=_cparams(("parallel",)),
        name="post",
    )(proj, proj, proj, proj, proj, proj, gq, gsel, gwin, gv, g64, g128)


CMP_PAGES_PER_STEP = 32


def _cmp_kernel(pt_ref, *refs, pps, nsteps, nchunk):
    srcs = refs[:pps]
    wab_ref, pos_ref, w2_ref, gk_ref, out_ref, c_sc = refs[pps:]
    s = pl.program_id(2)
    for pp in range(pps // 2):
        row0 = pl.multiple_of(s * (pps * 8) + pp * 16, 16)
        for j in range(CMP_STRIDE):
            a = srcs[2 * pp][pl.ds(j, 8, stride=CMP_STRIDE), :]
            b = srcs[2 * pp + 1][pl.ds(j, 8, stride=CMP_STRIDE), :]
            c_sc[pl.ds(row0, 16), j * 128:(j + 1) * 128] = jnp.concatenate([a, b], axis=0).astype(bf16)

    @pl.when(s == nsteps - 1)
    def _():
        wab = wab_ref[...]
        pq = _dot(c_sc[...], wab)
        pos = pos_ref[...].astype(bf16)
        pterm = _dot(pos[:, :2048], wab[:, :128]) + _dot(pos[:, 2048:], wab[:, 128:])
        pre = pq[:, :128] + pltpu.roll(pq[:, 128:], nchunk - 1, 0) + pterm[0:1, :]
        y = _dot(_gelu(pre).astype(bf16), w2_ref[...])
        lane = lax.broadcasted_iota(jnp.int32, (1, 128), 1)
        is_k = lane < 64
        ms = jnp.sum(jnp.where(is_k, y * y, 0.0), axis=-1, keepdims=True) * (1.0 / 64)
        out_ref[...] = jnp.where(is_k, y * lax.rsqrt(ms + EPS) * gk_ref[...], y)


def _cmp_call(pt_flat, src4, layer, col0, nb, npg, wab, pos, w2, gk):
    nchunk = npg * 8
    pps = min(CMP_PAGES_PER_STEP, npg)
    nsteps = npg // pps

    def src_spec(k):
        return pl.BlockSpec((None, None, PAGE_SIZE, 128),
                            lambda b, h, s, pt: (pt[b * npg + s * pps + k], layer, 0, col0 + h))

    def full(shape):
        return pl.BlockSpec(shape, lambda b, h, s, pt: (0,) * len(shape))

    return pl.pallas_call(
        functools.partial(_cmp_kernel, pps=pps, nsteps=nsteps, nchunk=nchunk),
        grid_spec=pltpu.PrefetchScalarGridSpec(
            num_scalar_prefetch=1,
            grid=(nb, N_KV_HEADS, nsteps),
            in_specs=[src_spec(k) for k in range(pps)]
                     + [full((2048, 256)), full((8, 4096)), full((128, 128)), full((1, 128))],
            out_specs=pl.BlockSpec((None, nchunk, 128), lambda b, h, s, pt: (b, 0, h)),
            scratch_shapes=[pltpu.VMEM((nchunk, 2048), bf16)],
        ),
        out_shape=jax.ShapeDtypeStruct((nb, nchunk, 512), f32),
        compiler_params=_cparams(("parallel", "parallel", "arbitrary")),
        name="compress",
    )(pt_flat, *([src4] * pps), wab, pos, w2, gk)


def _pattn_kernel(q_ref, kvc_ref, ksel_ref, kwin_ref, gate_ref, cbt_ref, tzt_ref, ovt_ref, ext_ref, out_ref,
                  kb_sel, vat_sel, kb_win, vat_win, kcb, vcat, sb_sc, gt_sc, m_sc, acc_sel, acc_win, *, tq, t):
    h = pl.program_id(1)
    i = pl.program_id(2)
    c4 = GQA * tq
    nsel = t // SEL_BLOCK
    ntile = t // tq
    nwin = WINDOW // tq
    ncc = t // CMP_STRIDE
    ksel = min(SEL_TOPK, nsel)

    @pl.when(i == 0)
    def _():
        lane = lax.broadcasted_iota(jnp.int32, (1, 128), 1)
        for src, kb, vat in ((ksel_ref, kb_sel, vat_sel), (kwin_ref, kb_win, vat_win)):
            blk = src[...]
            kb[...] = blk[:, :64].astype(bf16)
            va = jnp.where(lane < 64, pltpu.roll(blk, 64, 1), 1.0)
            for jj in range(ntile):
                vat[jj] = jnp.transpose(va[jj * tq:(jj + 1) * tq]).astype(bf16)
        kvc = kvc_ref[...]
        kcb[...] = kvc[:, :64].astype(bf16)
        vcat[...] = jnp.transpose(jnp.where(lane < 64, pltpu.roll(kvc, 64, 1), 0.0)).astype(bf16)

    qt = jnp.transpose(q_ref[...].astype(f32))
    q4t = jnp.concatenate([qt[g * 64:(g + 1) * 64] for g in range(GQA)], axis=1).astype(bf16)
    gt_sc[...] = jnp.transpose(gate_ref[...])

    def lanes4(a):
        return jnp.concatenate([a] * GQA, axis=1)

    nrow = lax.broadcasted_iota(jnp.int32, (ncc, tq), 0)
    qpos = i * tq + lax.broadcasted_iota(jnp.int32, (ncc, tq), 1)
    cvis = lanes4((CMP_STRIDE * nrow + CMP_BLOCK - 1 <= qpos) & (nrow < (t - CMP_BLOCK) // CMP_STRIDE + 1))
    shift = pl.multiple_of((tq // CMP_STRIDE) * (ntile - 1 - i), tq // CMP_STRIDE)
    cb = jnp.concatenate([cbt_ref[g, pl.ds(shift, ncc), :] for g in range(GQA)], axis=1)
    st = jnp.where(cvis, _dot(kcb[...], q4t) + cb, NEG)
    m = jnp.max(st, axis=0, keepdims=True)
    e = jnp.where(cvis, jnp.exp2(st - m), 0.0)
    p = e / jnp.maximum(jnp.sum(e, axis=0, keepdims=True), 1e-30)
    o_cmp = _dot(vcat[...], p.astype(bf16))
    pg = p[:, 0:tq] + p[:, tq:2 * tq] + p[:, 2 * tq:3 * tq] + p[:, 3 * tq:4 * tq]
    ovt = ovt_ref[...]
    hi, mid, lo = _split3(pg)
    score = (_dot(ovt, hi) + _dot(ovt, mid) + _dot(ovt, lo))[0:nsel]

    blk = lax.broadcasted_iota(jnp.int32, (nsel, tq), 0)
    qp = i * tq + lax.broadcasted_iota(jnp.int32, (nsel, tq), 1)
    cur = qp // SEL_BLOCK
    forced = (blk == 0) | (blk == cur) | (blk == cur - 1)
    score = jnp.where(forced, 1e6, jnp.where(blk * SEL_BLOCK > qp, -1.0, score))
    rank = jnp.zeros((nsel, tq), f32)
    for sp in range(nsel):
        other = score[sp:sp + 1, :]
        ahead = (other > score) | ((other == score) & (blk > sp))
        rank = rank + jnp.where(ahead, 1.0, 0.0)
    selneg = jnp.where(rank < ksel, 0.0, NEG)
    selneg = jnp.concatenate([selneg, jnp.zeros((128 - nsel, tq), f32)], axis=0).astype(bf16)
    for jj in range(ntile):
        sb_sc[jj] = _dot(ext_ref[jj * tq:(jj + 1) * tq, :], selneg)

    def flash(kb, vat, acc, tiles):
        sts = []
        for j, bias in tiles:
            k0 = pl.multiple_of(j * tq, tq)
            st = _dot(kb[pl.ds(k0, tq), :], q4t)
            sts.append(st if bias is None else st + bias)
        m_old = m_sc[...]
        m_new = m_old
        for st in sts:
            m_new = jnp.maximum(m_new, jnp.max(st, axis=0, keepdims=True))
        upd = acc[...] * jnp.exp2(m_old - m_new)
        for (j, _), st in zip(tiles, sts):
            upd = upd + _dot(vat[j], jnp.exp2(st - m_new).astype(bf16))
        acc[...] = upd
        m_sc[...] = m_new

    def sel_mask(j):
        return lanes4(sb_sc[j])

    m_sc[...] = jnp.full((1, c4), NEG, f32)
    acc_sel[...] = jnp.zeros((128, c4), f32)
    nfar = jnp.maximum(i - 1, 0)

    def sel_far(p, c):
        flash(kb_sel, vat_sel, acc_sel, [(2 * p, sel_mask(2 * p)), (2 * p + 1, sel_mask(2 * p + 1))])
        return c

    lax.fori_loop(0, nfar // 2, sel_far, 0)

    @pl.when(nfar % 2 == 1)
    def _():
        flash(kb_sel, vat_sel, acc_sel, [(nfar - 1, sel_mask(nfar - 1))])

    @pl.when(i >= 1)
    def _():
        flash(kb_sel, vat_sel, acc_sel, [(i - 1, tzt_ref[1] + sel_mask(i - 1)), (i, tzt_ref[0] + sel_mask(i))])

    @pl.when(i == 0)
    def _():
        flash(kb_sel, vat_sel, acc_sel, [(0, tzt_ref[0] + sel_mask(0))])

    m_sc[...] = jnp.full((1, c4), NEG, f32)
    acc_win[...] = jnp.zeros((128, c4), f32)

    def win_far(j, c):
        flash(kb_win, vat_win, acc_win, [(j, None)])
        return c

    lax.fori_loop(jnp.maximum(i - nwin + 1, 0), jnp.maximum(i - 1, 0), win_far, 0)

    @pl.when(i >= nwin)
    def _():
        krow = lax.broadcasted_iota(jnp.int32, (tq, tq), 0)
        qcol = lax.broadcasted_iota(jnp.int32, (tq, tq), 1)
        edge = lanes4(jnp.where(qcol <= krow, 0.0, NEG))
        flash(kb_win, vat_win, acc_win, [(i - nwin, edge), (i - 1, tzt_ref[1]), (i, tzt_ref[0])])

    @pl.when((i >= 1) & (i < nwin))
    def _():
        flash(kb_win, vat_win, acc_win, [(i - 1, tzt_ref[1]), (i, tzt_ref[0])])

    @pl.when(i == 0)
    def _():
        flash(kb_win, vat_win, acc_win, [(0, tzt_ref[0])])

    a_sel = acc_sel[...]
    a_win = acc_win[...]
    o_sel = a_sel[0:64] / a_sel[64:65]
    o_win = a_win[0:64] / a_win[64:65]

    def grow(branch):
        return jnp.concatenate([gt_sc[pl.ds(branch * N_HEADS_A + h * GQA + g, 1), :] for g in range(GQA)], axis=1)

    comb = grow(0) * o_cmp[0:64] + grow(1) * o_sel + grow(2) * o_win
    out_ref[...] = jnp.concatenate([jnp.transpose(comb[:, g * tq:(g + 1) * tq]) for g in range(GQA)], axis=1)


def _pattn_call(qn, kvc, ksel, kwin, gates, cbt, tzt, ovt, ext, nb, t, tq=256):
    ntile = t // tq
    c4 = GQA * tq
    ncc = t // CMP_STRIDE
    grid = (nb, N_KV_HEADS, ntile)
    return pl.pallas_call(
        functools.partial(_pattn_kernel, tq=tq, t=t),
        grid=grid,
        in_specs=[
            pl.BlockSpec((tq, 256), lambda b, h, i: (b * ntile + i, h)),
            pl.BlockSpec((None, ncc, 128), lambda b, h, i: (b, 0, h)),
            pl.BlockSpec((t, 128), lambda b, h, i: (b, h)),
            pl.BlockSpec((t, 128), lambda b, h, i: (b, h)),
            pl.BlockSpec((tq, 128), lambda b, h, i: (b * ntile + i, 0)),
            pl.BlockSpec((GQA, 2 * ncc, tq), lambda b, h, i: (h, 0, 0)),
            pl.BlockSpec((None, 2, tq, c4), lambda b, h, i: (h, 0, 0, 0)),
            pl.BlockSpec((128, ncc), lambda b, h, i: (0, 0)),
            pl.BlockSpec((t, 128), lambda b, h, i: (0, 0)),
        ],
        out_specs=pl.BlockSpec((tq, 256), lambda b, h, i: (b * ntile + i, h)),
        out_shape=jax.ShapeDtypeStruct((nb * t, ATTN_WIDTH), f32),
        scratch_shapes=[
            pltpu.VMEM((t, 64), bf16), pltpu.VMEM((ntile, 128, tq), bf16),
            pltpu.VMEM((t, 64), bf16), pltpu.VMEM((ntile, 128, tq), bf16),
            pltpu.VMEM((ncc, 64), bf16), pltpu.VMEM((128, ncc), bf16),
            pltpu.VMEM((ntile, tq, tq), f32), pltpu.VMEM((128, tq), f32),
            pltpu.VMEM((1, c4), f32), pltpu.VMEM((128, c4), f32), pltpu.VMEM((128, c4), f32),
        ],
        compiler_params=_cparams(("parallel", "parallel", "arbitrary")),
        name="prompt_attn",
    )(qn, kvc, ksel, kwin, gates, cbt, tzt, ovt, ext)


CH_PAGES = 16


def _topk_neg_rows(score, rowf, k):
    neg = jnp.full(score.shape, NEG, f32)
    work = score
    for _ in range(k):
        m = jnp.max(work, axis=0, keepdims=True)
        idx = jnp.min(jnp.where(work == m, rowf, 1e9), axis=0, keepdims=True)
        hit = rowf == idx
        neg = jnp.where(hit, 0.0, neg)
        work = jnp.where(hit, -3e38, work)
    return neg


def _col_of(vec):
    return jnp.transpose(jnp.broadcast_to(vec, (128, 128)))[:, :1]


def _sattn_kernel(pt_ref, qbig_ref, kvc_ref, cbt_ref, ovt_ref, smat_ref, e2_ref, pool_ref, sblast_ref,
                  nsel_ref, nbsel_ref, wcache_ref, wbt_ref, nwin_ref, nbwin_ref, gt_ref, out_ref,
                  buf, sem, seln_sc, m_sc, l_sc, acc_sc, *, layer, npg, n_new):
    b = pl.program_id(0)
    nch = npg // CH_PAGES
    ck = CH_PAGES * PAGE_SIZE
    past = npg * PAGE_SIZE
    qbig = qbig_ref[...]

    def page_copy(c, k, slot):
        pg = pt_ref[b * npg + c * CH_PAGES + k]
        return pltpu.make_async_copy(pool_ref.at[pg, layer], buf.at[slot, k], sem.at[slot])

    def start_chunk(c, slot):
        for k in range(CH_PAGES):
            page_copy(c, k, slot).start()

    def wait_chunk(c, slot):
        for k in range(CH_PAGES):
            page_copy(c, k, slot).wait()

    start_chunk(0, 0)
    if nch > 1:
        start_chunk(1, 1)

    def reset():
        m_sc[...] = jnp.full((1, 128), NEG, f32)
        l_sc[...] = jnp.zeros((1, 128), f32)
        acc_sc[...] = jnp.zeros((128, 512), f32)

    def attend(rows, bias_t):
        rb = rows.astype(bf16)
        st = _dot(rb, qbig) + bias_t
        m_old = m_sc[...]
        m_new = jnp.maximum(m_old, jnp.max(st, axis=0, keepdims=True))
        alpha = jnp.exp2(m_old - m_new)
        pt = jnp.exp2(st - m_new)
        l_sc[...] = l_sc[...] * alpha + jnp.sum(pt, axis=0, keepdims=True)
        acc_sc[...] = acc_sc[...] * _col_of(alpha) + _dot(jnp.transpose(pt).astype(bf16), rb)
        m_sc[...] = m_new

    rowhead = lax.broadcasted_iota(jnp.int32, (128, 128), 0) // 32

    def own_head(res):
        out = jnp.zeros((128, 128), f32)
        for hh in range(N_KV_HEADS):
            out = jnp.where(rowhead == hh, res[:, hh * 128:(hh + 1) * 128], out)
        return out

    kvc = kvc_ref[...]
    kb = kvc.astype(bf16)
    st = _dot(kb, qbig) + cbt_ref[...]
    m = jnp.max(st, axis=0, keepdims=True)
    e = jnp.exp2(st - m)
    pt = e / jnp.maximum(jnp.sum(e, axis=0, keepdims=True), 1e-30)
    o_cmp = own_head(_dot(jnp.transpose(pt).astype(bf16), kb))
    smat = smat_ref[...]
    hi, mid, lo = _split3(pt)
    pg = _dot(hi, smat) + _dot(mid, smat) + _dot(lo, smat)
    hi, mid, lo = _split3(pg)
    ovt = ovt_ref[...]
    score = _dot(ovt, hi) + _dot(ovt, mid) + _dot(ovt, lo)
    nrow = score.shape[0]
    blk = lax.broadcasted_iota(jnp.int32, (nrow, 128), 0)
    qpos = past + lax.broadcasted_iota(jnp.int32, (nrow, 128), 1) % 8
    cur = qpos // SEL_BLOCK
    forced = (blk == 0) | (blk == cur) | (blk == cur - 1)
    future = blk * SEL_BLOCK > qpos
    score = jnp.where(forced, 1e6, jnp.where(future, -1.0, score))
    nsel = -(-(past + n_new) // SEL_BLOCK)
    score = jnp.where(blk < nsel, score, -2.0)
    seln_sc[0:nrow, :] = _topk_neg_rows(score, blk.astype(f32), min(SEL_TOPK, nsel))
    seln_sc[nrow:, :] = jnp.zeros((seln_sc.shape[0] - nrow, 128), f32)

    reset()
    bpc = ck // SEL_BLOCK

    def sel_bias(c):
        r0 = pl.multiple_of(c * bpc, bpc)
        return _dot(e2_ref[...], seln_sc[pl.ds(r0, 128), :].astype(bf16))

    def chunk_body(c, carry):
        slot = c % 2
        wait_chunk(c, slot)
        attend(buf[slot].reshape(ck, 512), sel_bias(c))

        @pl.when(c + 2 < nch)
        def _():
            start_chunk(c + 2, slot)

        return carry

    lax.fori_loop(0, nch - 1, chunk_body, 0)
    last = nch - 1
    wait_chunk(last, last % 2)
    attend(buf[last % 2].reshape(ck, 512), sel_bias(last) + sblast_ref[...])
    attend(nsel_ref[...], nbsel_ref[...] + seln_sc[past // SEL_BLOCK:past // SEL_BLOCK + 1, :])
    o_sel = own_head(acc_sc[...]) / _col_of(l_sc[...])

    reset()
    attend(wcache_ref[...], wbt_ref[...])
    attend(nwin_ref[...], nbwin_ref[...])
    o_win = own_head(acc_sc[...]) / _col_of(l_sc[...])

    gts = gt_ref[...]
    out_ref[...] = gts[:, 0:1] * o_cmp + gts[:, 1:2] * o_sel + gts[:, 2:3] * o_win


def _sattn_call(pt_flat, qbig, kvc, cbt, ovt, smat, e2, pool, sblast, nsel, nbsel, wcache, wbt, nwin, nbwin,
                gt, layer, nb, npg, n_new):
    nrow = ovt.shape[0]
    ck = CH_PAGES * PAGE_SIZE

    def full(shape):
        return pl.BlockSpec(shape, lambda b, pt: (0,) * len(shape))

    def perb(shape):
        return pl.BlockSpec((None,) + shape, lambda b, pt: (b,) + (0,) * len(shape))

    return pl.pallas_call(
        functools.partial(_sattn_kernel, layer=layer, npg=npg, n_new=n_new),
        grid_spec=pltpu.PrefetchScalarGridSpec(
            num_scalar_prefetch=1,
            grid=(nb,),
            in_specs=[perb((512, 128)), perb((npg * 8, 512)), full(cbt.shape), full(ovt.shape), full((128, 128)),
                      full((ck, 128)), pl.BlockSpec(memory_space=pl.ANY), full((ck, 128)),
                      perb((128, 512)), full((128, 128)),
                      pl.BlockSpec((None, None, WINDOW, 512), lambda b, pt: (b, layer, 0, 0)), full((WINDOW, 128)),
                      perb((128, 512)), full((128, 128)), perb((128, 128))],
            out_specs=perb((128, 128)),
            scratch_shapes=[pltpu.VMEM((2, CH_PAGES, PAGE_SIZE, 512), f32), pltpu.SemaphoreType.DMA((2,)),
                            pltpu.VMEM((nrow + 128, 128), f32), pltpu.VMEM((1, 128), f32), pltpu.VMEM((1, 128), f32),
                            pltpu.VMEM((128, 512), f32)],
        ),
        out_shape=jax.ShapeDtypeStruct((nb, 128, 128), f32),
        compiler_params=_cparams(("arbitrary",)),
        name="sample_attn",
    )(pt_flat, qbig, kvc, cbt, ovt, smat, e2, pool, sblast, nsel, nbsel, wcache, wbt, nwin, nbwin, gt)


def _s5_kernel(*refs, nb, tc):
    u_refs = refs[:nb]
    (h0r_ref, h0i_ref, abr_ref, abi_ref, bre_ref, bim_ref, cre_ref, cim_ref, d_ref, gw_ref, gb_ref,
     o_ref, hr_out, hi_out, xr, xi, hr, hi) = refs[nb:]
    c = pl.program_id(0)

    @pl.when(c == 0)
    def _():
        hr[...] = h0r_ref[...]
        hi[...] = h0i_ref[...]

    for b in range(nb):
        u = u_refs[b][...].astype(bf16)
        for cb in range(4):
            ub = u[:, cb * 128:(cb + 1) * 128]
            pr = _dot(ub, bre_ref[cb])
            pi = _dot(ub, bim_ref[cb])
            for k in range(4):
                xr[cb * 4 + k, b * tc:(b + 1) * tc, :] = pr[:, k * 128:(k + 1) * 128]
                xi[cb * 4 + k, b * tc:(b + 1) * tc, :] = pi[:, k * 128:(k + 1) * 128]

    for lc in range(4):
        slabs = [lc * 4 + k for k in range(4)]
        ar = [abr_ref[0:nb, j * 128:(j + 1) * 128] for j in slabs]
        ai = [abi_ref[0:nb, j * 128:(j + 1) * 128] for j in slabs]

        def body(t, carry, slabs=slabs, ar=ar, ai=ai):
            rows = pl.ds(t, nb, stride=tc)
            new = []
            for k, j in enumerate(slabs):
                cr, ci = carry[2 * k], carry[2 * k + 1]
                nr = ar[k] * cr - ai[k] * ci + xr[j, rows, :]
                ni = ar[k] * ci + ai[k] * cr + xi[j, rows, :]
                xr[j, rows, :] = nr
                xi[j, rows, :] = ni
                new += [nr, ni]
            return tuple(new)

        init = []
        for j in slabs:
            init += [hr[0:nb, j * 128:(j + 1) * 128], hi[0:nb, j * 128:(j + 1) * 128]]
        fin = lax.fori_loop(0, tc, body, tuple(init), unroll=4)
        for k, j in enumerate(slabs):
            hr[0:nb, j * 128:(j + 1) * 128] = fin[2 * k]
            hi[0:nb, j * 128:(j + 1) * 128] = fin[2 * k + 1]

    ys = []
    for cb in range(4):
        hre = jnp.concatenate([xr[cb * 4 + k] for k in range(4)], axis=1).astype(bf16)
        him = jnp.concatenate([xi[cb * 4 + k] for k in range(4)], axis=1).astype(bf16)
        ys.append(_dot(hre, cre_ref[cb]) - _dot(him, cim_ref[cb]))
    u_all = jnp.concatenate([u_refs[b][...] for b in range(nb)], axis=0)
    y = _gelu(jnp.concatenate(ys, axis=1) + d_ref[...] * u_all)
    o = y * _sigmoid(_dot(y.astype(bf16), gw_ref[...]) + gb_ref[...])
    for b in range(nb):
        o_ref[b] = o[b * tc:(b + 1) * tc]

    @pl.when(c == pl.num_programs(0) - 1)
    def _():
        hr_out[...] = hr[...]
        hi_out[...] = hi[...]


def _s5_call(proj, row0, nb, t, tc, h0r, h0i, abr, abi, bre, bim, cre, cim, d, gw, gb):
    def full(shape):
        return pl.BlockSpec(shape, lambda c: (0,) * len(shape))

    u_specs = [pl.BlockSpec((tc, 512), lambda c, b=b: ((row0 + b * t) // tc + c, 5)) for b in range(nb)]
    return pl.pallas_call(
        functools.partial(_s5_kernel, nb=nb, tc=tc),
        grid=(t // tc,),
        in_specs=u_specs + [full((8, 2048)), full((8, 2048)), full((8, 2048)), full((8, 2048)),
                            full((4, 128, 512)), full((4, 128, 512)), full((4, 512, 128)), full((4, 512, 128)),
                            full((1, 512)), full((512, 512)), full((1, 512))],
        out_specs=[pl.BlockSpec((nb, tc, 512), lambda c: (0, c, 0)), full((8, 2048)), full((8, 2048))],
        out_shape=[jax.ShapeDtypeStruct((nb, t, 512), f32), jax.ShapeDtypeStruct((8, 2048), f32),
                   jax.ShapeDtypeStruct((8, 2048), f32)],
        scratch_shapes=[pltpu.VMEM((16, nb * tc, 128), f32), pltpu.VMEM((16, nb * tc, 128), f32),
                        pltpu.VMEM((8, 2048), f32), pltpu.VMEM((8, 2048), f32)],
        compiler_params=_cparams(("arbitrary",)),
        name="s5",
    )(*([proj] * nb), h0r, h0i, abr, abi, bre, bim, cre, cim, d, gw, gb)


def _gmlp_kernel(u_ref, v_ref, w_ref, bs_ref, o_ref, *, rows):
    u = u_ref[...]
    v = v_ref[...]
    outs = []
    for g in range(GMLP_GROUPS):
        vg = v[:, g * 128:(g + 1) * 128]
        if rows < CHUNK:
            vg = jnp.concatenate([vg, jnp.zeros((CHUNK - rows, 128), f32)], axis=0)
        mixed = _dot(w_ref[g][0:rows, :], vg.astype(bf16)) + bs_ref[0:rows, g:g + 1]
        outs.append(u[:, g * 128:(g + 1) * 128] * mixed)
    o_ref[...] = jnp.concatenate(outs, axis=1)


def _gmlp_call(uga, vn, w, bs, row0, nrows, rows):
    blk0 = row0 // rows

    def full(shape):
        return pl.BlockSpec(shape, lambda i: (0,) * len(shape))

    return pl.pallas_call(
        functools.partial(_gmlp_kernel, rows=rows),
        grid=(nrows // rows,),
        in_specs=[pl.BlockSpec((rows, 512), lambda i: (blk0 + i, 0)), pl.BlockSpec((rows, 512), lambda i: (blk0 + i, 0)),
                  full((GMLP_GROUPS, CHUNK, CHUNK)), full((CHUNK, 128))],
        out_specs=pl.BlockSpec((rows, 512), lambda i: (i, 0)),
        out_shape=jax.ShapeDtypeStruct((nrows, 512), f32),
        compiler_params=_cparams(("parallel",)),
        name="gmlp",
    )(uga, vn, w, bs)


def _merge_kernel(*refs, has_y):
    nres = 3 if has_y else 1
    oa_ref, ob_ref, oc_ref, gn_ref, w_ref, gf_ref, rwh_ref, rwl_ref, rb_ref, x1_ref, xn_ref, rt_ref = refs[nres:]

    def rms(v, g):
        return (v * lax.rsqrt(jnp.mean(v * v, axis=-1, keepdims=True) + EPS) * g).astype(bf16)

    acc = _dot(rms(oa_ref[...], gn_ref[:, 0:1024]), w_ref[0:1024, :])
    acc += _dot(rms(ob_ref[...], gn_ref[:, 1024:1536]), w_ref[1024:1536, :])
    acc += _dot(rms(oc_ref[...], gn_ref[:, 1536:2048]), w_ref[1536:2048, :])
    x1 = _residual(refs, has_y) + acc
    x1_ref[...] = x1
    xn = x1 * lax.rsqrt(jnp.mean(x1 * x1, axis=-1, keepdims=True) + EPS) * gf_ref[...]
    xn_ref[...] = xn
    hi = xn.astype(bf16)
    lo = (xn - hi.astype(f32)).astype(bf16)
    logits = _dot(hi, rwh_ref[...]) + _dot(lo, rwh_ref[...]) + _dot(hi, rwl_ref[...]) + rb_ref[...]
    lane = lax.broadcasted_iota(jnp.int32, logits.shape, 1)
    lanef = lane.astype(f32)
    is_g = lane < MOE_GROUPS
    gl = jnp.where(is_g, logits, NEG)
    gm = jnp.max(gl, axis=-1, keepdims=True)
    gidx = jnp.min(jnp.where(gl == gm, lanef, 1e9), axis=-1, keepdims=True)
    gprob = 1.0 / jnp.sum(jnp.where(is_g, jnp.exp(logits - gm), 0.0), axis=-1, keepdims=True)
    lo_lane = MOE_GROUPS + EXPERTS_PER_GROUP * gidx
    inl = jnp.where((lanef >= lo_lane) & (lanef < lo_lane + EXPERTS_PER_GROUP), logits, NEG)
    v1 = jnp.max(inl, axis=-1, keepdims=True)
    i1 = jnp.min(jnp.where(inl == v1, lanef, 1e9), axis=-1, keepdims=True)
    inl2 = jnp.where(lanef == i1, NEG, inl)
    v2 = jnp.max(inl2, axis=-1, keepdims=True)
    i2 = jnp.min(jnp.where(inl2 == v2, lanef, 1e9), axis=-1, keepdims=True)
    e2 = jnp.exp(v2 - v1)
    w1 = gprob / (1.0 + e2)
    w2 = gprob * e2 / (1.0 + e2)
    rt_ref[...] = jnp.where(lane == 0, i1 - MOE_GROUPS,
                            jnp.where(lane == 1, i2 - MOE_GROUPS,
                                      jnp.where(lane == 2, w1, jnp.where(lane == 3, w2, 0.0))))


def _merge_call(x, y2, oa, ob, oc, gn, w, gf, rwh, rwl, rb, tm=320):
    nt = x.shape[0]
    res_specs, res_args = _residual_specs(x, y2, tm, lambda i: i)

    def row(width):
        return pl.BlockSpec((tm, width), lambda i: (i, 0))

    def full(shape):
        return pl.BlockSpec(shape, lambda i: (0,) * len(shape))

    return pl.pallas_call(
        functools.partial(_merge_kernel, has_y=y2 is not None),
        grid=(nt // tm,),
        in_specs=res_specs + [row(1024), row(512), row(512), full((1, 2048)), full((2048, 2048)), full((1, 2048)),
                              full((2048, 128)), full((2048, 128)), full((1, 128))],
        out_specs=[row(2048), row(2048), row(128)],
        out_shape=[jax.ShapeDtypeStruct((nt, 2048), f32), jax.ShapeDtypeStruct((nt, 2048), f32),
                   jax.ShapeDtypeStruct((nt, 128), f32)],
        compiler_params=_cparams(("parallel",)),
        name="merge_router",
    )(*res_args, oa, ob, oc, gn, w, gf, rwh, rwl, rb)


def _expert_kernel(te_ref, nu_ref, src_ref, dst_ref, xn_hbm, wg_ref, w1_ref, w3_ref, w2_ref, y_hbm,
                   xbuf, obuf, w1b, w3b, w2b, gsem, ssem, *, tm, ntiles):
    t = pl.program_id(0)
    n_used = nu_ref[0]
    slot = t % 2

    def dma_priority(r):
        return r % 2 if isinstance(r, int) else 0

    def gather_row(tt, sl, r):
        tok = src_ref[tt * tm + r]
        pltpu.make_async_copy(xn_hbm.at[pl.ds(tok, 1)], xbuf.at[sl, pl.ds(r, 1)],
                              gsem.at[sl]).start(priority=dma_priority(r))

    def gather_wait(sl):
        pltpu.make_async_copy(xn_hbm.at[pl.ds(0, tm)], xbuf.at[sl], gsem.at[sl]).wait()

    def scatter_row(tt, sl, r):
        row = dst_ref[tt * tm + r]
        pltpu.make_async_copy(obuf.at[sl, pl.ds(r, 1)], y_hbm.at[pl.ds(row, 1)],
                              ssem.at[sl]).start(priority=dma_priority(r))

    def scatter_wait(sl):
        pltpu.make_async_copy(obuf.at[sl], y_hbm.at[pl.ds(0, tm)], ssem.at[sl]).wait()

    def rolled(fn, tt, sl):
        def body(r, c):
            fn(tt, sl, r)
            return c
        lax.fori_loop(0, tm, body, 0, unroll=8)

    @pl.when(t == 0)
    def _():
        rolled(gather_row, 0, 0)
        npair = y_hbm.shape[0] - 2 * tm
        for sl in range(2):
            obuf[sl] = jnp.zeros((tm, D_MODEL), f32)
            cp = pltpu.make_async_copy(obuf.at[sl], y_hbm.at[pl.ds(npair + sl * tm, tm)], ssem.at[sl])
            cp.start()
            cp.wait()

    def tile_step(with_scatter):
        gather_wait(slot)

        @pl.when(t >= 3)
        def _():
            scatter_wait(t % 3)

        @pl.when((t == 0) | (te_ref[t] != te_ref[jnp.maximum(t - 1, 0)]))
        def _():
            w1b[...] = w1_ref[...].astype(bf16)
            w3b[...] = w3_ref[...].astype(bf16)
            w2b[...] = w2_ref[...].astype(bf16)

        nxt = jnp.minimum(t + 1, ntiles - 1)
        for r in range(tm):
            gather_row(nxt, 1 - slot, r)
        x = xbuf[slot].astype(bf16)
        a = _dot(x, w1b[...])
        hmid = a * _sigmoid(a) * _dot(x, w3b[...]) * wg_ref[...]
        if with_scatter:
            prev_slot = (t - 1) % 3
            for r in range(tm):
                scatter_row(t - 1, prev_slot, r)
        obuf[t % 3] = _dot(hmid.astype(bf16), w2b[...])

    @pl.when((t < n_used) & (t == 0))
    def _():
        tile_step(False)

    @pl.when((t < n_used) & (t >= 1))
    def _():
        tile_step(True)

    @pl.when(t == ntiles - 1)
    def _():
        last = n_used - 1
        gather_wait(n_used % 2)

        @pl.when(last >= 2)
        def _():
            scatter_wait((last - 2) % 3)

        @pl.when(last >= 1)
        def _():
            scatter_wait((last - 1) % 3)

        rolled(scatter_row, last, last % 3)
        scatter_wait(last % 3)


def _expert_call(tile_e, n_used, src_tok, dst_row, xn, wgt, w1, w3, w2, layer, tm, ntiles, nrows_out):
    def wmap(t, te, nu, src, dst):
        return (layer, te[t], 0, 0)

    return pl.pallas_call(
        functools.partial(_expert_kernel, tm=tm, ntiles=ntiles),
        grid_spec=pltpu.PrefetchScalarGridSpec(
            num_scalar_prefetch=4,
            grid=(ntiles,),
            in_specs=[pl.BlockSpec(memory_space=pl.ANY),
                      pl.BlockSpec((tm, 1), lambda t, te, nu, src, dst: (t, 0)),
                      pl.BlockSpec((None, None, D_MODEL, D_EXPERT), wmap),
                      pl.BlockSpec((None, None, D_MODEL, D_EXPERT), wmap),
                      pl.BlockSpec((None, None, D_EXPERT, D_MODEL), wmap)],
            out_specs=pl.BlockSpec(memory_space=pl.ANY),
            scratch_shapes=[pltpu.VMEM((2, tm, D_MODEL), f32), pltpu.VMEM((3, tm, D_MODEL), f32),
                            pltpu.VMEM((D_MODEL, D_EXPERT), bf16), pltpu.VMEM((D_MODEL, D_EXPERT), bf16),
                            pltpu.VMEM((D_EXPERT, D_MODEL), bf16),
                            pltpu.SemaphoreType.DMA((2,)), pltpu.SemaphoreType.DMA((3,))],
        ),
        out_shape=jax.ShapeDtypeStruct((nrows_out, D_MODEL), f32),
        compiler_params=_cparams(("arbitrary",), disable_bounds_checks=True),
        name="experts",
    )(tile_e, n_used, src_tok, dst_row, xn, wgt, w1, w3, w2)


def _add3_kernel(x_ref, ya_ref, yb_ref, o_ref):
    o_ref[...] = x_ref[...] + ya_ref[...] + yb_ref[...]


def _add3_call(x, y2, tm=640):
    nt = x.shape[0]
    nrow = nt // tm
    return pl.pallas_call(
        _add3_kernel,
        grid=(nrow,),
        in_specs=[pl.BlockSpec((tm, D_MODEL), lambda i: (i, 0)), pl.BlockSpec((tm, D_MODEL), lambda i: (i, 0)),
                  pl.BlockSpec((tm, D_MODEL), lambda i: (i + nrow, 0))],
        out_specs=pl.BlockSpec((tm, D_MODEL), lambda i: (i, 0)),
        out_shape=jax.ShapeDtypeStruct((nt, D_MODEL), f32),
        compiler_params=_cparams(("parallel",)),
        name="residual_add",
    )(x, y2, y2)


EXPERT_TM = 256
PROMPT_TQ = 256


def _bucket(n):
    max_exact = NUM_BUCKETS // 2
    nf = jnp.maximum(n, max_exact).astype(f32)
    large = max_exact + (jnp.log(nf / max_exact) / math.log(MAX_DISTANCE / max_exact)
                         * (NUM_BUCKETS - max_exact)).astype(jnp.int32)
    return jnp.where(n < max_exact, n, jnp.minimum(large, NUM_BUCKETS - 1))


def _bias_tables(rel_bias, tq, t, past, n_new):
    bd = (rel_bias[_bucket(jnp.arange(128))] - rel_bias[NUM_BUCKETS - 1][None, :]) * LOG2E

    def look(dist):
        oh = jax.nn.one_hot(jnp.clip(dist, 0, 127), 128, dtype=f32)
        return jnp.einsum('...d,dh->...h', oh, bd, precision=lax.Precision.HIGHEST)

    ntile = t // tq
    r = jnp.arange(tq)
    d0 = r[:, None] - r[None, :]
    t0 = jnp.where((d0 >= 0)[..., None], look(d0), NEG)
    t1 = look(tq + d0)
    tz = jnp.stack([t0, t1], axis=0).reshape(2, tq, tq, N_KV_HEADS, GQA)
    tz = jnp.transpose(tz, (3, 0, 2, 4, 1)).reshape(N_KV_HEADS, 2, tq, GQA * tq)
    n = jnp.arange(2 * (t // CMP_STRIDE))
    cbl = jnp.transpose(look(tq * (ntile - 1) + r[None, :] - CMP_STRIDE * n[:, None] - (CMP_BLOCK - 1)), (2, 0, 1))

    qq = jnp.arange(8)

    def look_t(dist, vis):
        tab = jnp.where(vis[:, None, :], jnp.transpose(look(dist), (0, 2, 1)), NEG)
        return tab.reshape(dist.shape[0], N_HEADS_A * 8)

    n_cmp_s = (past + n_new - CMP_BLOCK) // CMP_STRIDE + 1
    nchunk = past // CMP_STRIDE
    nn = jnp.arange(nchunk)[:, None]
    dist = past + qq[None, :] - (CMP_STRIDE * nn + CMP_BLOCK - 1)
    cbt = look_t(dist, (dist >= 0) & (nn < n_cmp_s))
    ck = CH_PAGES * PAGE_SIZE
    rr = jnp.arange(ck - 128, ck)[:, None]
    dist = qq[None, :] + ck - rr
    sblast = jnp.concatenate([jnp.zeros((ck - 128, 128), f32), look_t(dist, dist >= 0)], axis=0)
    r128 = jnp.arange(128)[:, None]
    dist = qq[None, :] - r128
    nbnew = look_t(dist, (dist >= 0) & (r128 < n_new))
    rw = jnp.arange(WINDOW)[:, None]
    dist = WINDOW + qq[None, :] - rw
    wbt = look_t(dist, (dist >= 0) & (dist <= WINDOW))
    return tz, cbl, cbt, sblast, nbnew, wbt


def _static_mats(t, past, n_new):
    n = np.arange(128)
    n_cmp = (t - CMP_BLOCK) // CMP_STRIDE + 1
    s = np.arange(128)
    ov = ((CMP_STRIDE * n[:, None] < SEL_BLOCK * s[None, :] + SEL_BLOCK)
          & (CMP_STRIDE * n[:, None] + CMP_BLOCK > SEL_BLOCK * s[None, :])
          & (n[:, None] < n_cmp) & (s[None, :] < t // SEL_BLOCK))
    ex = (np.arange(t)[None, :] // SEL_BLOCK == s[:, None])
    nsel_s = -(-(past + n_new) // SEL_BLOCK)
    nrow = -(-nsel_s // 8) * 8
    n_cmp_s = (past + n_new - CMP_BLOCK) // CMP_STRIDE + 1
    ss = np.arange(nrow)[:, None]
    ns = np.arange(past // CMP_STRIDE)[None, :]
    ovt = ((CMP_STRIDE * ns < SEL_BLOCK * ss + SEL_BLOCK) & (CMP_STRIDE * ns + CMP_BLOCK > SEL_BLOCK * ss)
           & (ns < n_cmp_s) & (ss < nsel_s))
    c = np.arange(128)
    smat = (c[:, None] // 32 == c[None, :] // 32) & (c[:, None] % 8 == c[None, :] % 8)
    ck = CH_PAGES * PAGE_SIZE
    e2 = (np.arange(ck)[:, None] // SEL_BLOCK == np.arange(128)[None, :])
    cvt = lambda a: jnp.asarray(a.astype(np.float32), dtype=bf16)
    return cvt(ov.T), cvt(ex.T), cvt(ovt), cvt(smat), cvt(e2)


def _route_metadata(route, nt, tm, ntiles):
    e_flat = jnp.concatenate([route[:, 0], route[:, 1]]).astype(jnp.int32)
    w_flat = jnp.concatenate([route[:, 2], route[:, 3]])
    npair = 2 * nt
    order = jnp.argsort(e_flat, stable=True).astype(jnp.int32)
    counts = jnp.sum(jax.nn.one_hot(e_flat, N_EXPERTS, dtype=jnp.int32), axis=0)
    tiles_e = (counts + tm - 1) // tm
    tend = jnp.cumsum(tiles_e)
    tstart = tend - tiles_e
    cstart = jnp.cumsum(counts) - counts
    n_used = tend[-1]
    tidx = jnp.arange(ntiles, dtype=jnp.int32)
    tile_e = jnp.sum((jnp.minimum(tidx, n_used - 1)[:, None] >= tend[None, :]).astype(jnp.int32), axis=1)
    tile_oh = jax.nn.one_hot(tile_e, N_EXPERTS, dtype=jnp.int32)
    t_cnt = jnp.sum(tile_oh * counts[None, :], axis=1)
    t_first = jnp.sum(tile_oh * (cstart - tstart * tm)[None, :], axis=1) + tidx * tm
    rows = jnp.arange(tm, dtype=jnp.int32)[None, :]
    rank = tidx[:, None] * tm + rows - jnp.sum(tile_oh * tstart[None, :], axis=1)[:, None] * tm
    valid = (rank < t_cnt[:, None]) & (tidx[:, None] < n_used)
    pair = order[jnp.clip(t_first[:, None] + rows, 0, npair - 1)]
    dump = npair + (tidx[:, None] % 2) * tm + rows
    src_tok = jnp.where(valid, pair % nt, 0).reshape(-1)
    dst_row = jnp.where(valid, pair, dump).reshape(-1)
    wgt = jnp.where(valid, w_flat[pair], 0.0).reshape(-1, 1)
    return tile_e.astype(jnp.int32), n_used.reshape(1).astype(jnp.int32), src_tok, dst_row, wgt


def _block_diag(blocks):
    n, r, c = blocks.shape
    return jnp.einsum('grc,gk->grkc', blocks, jnp.eye(n, dtype=blocks.dtype)).reshape(n * r, n * c)


def kernel(x_prompt, x_sample, cache_kv_cmp, cache_kv_sel, cache_kv_win, state_ssm_re, state_ssm_im, page_table, rel_bias, norm_mix, w_in, qk_norm, cmp_pos, cmp_w1, cmp_w2, ssm_a_re, ssm_a_im, ssm_log_dt, ssm_b_re, ssm_b_im, ssm_c_re, ssm_c_im, ssm_d, ssm_glu_w, ssm_glu_b, gmlp_norm, gmlp_ws, gmlp_bs, out_norm, w_out, norm_ffn, router_group_w, router_group_b, router_expert_w, router_expert_b, expert_w1, expert_w3, expert_w2):
    bp, t, _ = x_prompt.shape
    bs, s_new, _ = x_sample.shape
    npg = page_table.shape[1]
    past = npg * PAGE_SIZE
    nphys = cache_kv_cmp.shape[0]
    n_p, n_s = bp * t, bs * s_new
    nt = -(-(n_p + n_s) // 640) * 640
    tq = PROMPT_TQ
    assert t % tq == 0 and WINDOW % tq == 0 and s_new == 8 and bs == 8 and npg % CH_PAGES == 0

    x = jnp.concatenate([x_prompt.reshape(n_p, D_MODEL), x_sample.reshape(n_s, D_MODEL),
                         jnp.zeros((nt - n_p - n_s, D_MODEL), f32)], axis=0)
    y2 = None
    pool_cmp = cache_kv_cmp.reshape(nphys, DEPTH, PAGE_SIZE, 512)
    pool_sel = cache_kv_sel.reshape(nphys, DEPTH, PAGE_SIZE, 512)
    wcache = cache_kv_win.reshape(bs, DEPTH, WINDOW, 512)
    pt_flat = page_table.reshape(-1).astype(jnp.int32)
    pt_ident = jnp.arange(bp * (t // PAGE_SIZE), dtype=jnp.int32)

    tz, cbl, cbt, sblast, nbnew, wbt = _bias_tables(rel_bias, tq, t, past, s_new)
    ov, ex, ovt, smat, e2 = _static_mats(t, past, s_new)
    g64 = _block_diag(jnp.ones((8, 64, 64), bf16))
    g128 = _block_diag(jnp.ones((4, 128, 128), bf16))
    ones64 = jnp.ones((HEAD_DIM,), f32)
    tril = jnp.tril(jnp.ones((CHUNK, CHUNK), f32))
    zeros_state = jnp.zeros((8, SSM_GROUPS * SSM_STATE), f32)
    ntiles = 2 * nt // EXPERT_TM + N_EXPERTS
    all_rows = lambda p, s: jnp.concatenate([p, s, jnp.zeros((nt - n_p - n_s, p.shape[1]), p.dtype)], axis=0)

    outs = {k: [] for k in ('pc', 'ps', 'pw', 'pr', 'pi', 'pv', 'sc', 'ss', 'sw', 'sr', 'si', 'sv')}
    for l in range(DEPTH):
        wl = w_in[l]
        o1, o2, o3 = ATTN_WIDTH, ATTN_WIDTH + 6 * KV_WIDTH, ATTN_WIDTH + 6 * KV_WIDTH + 3 * N_HEADS_A
        w_pad = jnp.concatenate([wl[:, :o2], wl[:, o3:], wl[:, o2:o3],
                                 jnp.zeros((D_MODEL, 128 - 3 * N_HEADS_A), f32)], axis=1).astype(bf16)
        gq = jnp.tile(qk_norm[l, 0], N_HEADS_A).reshape(1, 1024)
        gsel = jnp.tile(jnp.concatenate([qk_norm[l, 2], ones64]), N_KV_HEADS).reshape(1, 512)
        gwin = jnp.tile(jnp.concatenate([qk_norm[l, 3], ones64]), N_KV_HEADS).reshape(1, 512)
        gk = jnp.concatenate([qk_norm[l, 1], ones64]).reshape(1, 128)
        w1k = cmp_w1[l, 0].reshape(CMP_BLOCK, HEAD_DIM, HEAD_DIM)
        w1v = cmp_w1[l, 1].reshape(CMP_BLOCK, HEAD_DIM, HEAD_DIM)
        zz = jnp.zeros_like(w1k)
        wfull = jnp.concatenate([jnp.concatenate([w1k, zz], axis=2), jnp.concatenate([zz, w1v], axis=2)], axis=1)
        wab = jnp.concatenate([wfull[:16].reshape(2048, 128), wfull[16:].reshape(2048, 128)], axis=1).astype(bf16)
        posf = jnp.concatenate([cmp_pos[l, 0], cmp_pos[l, 1]], axis=1)
        pos8 = jnp.broadcast_to(jnp.concatenate([posf[:16].reshape(1, 2048), posf[16:].reshape(1, 2048)], axis=1),
                                (8, 4096))
        w2bd = _block_diag(cmp_w2[l]).astype(bf16)

        dt = jnp.exp(ssm_log_dt[l])[:, None]
        a_re, a_im = ssm_a_re[l], ssm_a_im[l]
        mag = jnp.exp(dt * a_re)
        ab_re, ab_im = mag * jnp.cos(dt * a_im), mag * jnp.sin(dt * a_im)
        den = a_re * a_re + a_im * a_im
        f_re = ((ab_re - 1.0) * a_re + ab_im * a_im) / den
        f_im = (ab_im * a_re - (ab_re - 1.0) * a_im) / den
        bb_re = f_re[..., None] * ssm_b_re[l] - f_im[..., None] * ssm_b_im[l]
        bb_im = f_re[..., None] * ssm_b_im[l] + f_im[..., None] * ssm_b_re[l]
        abr = jnp.broadcast_to(ab_re.reshape(1, -1), (8, SSM_GROUPS * SSM_STATE))
        abi = jnp.broadcast_to(ab_im.reshape(1, -1), (8, SSM_GROUPS * SSM_STATE))
        eye8 = jnp.eye(8, dtype=f32)

        def in_blocks(bb):
            xx = jnp.transpose(bb, (0, 2, 1)).reshape(4, 8, SSM_GROUP, SSM_STATE)
            return jnp.einsum('agcn,gk->agckn', xx, eye8).reshape(4, 128, 512).astype(bf16)

        def out_blocks(cc):
            yy = jnp.transpose(cc, (0, 2, 1)).reshape(4, 8, SSM_STATE, SSM_GROUP)
            return jnp.einsum('agnc,gk->agnkc', yy, eye8).reshape(4, 512, 128).astype(bf16)

        s5p = (abr, abi, in_blocks(bb_re), in_blocks(bb_im), out_blocks(ssm_c_re[l]), out_blocks(ssm_c_im[l]),
               ssm_d[l].reshape(1, 512), ssm_glu_w[l].astype(bf16), ssm_glu_b[l].reshape(1, 512))
        gw = (gmlp_ws[l] * tril).astype(bf16)
        gbs = jnp.concatenate([gmlp_bs[l].T, jnp.zeros((CHUNK, 128 - GMLP_GROUPS), f32)], axis=1)
        rw = jnp.concatenate([router_group_w[l], jnp.transpose(router_expert_w[l], (1, 0, 2)).reshape(D_MODEL, N_EXPERTS),
                              jnp.zeros((D_MODEL, 128 - MOE_GROUPS - N_EXPERTS), f32)], axis=1)
        rwh = rw.astype(bf16)
        rwl = (rw - rwh.astype(f32)).astype(bf16)
        rb = jnp.concatenate([router_group_b[l], router_expert_b[l].reshape(-1),
                              jnp.zeros((128 - MOE_GROUPS - N_EXPERTS,), f32)]).reshape(1, 128)

        proj = _proj_call(x, y2, norm_mix[l].reshape(1, D_MODEL), w_pad)
        qn, ksel, kwin, gates, uga, vn = _post_call(proj, gq, gsel, gwin, gmlp_norm[l].reshape(1, 512), g64, g128)

        src_p = proj.reshape(nt // PAGE_SIZE, 1, PAGE_SIZE, PROJ_W)
        kvc_p = _cmp_call(pt_ident, src_p, 0, ATTN_WIDTH // 128, bp, t // PAGE_SIZE, wab, pos8, w2bd, gk)
        kvc_s = _cmp_call(pt_flat, pool_cmp, l, 0, bs, npg, wab, pos8, w2bd, gk)
        oa_p = _pattn_call(qn, kvc_p, ksel, kwin, gates, cbl, tz, ov, ex, bp, t, tq)

        qs = qn[n_p:n_p + n_s].reshape(bs, s_new, N_KV_HEADS, GQA, HEAD_DIM)
        qa = jnp.transpose(qs, (0, 2, 4, 3, 1)).reshape(bs, N_KV_HEADS, HEAD_DIM, GQA * s_new)
        qbig = jnp.einsum('bhdc,hk->bhdkc', qa, jnp.eye(N_KV_HEADS, dtype=bf16))
        qbig = jnp.pad(qbig, ((0, 0), (0, 0), (0, 64), (0, 0), (0, 0))).reshape(bs, 512, 128)
        gs = gates[n_p:n_p + n_s, :48].reshape(bs, s_new, 3, N_KV_HEADS, GQA)
        gt = jnp.pad(jnp.transpose(gs, (0, 3, 4, 1, 2)).reshape(bs, 128, 3), ((0, 0), (0, 0), (0, 125)))
        new_sel = jnp.pad(ksel[n_p:n_p + n_s].reshape(bs, s_new, 512), ((0, 0), (0, 128 - s_new), (0, 0)))
        new_win = jnp.pad(kwin[n_p:n_p + n_s].reshape(bs, s_new, 512), ((0, 0), (0, 128 - s_new), (0, 0)))
        osmp = _sattn_call(pt_flat, qbig, kvc_s, cbt, ovt, smat, e2, pool_sel, sblast, new_sel, nbnew,
                           wcache, wbt, new_win, nbnew, gt, l, bs, npg, s_new)
        oa_s = jnp.transpose(osmp[:, :, 64:].reshape(bs, N_KV_HEADS, GQA, s_new, HEAD_DIM),
                             (0, 3, 1, 2, 4)).reshape(n_s, ATTN_WIDTH)
        oa = all_rows(oa_p, oa_s)

        ob_p, hr_p, hi_p = _s5_call(proj, 0, bp, t, 256, zeros_state, zeros_state, *s5p)
        ob_s, hr_s, hi_s = _s5_call(proj, n_p, bs, s_new, s_new, state_ssm_re[:, l].reshape(bs, -1),
                                    state_ssm_im[:, l].reshape(bs, -1), *s5p)
        ob = all_rows(ob_p.reshape(n_p, 512), ob_s.reshape(n_s, 512))

        oc_p = _gmlp_call(uga, vn, gw, gbs, 0, n_p, CHUNK)
        oc_s = _gmlp_call(uga, vn, gw, gbs, n_p, n_s, s_new)
        oc = all_rows(oc_p, oc_s)

        x1, xn2, route = _merge_call(x, y2, oa, ob, oc, out_norm[l].reshape(1, -1), w_out[l].astype(bf16),
                                     norm_ffn[l].reshape(1, -1), rwh, rwl, rb)
        tile_e, n_used, src_tok, dst_row, wgt = _route_metadata(route, nt, EXPERT_TM, ntiles)
        y2 = _expert_call(tile_e, n_used, src_tok, dst_row, xn2, wgt, expert_w1, expert_w3, expert_w2, l,
                          EXPERT_TM, ntiles, 2 * nt + 2 * EXPERT_TM)
        x = x1

        kvshape = (N_KV_HEADS, 2, HEAD_DIM)
        cmp_rows = proj[:, ATTN_WIDTH:ATTN_WIDTH + 512]
        outs['pc'].append(cmp_rows[:n_p].reshape(bp, t, *kvshape))
        outs['ps'].append(ksel[:n_p].reshape(bp, t, *kvshape))
        outs['pw'].append(kwin[:n_p].reshape(bp, t, *kvshape)[:, t - min(WINDOW, t):])
        outs['pr'].append(hr_p[:bp].reshape(bp, SSM_GROUPS, SSM_STATE))
        outs['pi'].append(hi_p[:bp].reshape(bp, SSM_GROUPS, SSM_STATE))
        outs['pv'].append(vn[:n_p].reshape(bp, t, GMLP_WIDTH)[:, (t - 1) // CHUNK * CHUNK:])
        outs['sc'].append(cmp_rows[n_p:n_p + n_s].reshape(bs, s_new, *kvshape))
        outs['ss'].append(ksel[n_p:n_p + n_s].reshape(bs, s_new, *kvshape))
        win_new = kwin[n_p:n_p + n_s].reshape(bs, s_new, *kvshape)
        outs['sw'].append(jnp.concatenate([cache_kv_win[:, l], win_new], axis=1)[:, s_new:])
        outs['sr'].append(hr_s[:bs].reshape(bs, SSM_GROUPS, SSM_STATE))
        outs['si'].append(hi_s[:bs].reshape(bs, SSM_GROUPS, SSM_STATE))
        outs['sv'].append(vn[n_p:n_p + n_s].reshape(bs, s_new, GMLP_WIDTH))

    xf = _add3_call(x, y2)
    st = {k: jnp.stack(v, axis=1) for k, v in outs.items()}
    return (xf[:n_p].reshape(bp, t, D_MODEL), xf[n_p:n_p + n_s].reshape(bs, s_new, D_MODEL),
            st['pc'], st['ps'], st['pw'], st['pr'], st['pi'], st['pv'],
            st['sc'], st['ss'], st['sw'], st['sr'], st['si'], st['sv'])
```
